```python
import math
import jax, jax.numpy as jnp
from jax import lax
import numpy as np

D_MODEL = 1024
BATCH = 32
SEQ = 256
DEPTH = 1
DEC_BATCH = 8
DEC_SEQ = 2048
PAST_LEN = 256

GRID_W = 64
N_HEADS = 8
HEAD_DIM = 64
V_DIM = 2 * HEAD_DIM
D_ATTN = N_HEADS * 2 * HEAD_DIM
N_POOL_GROUPS = 4
POOL_WINDOWS = (2, 4, 8, 16)
D_POOL = D_MODEL // 2
POOL_GROUP_DIM = D_POOL // N_POOL_GROUPS
D_FF = 2816
CONV_W = 3
ROPE_BASE = 10000.0
Q_BLOCK = 128
EPS = 1e-6
D_IN = D_POOL + 3 * D_ATTN + 2 * D_MODEL

kernel_name = "hybrid_pool_diffattn_prefix_dit_step"


def rms_norm(x, g):
    xf = x.astype(jnp.float32)
    y = xf * lax.rsqrt(jnp.mean(xf * xf, axis=-1, keepdims=True) + EPS)
    return (y * g.astype(jnp.float32)).astype(x.dtype)


def modulate(x, g, shift, scale):
    return rms_norm(x, g) * (1 + scale[:, None, :]) + shift[:, None, :]


def rope_2d(x):
    n_tok = x.shape[1]
    rows = n_tok // GRID_W
    row = jnp.repeat(jnp.arange(rows, dtype=jnp.float32), GRID_W)
    col = jnp.tile(jnp.arange(GRID_W, dtype=jnp.float32), rows)
    n_freq = HEAD_DIM // 4
    inv = ROPE_BASE ** (-jnp.arange(n_freq, dtype=jnp.float32) / n_freq)
    ang = jnp.stack([row[:, None] * inv, col[:, None] * inv], axis=1)
    cos = jnp.cos(ang)[None, :, None, None]
    sin = jnp.sin(ang)[None, :, None, None]
    xr = x.astype(jnp.float32).reshape(x.shape[:-1] + (2, 2, n_freq))
    x1 = xr[..., 0, :]
    x2 = xr[..., 1, :]
    out = jnp.stack([x1 * cos - x2 * sin, x2 * cos + x1 * sin], axis=-2)
    return out.reshape(x.shape).astype(x.dtype)


def multi_pool(u, w_pool, pool_scale):
    B, S, _ = u.shape
    ug = u.astype(jnp.float32).reshape(B, S, N_POOL_GROUPS, POOL_GROUP_DIM)
    cs = jnp.concatenate([jnp.zeros((B, 1, N_POOL_GROUPS, POOL_GROUP_DIM), jnp.float32),
                          jnp.cumsum(ug, axis=1)], axis=1)
    t = jnp.arange(S)
    outs = []
    for gi, w in enumerate(POOL_WINDOWS):
        lo = jnp.clip(t - w // 2, 0, S)
        hi = jnp.clip(t + (w - w // 2), 0, S)
        csg = cs[:, :, gi]
        s = jnp.take(csg, hi, axis=1) - jnp.take(csg, lo, axis=1)
        cnt = (hi - lo).astype(jnp.float32)[None, :, None]
        outs.append(s / cnt - ug[:, :, gi])
    pooled = jnp.stack(outs, axis=2)
    mixed = jnp.einsum('bsgc,gce->bsge', pooled, w_pool.astype(jnp.float32)).reshape(B, S, D_POOL)
    return (mixed * pool_scale.astype(jnp.float32)).astype(u.dtype)


def diff_attention(q, k, v, lam):
    B, S = q.shape[0], q.shape[1]
    nb = S // Q_BLOCK
    qb = q.reshape(B, nb, Q_BLOCK, N_HEADS, 2, HEAD_DIM).swapaxes(0, 1)
    scale = HEAD_DIM ** -0.5
    vf = v.astype(jnp.float32)

    def block(qi):
        s = jnp.einsum('bqhmd,bkhmd->bhmqk', qi, k).astype(jnp.float32) * scale
        p = jax.nn.softmax(s, axis=-1)
        a = p[:, :, 0] - lam * p[:, :, 1]
        return jnp.einsum('bhqk,bkhe->bqhe', a, vf)

    o = lax.map(block, qb)
    return o.swapaxes(0, 1).reshape(B, S, N_HEADS, V_DIM)


def dwconv3(a, w, b):
    ap = jnp.pad(a, ((0, 0), (1, 1), (0, 0)))
    return ap[:, :-2] * w[0] + ap[:, 1:-1] * w[1] + ap[:, 2:] * w[2] + b


def trunk_layer(x, mod, lam_init, w_in, w_pool, pool_scale, w_proj_a, w_proj_b, w_out,
                lam_q1, lam_k1, lam_q2, lam_k2, subln_w, norm1_w, norm2_w,
                w_ffn_in, ffn_conv_w, ffn_conv_b, w_ffn_out, ctx_k=None, ctx_v=None):
    B, S, _ = x.shape
    shift1, scale1, gate1, shift2, scale2, gate2 = jnp.split(mod, 6, axis=-1)

    h = modulate(x, norm1_w, shift1, scale1)
    proj = h @ w_in
    u, q, k, v, gates = jnp.split(
        proj, [D_POOL, D_POOL + D_ATTN, D_POOL + 2 * D_ATTN, D_POOL + 3 * D_ATTN], axis=-1)
    q = q.reshape(B, S, N_HEADS, 2, HEAD_DIM)
    k = k.reshape(B, S, N_HEADS, 2, HEAD_DIM)
    v = v.reshape(B, S, N_HEADS, V_DIM)

    y_a = multi_pool(u, w_pool, pool_scale) @ w_proj_a

    lam = (jnp.exp(jnp.sum(lam_q1.astype(jnp.float32) * lam_k1.astype(jnp.float32)))
           - jnp.exp(jnp.sum(lam_q2.astype(jnp.float32) * lam_k2.astype(jnp.float32)))
           + lam_init)
    if ctx_k is None:
        attn = diff_attention(q, k, v, lam)
    else:
        q = rope_2d(q)
        k_lat = rope_2d(k)
        k_all = jnp.concatenate([ctx_k.astype(k_lat.dtype), k_lat], axis=1)
        v_all = jnp.concatenate([ctx_v.astype(v.dtype), v], axis=1)
        attn = diff_attention(q, k_all, v_all, lam)
    o = rms_norm(attn, subln_w) * (1 - lam_init)
    y_b = o.reshape(B, S, D_ATTN).astype(x.dtype) @ w_proj_b

    g_a, g_b = jnp.split(jax.nn.sigmoid(gates), 2, axis=-1)
    x = x + gate1[:, None, :] * ((g_a * y_a + g_b * y_b) @ w_out)

    h = modulate(x, norm2_w, shift2, scale2)
    a, up = jnp.split(h @ w_ffn_in, 2, axis=-1)
    a = dwconv3(a, ffn_conv_w, ffn_conv_b)
    x = x + gate2[:, None, :] * ((jax.nn.silu(a) * up) @ w_ffn_out)
    return x, k, v


def setup_inputs(seed: int = 0) -> dict:
    key = jax.random.key(seed)
    ks = jax.random.split(key, 26)
    nrm = jax.random.normal
    f32 = jnp.float32
    return {
        "x_prompt": nrm(ks[0], (BATCH, SEQ, D_MODEL), f32),
        "x_sample": nrm(ks[1], (DEC_BATCH, DEC_SEQ, D_MODEL), f32),
        "cache_k": nrm(ks[2], (DEC_BATCH, DEPTH, PAST_LEN, N_HEADS, 2, HEAD_DIM), f32),
        "cache_v": nrm(ks[3], (DEC_BATCH, DEPTH, PAST_LEN, N_HEADS, V_DIM), f32),
        "c": nrm(ks[4], (DEC_BATCH, D_MODEL), f32),
        "c_ctx": nrm(ks[5], (D_MODEL,), f32),
        "w_ada": nrm(ks[6], (DEPTH, D_MODEL, 6 * D_MODEL), f32) * D_MODEL ** -0.5,
        "b_ada": nrm(ks[7], (DEPTH, 6 * D_MODEL), f32) * 0.01,
        "w_in": nrm(ks[8], (DEPTH, D_MODEL, D_IN), f32) * D_MODEL ** -0.5,
        "w_pool": nrm(ks[9], (DEPTH, N_POOL_GROUPS, POOL_GROUP_DIM, POOL_GROUP_DIM), f32) * POOL_GROUP_DIM ** -0.5,
        "pool_scale": 1.0 + 0.1 * nrm(ks[10], (DEPTH, D_POOL), f32),
        "w_proj_a": nrm(ks[11], (DEPTH, D_POOL, D_MODEL), f32) * D_POOL ** -0.5,
        "w_proj_b": nrm(ks[12], (DEPTH, D_ATTN, D_MODEL), f32) * D_ATTN ** -0.5,
        "w_out": nrm(ks[13], (DEPTH, D_MODEL, D_MODEL), f32) * D_MODEL ** -0.5,
        "lam_q1": nrm(ks[14], (DEPTH, HEAD_DIM), f32) * 0.1,
        "lam_k1": nrm(ks[15], (DEPTH, HEAD_DIM), f32) * 0.1,
        "lam_q2": nrm(ks[16], (DEPTH, HEAD_DIM), f32) * 0.1,
        "lam_k2": nrm(ks[17], (DEPTH, HEAD_DIM), f32) * 0.1,
        "subln_w": 1.0 + 0.1 * nrm(ks[18], (DEPTH, V_DIM), f32),
        "norm1_w": 1.0 + 0.1 * nrm(ks[19], (DEPTH, D_MODEL), f32),
        "norm2_w": 1.0 + 0.1 * nrm(ks[20], (DEPTH, D_MODEL), f32),
        "w_ffn_in": nrm(ks[21], (DEPTH, D_MODEL, 2 * D_FF), f32) * D_MODEL ** -0.5,
        "ffn_conv_w": nrm(ks[22], (DEPTH, CONV_W, D_FF), f32) * CONV_W ** -0.5,
        "ffn_conv_b": nrm(ks[23], (DEPTH, D_FF), f32) * 0.01,
        "w_ffn_out": nrm(ks[24], (DEPTH, D_FF, D_MODEL), f32) * D_FF ** -0.5,
        "final_norm_w": 1.0 + 0.1 * nrm(ks[25], (D_MODEL,), f32),
    }


def reference(x_prompt, x_sample, cache_k, cache_v, c, c_ctx, w_ada, b_ada, w_in, w_pool, pool_scale,
              w_proj_a, w_proj_b, w_out, lam_q1, lam_k1, lam_q2, lam_k2, subln_w, norm1_w, norm2_w,
              w_ffn_in, ffn_conv_w, ffn_conv_b, w_ffn_out, final_norm_w):
    xp = x_prompt
    xs = x_sample
    k_list = []
    v_list = []
    for i in range(DEPTH):
        lam_init = 0.8 - 0.6 * math.exp(-0.3 * i)
        mod_ctx = jax.nn.silu(c_ctx)[None, :] @ w_ada[i] + b_ada[i]
        mod_lat = jax.nn.silu(c) @ w_ada[i] + b_ada[i]
        layer_w = dict(w_in=w_in[i], w_pool=w_pool[i], pool_scale=pool_scale[i], w_proj_a=w_proj_a[i],
                       w_proj_b=w_proj_b[i], w_out=w_out[i], lam_q1=lam_q1[i], lam_k1=lam_k1[i],
                       lam_q2=lam_q2[i], lam_k2=lam_k2[i], subln_w=subln_w[i], norm1_w=norm1_w[i],
                       norm2_w=norm2_w[i], w_ffn_in=w_ffn_in[i], ffn_conv_w=ffn_conv_w[i],
                       ffn_conv_b=ffn_conv_b[i], w_ffn_out=w_ffn_out[i])
        xp, k_ctx, v_ctx = trunk_layer(xp, mod_ctx, lam_init, **layer_w)
        k_list.append(k_ctx)
        v_list.append(v_ctx)
        xs, _, _ = trunk_layer(xs, mod_lat, lam_init, ctx_k=cache_k[:, i], ctx_v=cache_v[:, i], **layer_w)
    y_prompt = rms_norm(xp, final_norm_w)
    y_sample = rms_norm(xs, final_norm_w)
    new_cache_k = jnp.stack(k_list, axis=1)
    new_cache_v = jnp.stack(v_list, axis=1)
    return (y_prompt, y_sample, new_cache_k, new_cache_v)
```

```python
import functools
import math

import jax
import jax.numpy as jnp
from jax import lax
from jax.experimental import pallas as pl
from jax.experimental.pallas import tpu as pltpu

F32 = jnp.float32
BF16 = jnp.bfloat16

GRID_W = 64
N_HEADS = 8
HEAD_DIM = 64
V_DIM = 2 * HEAD_DIM
POOL_WINDOWS = (2, 4, 8, 16)
ROPE_BASE = 10000.0
EPS = 1e-6

LANES = 128
BF16_SUBLANES = 16
VMEM_LIMIT_BYTES = 56 * 1024 * 1024

MOD_ROWS = 16
MOD_TN = 512
HALO = BF16_SUBLANES
FFN_CHUNK = 256


def _dot(a, b):
    return jnp.dot(a, b, preferred_element_type=F32)


def _const_spec(shape):
    zeros = (0,) * len(shape)
    return pl.BlockSpec(shape, lambda *_: zeros, pipeline_mode=pl.Buffered(1))


def _params(*sem):
    return pltpu.CompilerParams(dimension_semantics=sem, vmem_limit_bytes=VMEM_LIMIT_BYTES)


def _mod_kernel(cc_ref, w_ref, b_ref, lamv_ref, mod_ref, lam_ref, *, lam_init):
    c = cc_ref[...]
    sc = (c * jax.nn.sigmoid(c)).astype(BF16)
    mod_ref[...] = _dot(sc, w_ref[...].astype(BF16)) + b_ref[...]
    lv = lamv_ref[...]
    p1 = jnp.sum(lv[0:1] * lv[1:2], axis=-1, keepdims=True)
    p2 = jnp.sum(lv[2:3] * lv[3:4], axis=-1, keepdims=True)
    lam = jnp.exp(p1) - jnp.exp(p2) + lam_init
    lam_ref[...] = jnp.broadcast_to(lam, lam_ref.shape)


def _mod_call(cc, w_ada, b_ada, lamv, lam_init):
    d, n = w_ada.shape
    return pl.pallas_call(
        functools.partial(_mod_kernel, lam_init=lam_init),
        grid=(n // MOD_TN,),
        in_specs=[
            pl.BlockSpec((MOD_ROWS, d), lambda j: (0, 0)),
            pl.BlockSpec((d, MOD_TN), lambda j: (0, j)),
            pl.BlockSpec((1, MOD_TN), lambda j: (0, j)),
            pl.BlockSpec(lamv.shape, lambda j: (0, 0)),
        ],
        out_specs=[
            pl.BlockSpec((MOD_ROWS, MOD_TN), lambda j: (0, j)),
            pl.BlockSpec((8, LANES), lambda j: (0, 0)),
        ],
        out_shape=[
            jax.ShapeDtypeStruct((MOD_ROWS, n), F32),
            jax.ShapeDtypeStruct((8, LANES), F32),
        ],
        compiler_params=_params("arbitrary"),
        name="mod",
    )(cc, w_ada, b_ada, lamv)


def _rms_modulate(x, g, shift, scale):
    ms = jnp.mean(x * x, axis=-1, keepdims=True)
    return x * lax.rsqrt(ms + EPS) * g * (1.0 + scale) + shift


def _inproj_kernel(*refs, rope, emit_f32, d_pool, d_attn):
    x_ref, shift_ref, scale_ref, g_ref, w_ref = refs[:5]
    pos = 5
    if rope:
        cos_ref, sa_ref, sb_ref = refs[pos:pos + 3]
        pos += 3
    u_ref, q_ref, k_ref, v_ref, gt_ref = refs[pos:pos + 5]
    pos += 5
    if emit_f32:
        k32_ref, v32_ref = refs[pos:pos + 2]

    hb = _rms_modulate(x_ref[0], g_ref[...], shift_ref[0], scale_ref[0]).astype(BF16)

    def rotate(t):
        if not rope:
            return t
        cos, sa, sb = cos_ref[...], sa_ref[...], sb_ref[...]
        outs = []
        for c in range(t.shape[1] // LANES):
            tc = t[:, c * LANES:(c + 1) * LANES]
            outs.append(tc * cos + pltpu.roll(tc, LANES - HEAD_DIM // 4, 1) * sa
                        + pltpu.roll(tc, HEAD_DIM // 4, 1) * sb)
        return jnp.concatenate(outs, axis=1)

    o0 = 0
    u_ref[0] = _dot(hb, w_ref[:, o0:o0 + d_pool]).astype(BF16)
    o0 += d_pool
    q = _dot(hb, w_ref[:, o0:o0 + d_attn])
    q_ref[0] = (rotate(q) * (HEAD_DIM ** -0.5)).astype(BF16)
    o0 += d_attn
    k = _dot(hb, w_ref[:, o0:o0 + d_attn])
    if emit_f32:
        k32_ref[0] = k
    k_ref[0] = rotate(k).astype(BF16)
    o0 += d_attn
    v = _dot(hb, w_ref[:, o0:o0 + d_attn])
    if emit_f32:
        v32_ref[0] = v
    v_ref[0] = v.astype(BF16)
    o0 += d_attn
    gt_ref[0] = _dot(hb, w_ref[:, o0:]).astype(BF16)


def _row_spec(n_rows):
    if n_rows == 1:
        return lambda b, s: (0, 0, 0)
    return lambda b, s: (b, 0, 0)


def _inproj_call(x, shift, scale, g, w_in, rope_tabs, *, tm, emit_f32, d_pool, d_attn):
    bsz, seq, d = x.shape
    d_in = w_in.shape[1]
    d_gate = d_in - d_pool - 3 * d_attn
    rope = rope_tabs is not None
    tok = lambda b, s: (b, s, 0)
    in_specs = [
        pl.BlockSpec((1, tm, d), tok),
        pl.BlockSpec((1, 1, d), _row_spec(shift.shape[0])),
        pl.BlockSpec((1, 1, d), _row_spec(scale.shape[0])),
        _const_spec((1, d)),
        _const_spec(w_in.shape),
    ]
    args = [x, shift, scale, g, w_in]
    if rope:
        in_specs += [pl.BlockSpec((tm, LANES), lambda b, s: (s, 0))] * 3
        args += list(rope_tabs)
    widths = [d_pool, d_attn, d_attn, d_attn, d_gate]
    out_specs = [pl.BlockSpec((1, tm, w), tok) for w in widths]
    out_shape = [jax.ShapeDtypeStruct((bsz, seq, w), BF16) for w in widths]
    if emit_f32:
        out_specs += [pl.BlockSpec((1, tm, d_attn), tok)] * 2
        out_shape += [jax.ShapeDtypeStruct((bsz, seq, d_attn), F32)] * 2
    return pl.pallas_call(
        functools.partial(_inproj_kernel, rope=rope, emit_f32=emit_f32, d_pool=d_pool, d_attn=d_attn),
        grid=(bsz, seq // tm),
        in_specs=in_specs,
        out_specs=out_specs,
        out_shape=out_shape,
        compiler_params=_params("parallel", "parallel"),
        name="in_proj_rope" if rope else "in_proj",
    )(*args)


def _attn_kernel(*refs, n_src, tq, out_scale):
    lam_ref, sub_ref, q_ref = refs[:3]
    kv_refs = refs[3:3 + 2 * n_src]
    o_ref = refs[3 + 2 * n_src]

    q = q_ref[0].astype(F32)
    lane = lax.broadcasted_iota(jnp.int32, q.shape, 1)
    qq = jnp.concatenate([jnp.where(lane < HEAD_DIM, q, 0.0),
                          jnp.where(lane >= HEAD_DIM, q, 0.0)], axis=0).astype(BF16)

    scores = []
    for i in range(n_src):
        k = kv_refs[2 * i][0].astype(BF16)
        scores.append(lax.dot_general(qq, k, (((1,), (1,)), ((), ())), preferred_element_type=F32))
    m = jnp.max(scores[0], axis=-1, keepdims=True)
    for s in scores[1:]:
        m = jnp.maximum(m, jnp.max(s, axis=-1, keepdims=True))
    es = [jnp.exp(s - m) for s in scores]
    l = jnp.sum(es[0], axis=-1, keepdims=True)
    for e in es[1:]:
        l = l + jnp.sum(e, axis=-1, keepdims=True)
    r = 1.0 / l
    r0 = r[:tq]
    r1 = r[tq:] * lam_ref[0:1, 0:1]

    o = None
    for i in range(n_src):
        a = (es[i][:tq] * r0 - es[i][tq:] * r1).astype(BF16)
        pv = _dot(a, kv_refs[2 * i + 1][0].astype(BF16))
        o = pv if o is None else o + pv

    ms = jnp.mean(o * o, axis=-1, keepdims=True)
    o_ref[0] = (o * lax.rsqrt(ms + EPS) * sub_ref[...] * out_scale).astype(BF16)


def _attn_call(lam, subln, q, kvs, *, tq, out_scale):
    bsz, seq, d_attn = q.shape
    head_w = d_attn // N_HEADS
    in_specs = [
        _const_spec(lam.shape),
        _const_spec(subln.shape),
        pl.BlockSpec((1, tq, head_w), lambda b, h, i: (b, i, h)),
    ]
    args = [lam, subln, q]
    for kv in kvs:
        in_specs.append(pl.BlockSpec((1, kv.shape[1], head_w), lambda b, h, i: (b, 0, h)))
        args.append(kv)
    return pl.pallas_call(
        functools.partial(_attn_kernel, n_src=len(kvs) // 2, tq=tq, out_scale=out_scale),
        grid=(bsz, N_HEADS, seq // tq),
        in_specs=in_specs,
        out_specs=pl.BlockSpec((1, tq, head_w), lambda b, h, i: (b, i, h)),
        out_shape=jax.ShapeDtypeStruct((bsz, seq, d_attn), BF16),
        compiler_params=_params("parallel", "parallel", "parallel"),
        name="attn%d" % (len(kvs) // 2),
    )(*args)


def _fill_with_halo(dst_ref, prev_ref, mid, next_ref, tm):
    s = pl.program_id(1)
    last = pl.num_programs(1) - 1
    prev = prev_ref[0].astype(dst_ref.dtype)
    nxt = next_ref[0].astype(dst_ref.dtype)
    dst_ref[0:HALO] = jnp.where(s > 0, prev, jnp.zeros_like(prev))
    dst_ref[HALO:HALO + tm] = mid
    dst_ref[HALO + tm:] = jnp.where(s < last, nxt, jnp.zeros_like(nxt))


def _mix_kernel(x_ref, u_ref, uprev_ref, unext_ref, o_ref, gt_ref, gate1_ref, shift2_ref, scale2_ref,
                n2_ref, wpool_ref, pscale_ref, wpa_ref, wpb_ref, wout_ref,
                x1_ref, h2_ref, pad_ref, *, tm, seq):
    d = x_ref.shape[-1]
    uf = u_ref[0].astype(F32)
    _fill_with_halo(pad_ref, uprev_ref, uf, unext_ref, tm)
    t = pl.program_id(1) * tm + lax.broadcasted_iota(jnp.int32, (tm, 1), 0)

    mixed = []
    for gi, w in enumerate(POOL_WINDOWS):
        cols = slice(gi * LANES, (gi + 1) * LANES)
        acc = None
        for off in range(-(w // 2), w - w // 2):
            piece = pad_ref[HALO + off:HALO + off + tm, cols]
            acc = piece if acc is None else acc + piece
        lo = jnp.maximum(t - w // 2, 0)
        hi = jnp.minimum(t + (w - w // 2), seq)
        cnt = (hi - lo).astype(F32)
        pooled = acc / cnt - uf[:, cols]
        mixed.append(_dot(pooled.astype(BF16), wpool_ref[gi]) * pscale_ref[:, cols])
    mixed = jnp.concatenate(mixed, axis=1).astype(BF16)

    y_a = _dot(mixed, wpa_ref[...])
    y_b = _dot(o_ref[0], wpb_ref[...])
    g = jax.nn.sigmoid(gt_ref[0].astype(F32))
    merged = (g[:, :d] * y_a + g[:, d:] * y_b).astype(BF16)
    x1 = x_ref[0] + gate1_ref[0] * _dot(merged, wout_ref[...])
    x1_ref[0] = x1
    h2_ref[0] = _rms_modulate(x1, n2_ref[...], shift2_ref[0], scale2_ref[0]).astype(BF16)


def _halo_specs(width, tm, seq):
    blocks_per_tile = tm // HALO
    last_block = seq // HALO - 1
    prev = pl.BlockSpec((1, HALO, width), lambda b, s: (b, jnp.maximum(s * blocks_per_tile - 1, 0), 0))
    nxt = pl.BlockSpec((1, HALO, width), lambda b, s: (b, jnp.minimum((s + 1) * blocks_per_tile, last_block), 0))
    return prev, nxt


def _mix_call(x, u, o, gates, gate1, shift2, scale2, n2, w_pool, pscale, wpa, wpb, wout, *, tm):
    bsz, seq, d = x.shape
    d_pool = u.shape[-1]
    tok = lambda b, s: (b, s, 0)
    uprev, unext = _halo_specs(d_pool, tm, seq)
    in_specs = [
        pl.BlockSpec((1, tm, d), tok),
        pl.BlockSpec((1, tm, d_pool), tok), uprev, unext,
        pl.BlockSpec((1, tm, o.shape[-1]), tok),
        pl.BlockSpec((1, tm, gates.shape[-1]), tok),
        pl.BlockSpec((1, 1, d), _row_spec(gate1.shape[0])),
        pl.BlockSpec((1, 1, d), _row_spec(shift2.shape[0])),
        pl.BlockSpec((1, 1, d), _row_spec(scale2.shape[0])),
        _const_spec(n2.shape), _const_spec(w_pool.shape), _const_spec(pscale.shape),
        _const_spec(wpa.shape), _const_spec(wpb.shape), _const_spec(wout.shape),
    ]
    return pl.pallas_call(
        functools.partial(_mix_kernel, tm=tm, seq=seq),
        grid=(bsz, seq // tm),
        in_specs=in_specs,
        out_specs=[pl.BlockSpec((1, tm, d), tok), pl.BlockSpec((1, tm, d), tok)],
        out_shape=[jax.ShapeDtypeStruct((bsz, seq, d), F32), jax.ShapeDtypeStruct((bsz, seq, d), BF16)],
        scratch_shapes=[pltpu.VMEM((tm + 2 * HALO, d_pool), F32)],
        compiler_params=_params("parallel", "parallel"),
        name="mix",
    )(x, u, u, u, o, gates, gate1, shift2, scale2, n2, w_pool, pscale, wpa, wpb, wout)


def _ffn_kernel(x1_ref, h2_ref, hprev_ref, hnext_ref, win_ref, cw_ref, cb_ref, wout_ref, gate2_ref, fnw_ref,
                y_ref, hext_ref, aext_ref, *, tm, d_ff):
    _fill_with_halo(hext_ref, hprev_ref, h2_ref[0], hnext_ref, tm)
    he = hext_ref[...]
    h2 = h2_ref[0]
    acc = None
    for c in range(d_ff // FFN_CHUNK):
        cols = slice(c * FFN_CHUNK, (c + 1) * FFN_CHUNK)
        aext_ref[...] = _dot(he, win_ref[:, cols])
        up = _dot(h2, win_ref[:, d_ff + c * FFN_CHUNK:d_ff + (c + 1) * FFN_CHUNK])
        conv = (aext_ref[HALO - 1:HALO - 1 + tm] * cw_ref[0:1, cols]
                + aext_ref[HALO:HALO + tm] * cw_ref[1:2, cols]
                + aext_ref[HALO + 1:HALO + 1 + tm] * cw_ref[2:3, cols]
                + cb_ref[:, cols])
        act = (conv * jax.nn.sigmoid(conv) * up).astype(BF16)
        part = _dot(act, wout_ref[cols, :])
        acc = part if acc is None else acc + part
    x2 = x1_ref[0] + gate2_ref[0] * acc
    ms = jnp.mean(x2 * x2, axis=-1, keepdims=True)
    y_ref[0] = x2 * lax.rsqrt(ms + EPS) * fnw_ref[...]


def _ffn_call(x1, h2, w_ffn_in, conv_w, conv_b, w_ffn_out, gate2, fnw, *, tm):
    bsz, seq, d = x1.shape
    d_ff = w_ffn_out.shape[0]
    tok = lambda b, s: (b, s, 0)
    hprev, hnext = _halo_specs(d, tm, seq)
    in_specs = [
        pl.BlockSpec((1, tm, d), tok),
        pl.BlockSpec((1, tm, d), tok), hprev, hnext,
        _const_spec(w_ffn_in.shape), _const_spec(conv_w.shape), _const_spec(conv_b.shape),
        _const_spec(w_ffn_out.shape),
        pl.BlockSpec((1, 1, d), _row_spec(gate2.shape[0])),
        _const_spec(fnw.shape),
    ]
    return pl.pallas_call(
        functools.partial(_ffn_kernel, tm=tm, d_ff=d_ff),
        grid=(bsz, seq // tm),
        in_specs=in_specs,
        out_specs=pl.BlockSpec((1, tm, d), tok),
        out_shape=jax.ShapeDtypeStruct((bsz, seq, d), F32),
        scratch_shapes=[pltpu.VMEM((tm + 2 * HALO, d), BF16), pltpu.VMEM((tm + 2 * HALO, FFN_CHUNK), F32)],
        compiler_params=_params("parallel", "parallel"),
        name="ffn",
    )(x1, h2, h2, h2, w_ffn_in, conv_w, conv_b, w_ffn_out, gate2, fnw)


def _rope_tables(n_tok):
    rows = n_tok // GRID_W
    row = jnp.repeat(jnp.arange(rows, dtype=F32), GRID_W)
    col = jnp.tile(jnp.arange(GRID_W, dtype=F32), rows)
    n_freq = HEAD_DIM // 4
    inv = ROPE_BASE ** (-jnp.arange(n_freq, dtype=F32) / n_freq)
    ang_row = row[:, None] * inv
    ang_col = col[:, None] * inv
    zeros = jnp.zeros_like(ang_row)
    cos_row, sin_row = jnp.cos(ang_row), jnp.sin(ang_row)
    cos_col, sin_col = jnp.cos(ang_col), jnp.sin(ang_col)
    cos = jnp.concatenate([cos_row, cos_row, cos_col, cos_col], axis=1)
    sa = jnp.concatenate([-sin_row, zeros, -sin_col, zeros], axis=1)
    sb = jnp.concatenate([zeros, sin_row, zeros, sin_col], axis=1)
    rep = LANES // HEAD_DIM
    return tuple(jnp.tile(t, (1, rep)) for t in (cos, sa, sb))


def _stream(x, mods, w, lam, ctx_kv, *, tm, tq, lam_init, final_norm_w):
    shift1, scale1, gate1, shift2, scale2, gate2 = mods
    d_pool = w["w_proj_a"].shape[0]
    d_attn = w["w_proj_b"].shape[0]
    is_ctx = ctx_kv is None
    rope_tabs = None if is_ctx else _rope_tables(x.shape[1])
    outs = _inproj_call(x, shift1, scale1, w["norm1_w"], w["w_in"], rope_tabs,
                        tm=tm, emit_f32=is_ctx, d_pool=d_pool, d_attn=d_attn)
    u, q, k, v, gates = outs[:5]
    kvs = [k, v] if is_ctx else [ctx_kv[0], ctx_kv[1], k, v]
    o = _attn_call(lam, w["subln_w"], q, kvs, tq=tq, out_scale=1.0 - lam_init)
    x1, h2 = _mix_call(x, u, o, gates, gate1, shift2, scale2, w["norm2_w"], w["w_pool"], w["pool_scale"],
                       w["w_proj_a"], w["w_proj_b"], w["w_out"], tm=tm)
    y = _ffn_call(x1, h2, w["w_ffn_in"], w["ffn_conv_w"], w["ffn_conv_b"], w["w_ffn_out"], gate2,
                  final_norm_w, tm=tm)
    return y, outs[5:]


def kernel(x_prompt, x_sample, cache_k, cache_v, c, c_ctx, w_ada, b_ada, w_in, w_pool, pool_scale, w_proj_a, w_proj_b, w_out, lam_q1, lam_k1, lam_q2, lam_k2, subln_w, norm1_w, norm2_w, w_ffn_in, ffn_conv_w, ffn_conv_b, w_ffn_out, final_norm_w):
    assert w_ada.shape[0] == 1, "single trunk layer"
    bsz, seq, d = x_prompt.shape
    dec_b, dec_seq, _ = x_sample.shape
    assert 1 + dec_b <= MOD_ROWS
    lam_init = 0.8 - 0.6 * math.exp(-0.3 * 0)

    cc = jnp.zeros((MOD_ROWS, d), F32).at[0].set(c_ctx).at[1:1 + dec_b].set(c)
    lamv = jnp.concatenate([lam_q1, lam_k1, lam_q2, lam_k2], axis=0)
    mod, lam = _mod_call(cc, w_ada[0], b_ada, lamv, lam_init)
    mod = mod.reshape(MOD_ROWS, 6, 1, d)
    mods_ctx = [mod[0:1, j] for j in range(6)]
    mods_lat = [mod[1:1 + dec_b, j] for j in range(6)]

    w = dict(
        w_in=w_in[0].astype(BF16), w_pool=w_pool[0].astype(BF16), pool_scale=pool_scale,
        w_proj_a=w_proj_a[0].astype(BF16), w_proj_b=w_proj_b[0].astype(BF16), w_out=w_out[0].astype(BF16),
        subln_w=subln_w, norm1_w=norm1_w, norm2_w=norm2_w,
        w_ffn_in=w_ffn_in[0].astype(BF16), ffn_conv_w=ffn_conv_w[0], ffn_conv_b=ffn_conv_b,
        w_ffn_out=w_ffn_out[0].astype(BF16),
    )
    fnw = final_norm_w.reshape(1, d)

    y_prompt, (k32, v32) = _stream(x_prompt, mods_ctx, w, lam, None, tm=256, tq=256,
                                   lam_init=lam_init, final_norm_w=fnw)
    d_attn = k32.shape[-1]
    ctx_kv = (cache_k[:, 0].reshape(dec_b, -1, d_attn), cache_v[:, 0].reshape(dec_b, -1, d_attn))
    y_sample, _ = _stream(x_sample, mods_lat, w, lam, ctx_kv, tm=256, tq=256,
                          lam_init=lam_init, final_norm_w=fnw)

    new_cache_k = k32.reshape(bsz, 1, seq, N_HEADS, 2, HEAD_DIM)
    new_cache_v = v32.reshape(bsz, 1, seq, N_HEADS, V_DIM)
    return (y_prompt, y_sample, new_cache_k, new_cache_v)
```

```python
import functools
import math

import jax
import jax.numpy as jnp
from jax import lax
from jax.experimental import pallas as pl
from jax.experimental.pallas import tpu as pltpu

F32 = jnp.float32
BF16 = jnp.bfloat16

GRID_W = 64
N_HEADS = 8
HEAD_DIM = 64
V_DIM = 2 * HEAD_DIM
POOL_WINDOWS = (2, 4, 8, 16)
ROPE_BASE = 10000.0
EPS = 1e-6

LANES = 128
BF16_SUBLANES = 16
VMEM_LIMIT_BYTES = 56 * 1024 * 1024

MOD_ROWS = 16
MOD_TN = 512
HALO = BF16_SUBLANES
FFN_CHUNK = 256


def _dot(a, b):
    return jnp.dot(a, b, preferred_element_type=F32)


def _const_spec(shape):
    zeros = (0,) * len(shape)
    return pl.BlockSpec(shape, lambda *_: zeros, pipeline_mode=pl.Buffered(1))


def _params(*sem):
    return pltpu.CompilerParams(dimension_semantics=sem, vmem_limit_bytes=VMEM_LIMIT_BYTES)


def _mod_kernel(cc_ref, w_ref, b_ref, lamv_ref, mod_ref, lam_ref, *, lam_init):
    c = cc_ref[...]
    sc = (c * jax.nn.sigmoid(c)).astype(BF16)
    mod_ref[...] = _dot(sc, w_ref[...].astype(BF16)) + b_ref[...]
    lv = lamv_ref[...]
    p1 = jnp.sum(lv[0:1] * lv[1:2], axis=-1, keepdims=True)
    p2 = jnp.sum(lv[2:3] * lv[3:4], axis=-1, keepdims=True)
    lam = jnp.exp(p1) - jnp.exp(p2) + lam_init
    lam_ref[...] = jnp.broadcast_to(lam, lam_ref.shape)


def _mod_call(cc, w_ada, b_ada, lamv, lam_init):
    d, n = w_ada.shape
    return pl.pallas_call(
        functools.partial(_mod_kernel, lam_init=lam_init),
        grid=(n // MOD_TN,),
        in_specs=[
            pl.BlockSpec((MOD_ROWS, d), lambda j: (0, 0)),
            pl.BlockSpec((d, MOD_TN), lambda j: (0, j)),
            pl.BlockSpec((1, MOD_TN), lambda j: (0, j)),
            pl.BlockSpec(lamv.shape, lambda j: (0, 0)),
        ],
        out_specs=[
            pl.BlockSpec((MOD_ROWS, MOD_TN), lambda j: (0, j)),
            pl.BlockSpec((8, LANES), lambda j: (0, 0)),
        ],
        out_shape=[
            jax.ShapeDtypeStruct((MOD_ROWS, n), F32),
            jax.ShapeDtypeStruct((8, LANES), F32),
        ],
        compiler_params=_params("arbitrary"),
        name="mod",
    )(cc, w_ada, b_ada, lamv)


def _rms_modulate(x, g, shift, scale):
    ms = jnp.mean(x * x, axis=-1, keepdims=True)
    return x * lax.rsqrt(ms + EPS) * g * (1.0 + scale) + shift


def _inproj_kernel(*refs, rope, emit_f32, d_pool, d_attn):
    x_ref, shift_ref, scale_ref, g_ref, w_ref = refs[:5]
    pos = 5
    if rope:
        cos_ref, sa_ref, sb_ref = refs[pos:pos + 3]
        pos += 3
    u_ref, q_ref, k_ref, v_ref, gt_ref = refs[pos:pos + 5]
    pos += 5
    if emit_f32:
        k32_ref, v32_ref = refs[pos:pos + 2]

    hb = _rms_modulate(x_ref[0], g_ref[...], shift_ref[0], scale_ref[0]).astype(BF16)

    def rotate(t):
        if not rope:
            return t
        cos, sa, sb = cos_ref[...], sa_ref[...], sb_ref[...]
        outs = []
        for c in range(t.shape[1] // LANES):
            tc = t[:, c * LANES:(c + 1) * LANES]
            outs.append(tc * cos + pltpu.roll(tc, LANES - HEAD_DIM // 4, 1) * sa
                        + pltpu.roll(tc, HEAD_DIM // 4, 1) * sb)
        return jnp.concatenate(outs, axis=1)

    o0 = 0
    u_ref[0] = _dot(hb, w_ref[:, o0:o0 + d_pool]).astype(BF16)
    o0 += d_pool
    q = _dot(hb, w_ref[:, o0:o0 + d_attn])
    q_ref[0] = (rotate(q) * (HEAD_DIM ** -0.5)).astype(BF16)
    o0 += d_attn
    k = _dot(hb, w_ref[:, o0:o0 + d_attn])
    if emit_f32:
        k32_ref[0] = k
    k_ref[0] = rotate(k).astype(BF16)
    o0 += d_attn
    v = _dot(hb, w_ref[:, o0:o0 + d_attn])
    if emit_f32:
        v32_ref[0] = v
    v_ref[0] = v.astype(BF16)
    o0 += d_attn
    gt_ref[0] = _dot(hb, w_ref[:, o0:]).astype(BF16)


def _row_spec(n_rows):
    if n_rows == 1:
        return lambda b, s: (0, 0, 0)
    return lambda b, s: (b, 0, 0)


def _inproj_call(x, shift, scale, g, w_in, rope_tabs, *, tm, emit_f32, d_pool, d_attn):
    bsz, seq, d = x.shape
    d_in = w_in.shape[1]
    d_gate = d_in - d_pool - 3 * d_attn
    rope = rope_tabs is not None
    tok = lambda b, s: (b, s, 0)
    in_specs = [
        pl.BlockSpec((1, tm, d), tok),
        pl.BlockSpec((1, 1, d), _row_spec(shift.shape[0])),
        pl.BlockSpec((1, 1, d), _row_spec(scale.shape[0])),
        _const_spec((1, d)),
        _const_spec(w_in.shape),
    ]
    args = [x, shift, scale, g, w_in]
    if rope:
        in_specs += [pl.BlockSpec((tm, LANES), lambda b, s: (s, 0))] * 3
        args += list(rope_tabs)
    widths = [d_pool, d_attn, d_attn, d_attn, d_gate]
    out_specs = [pl.BlockSpec((1, tm, w), tok) for w in widths]
    out_shape = [jax.ShapeDtypeStruct((bsz, seq, w), BF16) for w in widths]
    if emit_f32:
        out_specs += [pl.BlockSpec((1, tm, d_attn), tok)] * 2
        out_shape += [jax.ShapeDtypeStruct((bsz, seq, d_attn), F32)] * 2
    return pl.pallas_call(
        functools.partial(_inproj_kernel, rope=rope, emit_f32=emit_f32, d_pool=d_pool, d_attn=d_attn),
        grid=(bsz, seq // tm),
        in_specs=in_specs,
        out_specs=out_specs,
        out_shape=out_shape,
        compiler_params=_params("parallel", "parallel"),
        name="in_proj_rope" if rope else "in_proj",
    )(*args)


def _attn_kernel(*refs, n_src, tq, group, out_scale):
    lam_ref, sub_ref, q_ref = refs[:3]
    kv_refs = refs[3:3 + 2 * n_src]
    o_ref = refs[3 + 2 * n_src]
    lam = lam_ref[0:1, 0:1]
    lane = lax.broadcasted_iota(jnp.int32, (tq, V_DIM), 1)

    def head_cols(h):
        return slice(h * V_DIM, (h + 1) * V_DIM)

    def score_phase(h):
        q = q_ref[0, :, head_cols(h)].astype(F32)
        qq = jnp.concatenate([jnp.where(lane < HEAD_DIM, q, 0.0),
                              jnp.where(lane >= HEAD_DIM, q, 0.0)], axis=0).astype(BF16)
        scores = []
        for i in range(n_src):
            k = kv_refs[2 * i][0, :, head_cols(h)].astype(BF16)
            scores.append(lax.dot_general(qq, k, (((1,), (1,)), ((), ())), preferred_element_type=F32))
        m = jnp.max(scores[0], axis=-1, keepdims=True)
        for s in scores[1:]:
            m = jnp.maximum(m, jnp.max(s, axis=-1, keepdims=True))
        return scores, m

    def value_phase(h, scores, m):
        pv = None
        for i in range(n_src):
            v = kv_refs[2 * i + 1][0, :, head_cols(h)].astype(BF16)
            ones_lane = lax.broadcasted_iota(jnp.int32, v.shape, 1) == 0
            v_ext = jnp.concatenate([v, jnp.where(ones_lane, 1.0, 0.0).astype(BF16)], axis=1)
            part = _dot(jnp.exp(scores[i] - m).astype(BF16), v_ext)
            pv = part if pv is None else pv + part
        return pv

    def output_phase(h, pv):
        r = 1.0 / pv[:, V_DIM:V_DIM + 1]
        o = pv[:tq, :V_DIM] * r[:tq] - pv[tq:, :V_DIM] * (r[tq:] * lam)
        ms = jnp.mean(o * o, axis=-1, keepdims=True)
        o_ref[0, :, head_cols(h)] = (o * lax.rsqrt(ms + EPS) * sub_ref[...] * out_scale).astype(BF16)

    for h0 in range(0, N_HEADS, group):
        heads = range(h0, h0 + group)
        scored = [score_phase(h) for h in heads]
        pvs = [value_phase(h, *sm) for h, sm in zip(heads, scored)]
        for h, pv in zip(heads, pvs):
            output_phase(h, pv)


def _attn_call(lam, subln, q, kvs, *, tq, group, out_scale):
    bsz, seq, d_attn = q.shape
    assert d_attn == N_HEADS * V_DIM
    in_specs = [
        _const_spec(lam.shape),
        _const_spec(subln.shape),
        pl.BlockSpec((1, tq, d_attn), lambda b, i: (b, i, 0)),
    ]
    args = [lam, subln, q]
    for kv in kvs:
        in_specs.append(pl.BlockSpec((1, kv.shape[1], d_attn), lambda b, i: (b, 0, 0)))
        args.append(kv)
    return pl.pallas_call(
        functools.partial(_attn_kernel, n_src=len(kvs) // 2, tq=tq, group=group, out_scale=out_scale),
        grid=(bsz, seq // tq),
        in_specs=in_specs,
        out_specs=pl.BlockSpec((1, tq, d_attn), lambda b, i: (b, i, 0)),
        out_shape=jax.ShapeDtypeStruct((bsz, seq, d_attn), BF16),
        compiler_params=_params("parallel", "parallel"),
        name="attn%d" % (len(kvs) // 2),
    )(*args)


def _fill_with_halo(dst_ref, prev_ref, mid, next_ref, tm):
    s = pl.program_id(1)
    last = pl.num_programs(1) - 1
    prev = prev_ref[0].astype(dst_ref.dtype)
    nxt = next_ref[0].astype(dst_ref.dtype)
    dst_ref[0:HALO] = jnp.where(s > 0, prev, jnp.zeros_like(prev))
    dst_ref[HALO:HALO + tm] = mid
    dst_ref[HALO + tm:] = jnp.where(s < last, nxt, jnp.zeros_like(nxt))


def _mix_kernel(x_ref, u_ref, uprev_ref, unext_ref, o_ref, gt_ref, gate1_ref, shift2_ref, scale2_ref,
                n2_ref, wpool_ref, pscale_ref, wpa_ref, wpb_ref, wout_ref,
                x1_ref, h2_ref, pad_ref, *, tm, seq):
    d = x_ref.shape[-1]
    uf = u_ref[0].astype(F32)
    _fill_with_halo(pad_ref, uprev_ref, uf, unext_ref, tm)
    t = pl.program_id(1) * tm + lax.broadcasted_iota(jnp.int32, (tm, 1), 0)

    mixed = []
    for gi, w in enumerate(POOL_WINDOWS):
        cols = slice(gi * LANES, (gi + 1) * LANES)
        acc = None
        for off in range(-(w // 2), w - w // 2):
            piece = pad_ref[HALO + off:HALO + off + tm, cols]
            acc = piece if acc is None else acc + piece
        lo = jnp.maximum(t - w // 2, 0)
        hi = jnp.minimum(t + (w - w // 2), seq)
        cnt = (hi - lo).astype(F32)
        pooled = acc / cnt - uf[:, cols]
        mixed.append(_dot(pooled.astype(BF16), wpool_ref[gi]) * pscale_ref[:, cols])
    mixed = jnp.concatenate(mixed, axis=1).astype(BF16)

    y_a = _dot(mixed, wpa_ref[...])
    y_b = _dot(o_ref[0], wpb_ref[...])
    g = jax.nn.sigmoid(gt_ref[0].astype(F32))
    merged = (g[:, :d] * y_a + g[:, d:] * y_b).astype(BF16)
    x1 = x_ref[0] + gate1_ref[0] * _dot(merged, wout_ref[...])
    x1_ref[0] = x1
    h2_ref[0] = _rms_modulate(x1, n2_ref[...], shift2_ref[0], scale2_ref[0]).astype(BF16)


def _halo_specs(width, tm, seq):
    blocks_per_tile = tm // HALO
    last_block = seq // HALO - 1
    prev = pl.BlockSpec((1, HALO, width), lambda b, s: (b, jnp.maximum(s * blocks_per_tile - 1, 0), 0))
    nxt = pl.BlockSpec((1, HALO, width), lambda b, s: (b, jnp.minimum((s + 1) * blocks_per_tile, last_block), 0))
    return prev, nxt


def _mix_call(x, u, o, gates, gate1, shift2, scale2, n2, w_pool, pscale, wpa, wpb, wout, *, tm):
    bsz, seq, d = x.shape
    d_pool = u.shape[-1]
    tok = lambda b, s: (b, s, 0)
    uprev, unext = _halo_specs(d_pool, tm, seq)
    in_specs = [
        pl.BlockSpec((1, tm, d), tok),
        pl.BlockSpec((1, tm, d_pool), tok), uprev, unext,
        pl.BlockSpec((1, tm, o.shape[-1]), tok),
        pl.BlockSpec((1, tm, gates.shape[-1]), tok),
        pl.BlockSpec((1, 1, d), _row_spec(gate1.shape[0])),
        pl.BlockSpec((1, 1, d), _row_spec(shift2.shape[0])),
        pl.BlockSpec((1, 1, d), _row_spec(scale2.shape[0])),
        _const_spec(n2.shape), _const_spec(w_pool.shape), _const_spec(pscale.shape),
        _const_spec(wpa.shape), _const_spec(wpb.shape), _const_spec(wout.shape),
    ]
    return pl.pallas_call(
        functools.partial(_mix_kernel, tm=tm, seq=seq),
        grid=(bsz, seq // tm),
        in_specs=in_specs,
        out_specs=[pl.BlockSpec((1, tm, d), tok), pl.BlockSpec((1, tm, d), tok)],
        out_shape=[jax.ShapeDtypeStruct((bsz, seq, d), F32), jax.ShapeDtypeStruct((bsz, seq, d), BF16)],
        scratch_shapes=[pltpu.VMEM((tm + 2 * HALO, d_pool), F32)],
        compiler_params=_params("parallel", "parallel"),
        name="mix",
    )(x, u, u, u, o, gates, gate1, shift2, scale2, n2, w_pool, pscale, wpa, wpb, wout)


def _ffn_kernel(x1_ref, h2_ref, hprev_ref, hnext_ref, win_ref, cw_ref, cb_ref, wout_ref, gate2_ref, fnw_ref,
                y_ref, hext_ref, aext_ref, *, tm, d_ff):
    _fill_with_halo(hext_ref, hprev_ref, h2_ref[0], hnext_ref, tm)
    he = hext_ref[...]
    h2 = h2_ref[0]
    acc = None
    for c in range(d_ff // FFN_CHUNK):
        cols = slice(c * FFN_CHUNK, (c + 1) * FFN_CHUNK)
        aext_ref[...] = _dot(he, win_ref[:, cols])
        up = _dot(h2, win_ref[:, d_ff + c * FFN_CHUNK:d_ff + (c + 1) * FFN_CHUNK])
        conv = (aext_ref[HALO - 1:HALO - 1 + tm] * cw_ref[0:1, cols]
                + aext_ref[HALO:HALO + tm] * cw_ref[1:2, cols]
                + aext_ref[HALO + 1:HALO + 1 + tm] * cw_ref[2:3, cols]
                + cb_ref[:, cols])
        act = (conv * jax.nn.sigmoid(conv) * up).astype(BF16)
        part = _dot(act, wout_ref[cols, :])
        acc = part if acc is None else acc + part
    x2 = x1_ref[0] + gate2_ref[0] * acc
    ms = jnp.mean(x2 * x2, axis=-1, keepdims=True)
    y_ref[0] = x2 * lax.rsqrt(ms + EPS) * fnw_ref[...]


def _ffn_call(x1, h2, w_ffn_in, conv_w, conv_b, w_ffn_out, gate2, fnw, *, tm):
    bsz, seq, d = x1.shape
    d_ff = w_ffn_out.shape[0]
    tok = lambda b, s: (b, s, 0)
    hprev, hnext = _halo_specs(d, tm, seq)
    in_specs = [
        pl.BlockSpec((1, tm, d), tok),
        pl.BlockSpec((1, tm, d), tok), hprev, hnext,
        _const_spec(w_ffn_in.shape), _const_spec(conv_w.shape), _const_spec(conv_b.shape),
        _const_spec(w_ffn_out.shape),
        pl.BlockSpec((1, 1, d), _row_spec(gate2.shape[0])),
        _const_spec(fnw.shape),
    ]
    return pl.pallas_call(
        functools.partial(_ffn_kernel, tm=tm, d_ff=d_ff),
        grid=(bsz, seq // tm),
        in_specs=in_specs,
        out_specs=pl.BlockSpec((1, tm, d), tok),
        out_shape=jax.ShapeDtypeStruct((bsz, seq, d), F32),
        scratch_shapes=[pltpu.VMEM((tm + 2 * HALO, d), BF16), pltpu.VMEM((tm + 2 * HALO, FFN_CHUNK), F32)],
        compiler_params=_params("parallel", "parallel"),
        name="ffn",
    )(x1, h2, h2, h2, w_ffn_in, conv_w, conv_b, w_ffn_out, gate2, fnw)


def _rope_tables(n_tok):
    rows = n_tok // GRID_W
    row = jnp.repeat(jnp.arange(rows, dtype=F32), GRID_W)
    col = jnp.tile(jnp.arange(GRID_W, dtype=F32), rows)
    n_freq = HEAD_DIM // 4
    inv = ROPE_BASE ** (-jnp.arange(n_freq, dtype=F32) / n_freq)
    ang_row = row[:, None] * inv
    ang_col = col[:, None] * inv
    zeros = jnp.zeros_like(ang_row)
    cos_row, sin_row = jnp.cos(ang_row), jnp.sin(ang_row)
    cos_col, sin_col = jnp.cos(ang_col), jnp.sin(ang_col)
    cos = jnp.concatenate([cos_row, cos_row, cos_col, cos_col], axis=1)
    sa = jnp.concatenate([-sin_row, zeros, -sin_col, zeros], axis=1)
    sb = jnp.concatenate([zeros, sin_row, zeros, sin_col], axis=1)
    rep = LANES // HEAD_DIM
    return tuple(jnp.tile(t, (1, rep)) for t in (cos, sa, sb))


def _stream(x, mods, w, lam, ctx_kv, *, tm, tq, lam_init, final_norm_w):
    shift1, scale1, gate1, shift2, scale2, gate2 = mods
    d_pool = w["w_proj_a"].shape[0]
    d_attn = w["w_proj_b"].shape[0]
    is_ctx = ctx_kv is None
    rope_tabs = None if is_ctx else _rope_tables(x.shape[1])
    outs = _inproj_call(x, shift1, scale1, w["norm1_w"], w["w_in"], rope_tabs,
                        tm=tm, emit_f32=is_ctx, d_pool=d_pool, d_attn=d_attn)
    u, q, k, v, gates = outs[:5]
    kvs = [k, v] if is_ctx else [ctx_kv[0], ctx_kv[1], k, v]
    o = _attn_call(lam, w["subln_w"], q, kvs, tq=tq, group=N_HEADS if is_ctx else 1,
                   out_scale=1.0 - lam_init)
    x1, h2 = _mix_call(x, u, o, gates, gate1, shift2, scale2, w["norm2_w"], w["w_pool"], w["pool_scale"],
                       w["w_proj_a"], w["w_proj_b"], w["w_out"], tm=tm)
    y = _ffn_call(x1, h2, w["w_ffn_in"], w["ffn_conv_w"], w["ffn_conv_b"], w["w_ffn_out"], gate2,
                  final_norm_w, tm=tm)
    return y, outs[5:]


def kernel(x_prompt, x_sample, cache_k, cache_v, c, c_ctx, w_ada, b_ada, w_in, w_pool, pool_scale, w_proj_a, w_proj_b, w_out, lam_q1, lam_k1, lam_q2, lam_k2, subln_w, norm1_w, norm2_w, w_ffn_in, ffn_conv_w, ffn_conv_b, w_ffn_out, final_norm_w):
    assert w_ada.shape[0] == 1, "single trunk layer"
    bsz, seq, d = x_prompt.shape
    dec_b, dec_seq, _ = x_sample.shape
    assert 1 + dec_b <= MOD_ROWS
    lam_init = 0.8 - 0.6 * math.exp(-0.3 * 0)

    cc = jnp.zeros((MOD_ROWS, d), F32).at[0].set(c_ctx).at[1:1 + dec_b].set(c)
    lamv = jnp.concatenate([lam_q1, lam_k1, lam_q2, lam_k2], axis=0)
    mod, lam = _mod_call(cc, w_ada[0], b_ada, lamv, lam_init)
    mod = mod.reshape(MOD_ROWS, 6, 1, d)
    mods_ctx = [mod[0:1, j] for j in range(6)]
    mods_lat = [mod[1:1 + dec_b, j] for j in range(6)]

    w = dict(
        w_in=w_in[0].astype(BF16), w_pool=w_pool[0].astype(BF16), pool_scale=pool_scale,
        w_proj_a=w_proj_a[0].astype(BF16), w_proj_b=w_proj_b[0].astype(BF16), w_out=w_out[0].astype(BF16),
        subln_w=subln_w, norm1_w=norm1_w, norm2_w=norm2_w,
        w_ffn_in=w_ffn_in[0].astype(BF16), ffn_conv_w=ffn_conv_w[0], ffn_conv_b=ffn_conv_b,
        w_ffn_out=w_ffn_out[0].astype(BF16),
    )
    fnw = final_norm_w.reshape(1, d)

    y_prompt, (k32, v32) = _stream(x_prompt, mods_ctx, w, lam, None, tm=256, tq=256,
                                   lam_init=lam_init, final_norm_w=fnw)
    d_attn = k32.shape[-1]
    ctx_kv = (cache_k[:, 0].reshape(dec_b, -1, d_attn), cache_v[:, 0].reshape(dec_b, -1, d_attn))
    y_sample, _ = _stream(x_sample, mods_lat, w, lam, ctx_kv, tm=256, tq=256,
                          lam_init=lam_init, final_norm_w=fnw)

    new_cache_k = k32.reshape(bsz, 1, seq, N_HEADS, 2, HEAD_DIM)
    new_cache_v = v32.reshape(bsz, 1, seq, N_HEADS, V_DIM)
    return (y_prompt, y_sample, new_cache_k, new_cache_v)
```

```python
import functools
import math

import jax
import jax.numpy as jnp
from jax import lax
from jax.experimental import pallas as pl
from jax.experimental.pallas import tpu as pltpu

F32 = jnp.float32
BF16 = jnp.bfloat16

GRID_W = 64
N_HEADS = 8
HEAD_DIM = 64
V_DIM = 2 * HEAD_DIM
POOL_WINDOWS = (2, 4, 8, 16)
ROPE_BASE = 10000.0
EPS = 1e-6

LANES = 128
BF16_SUBLANES = 16
VMEM_LIMIT_BYTES = 56 * 1024 * 1024

MOD_ROWS = 16
MOD_TN = 512
HALO = BF16_SUBLANES
FFN_CHUNK = 256
TOKEN_TILE = 512
QUERY_TILE = 256


def _dot(a, b):
    return jnp.dot(a, b, preferred_element_type=F32)


def _const_spec(shape):
    zeros = (0,) * len(shape)
    return pl.BlockSpec(shape, lambda *_: zeros, pipeline_mode=pl.Buffered(1))


def _params(*sem):
    return pltpu.CompilerParams(dimension_semantics=sem, vmem_limit_bytes=VMEM_LIMIT_BYTES)


def _mod_kernel(cc_ref, w_ref, b_ref, lamv_ref, mod_ref, lam_ref, *, lam_init):
    c = cc_ref[...]
    sc = (c * jax.nn.sigmoid(c)).astype(BF16)
    mod_ref[...] = _dot(sc, w_ref[...].astype(BF16)) + b_ref[...]
    lv = lamv_ref[...]
    p1 = jnp.sum(lv[0:1] * lv[1:2], axis=-1, keepdims=True)
    p2 = jnp.sum(lv[2:3] * lv[3:4], axis=-1, keepdims=True)
    lam = jnp.exp(p1) - jnp.exp(p2) + lam_init
    lam_ref[...] = jnp.broadcast_to(lam, lam_ref.shape)


def _mod_call(cc, w_ada, b_ada, lamv, lam_init):
    d, n = w_ada.shape
    return pl.pallas_call(
        functools.partial(_mod_kernel, lam_init=lam_init),
        grid=(n // MOD_TN,),
        in_specs=[
            pl.BlockSpec((MOD_ROWS, d), lambda j: (0, 0)),
            pl.BlockSpec((d, MOD_TN), lambda j: (0, j)),
            pl.BlockSpec((1, MOD_TN), lambda j: (0, j)),
            pl.BlockSpec(lamv.shape, lambda j: (0, 0)),
        ],
        out_specs=[
            pl.BlockSpec((MOD_ROWS, MOD_TN), lambda j: (0, j)),
            pl.BlockSpec((8, LANES), lambda j: (0, 0)),
        ],
        out_shape=[
            jax.ShapeDtypeStruct((MOD_ROWS, n), F32),
            jax.ShapeDtypeStruct((8, LANES), F32),
        ],
        compiler_params=_params("arbitrary"),
        name="mod",
    )(cc, w_ada, b_ada, lamv)


def _rms_modulate(x, g, shift, scale):
    ms = jnp.mean(x * x, axis=-1, keepdims=True)
    return x * lax.rsqrt(ms + EPS) * g * (1.0 + scale) + shift


def _inproj_kernel(*refs, rope, emit_f32, d_pool, d_attn):
    x_ref, shift_ref, scale_ref, g_ref, w_ref = refs[:5]
    pos = 5
    if rope:
        cos_ref, sa_ref, sb_ref = refs[pos:pos + 3]
        pos += 3
    u_ref, q_ref, k_ref, v_ref, gt_ref = refs[pos:pos + 5]
    pos += 5
    if emit_f32:
        k32_ref, v32_ref = refs[pos:pos + 2]

    hb = _rms_modulate(x_ref[0], g_ref[...], shift_ref[0], scale_ref[0]).astype(BF16)

    def rotate(t):
        if not rope:
            return t
        cos, sa, sb = cos_ref[...], sa_ref[...], sb_ref[...]
        outs = []
        for c in range(t.shape[1] // LANES):
            tc = t[:, c * LANES:(c + 1) * LANES]
            outs.append(tc * cos + pltpu.roll(tc, LANES - HEAD_DIM // 4, 1) * sa
                        + pltpu.roll(tc, HEAD_DIM // 4, 1) * sb)
        return jnp.concatenate(outs, axis=1)

    o0 = 0
    u_ref[0] = _dot(hb, w_ref[:, o0:o0 + d_pool]).astype(BF16)
    o0 += d_pool
    q = _dot(hb, w_ref[:, o0:o0 + d_attn])
    q_ref[0] = (rotate(q) * (HEAD_DIM ** -0.5)).astype(BF16)
    o0 += d_attn
    k = _dot(hb, w_ref[:, o0:o0 + d_attn])
    if emit_f32:
        k32_ref[0] = k
    k_ref[0] = rotate(k).astype(BF16)
    o0 += d_attn
    v = _dot(hb, w_ref[:, o0:o0 + d_attn])
    if emit_f32:
        v32_ref[0] = v
    v_ref[0] = v.astype(BF16)
    o0 += d_attn
    gt_ref[0] = _dot(hb, w_ref[:, o0:]).astype(BF16)


def _row_spec(n_rows):
    if n_rows == 1:
        return lambda b, s: (0, 0, 0)
    return lambda b, s: (b, 0, 0)


def _inproj_call(x, shift, scale, g, w_in, rope_tabs, *, tm, emit_f32, d_pool, d_attn):
    bsz, seq, d = x.shape
    d_in = w_in.shape[1]
    d_gate = d_in - d_pool - 3 * d_attn
    rope = rope_tabs is not None
    tok = lambda b, s: (b, s, 0)
    in_specs = [
        pl.BlockSpec((1, tm, d), tok),
        pl.BlockSpec((1, 1, d), _row_spec(shift.shape[0])),
        pl.BlockSpec((1, 1, d), _row_spec(scale.shape[0])),
        _const_spec((1, d)),
        _const_spec(w_in.shape),
    ]
    args = [x, shift, scale, g, w_in]
    if rope:
        in_specs += [pl.BlockSpec((tm, LANES), lambda b, s: (s, 0))] * 3
        args += list(rope_tabs)
    widths = [d_pool, d_attn, d_attn, d_attn, d_gate]
    out_specs = [pl.BlockSpec((1, tm, w), tok) for w in widths]
    out_shape = [jax.ShapeDtypeStruct((bsz, seq, w), BF16) for w in widths]
    if emit_f32:
        out_specs += [pl.BlockSpec((1, tm, d_attn), tok)] * 2
        out_shape += [jax.ShapeDtypeStruct((bsz, seq, d_attn), F32)] * 2
    return pl.pallas_call(
        functools.partial(_inproj_kernel, rope=rope, emit_f32=emit_f32, d_pool=d_pool, d_attn=d_attn),
        grid=(bsz, seq // tm),
        in_specs=in_specs,
        out_specs=out_specs,
        out_shape=out_shape,
        compiler_params=_params("parallel", "parallel"),
        name="in_proj_rope" if rope else "in_proj",
    )(*args)


def _attn_kernel(*refs, n_src, tq, group, out_scale):
    lam_ref, sub_ref, q_ref = refs[:3]
    kv_refs = refs[3:3 + 2 * n_src]
    o_ref = refs[3 + 2 * n_src]
    lam = lam_ref[0:1, 0:1]
    lane = lax.broadcasted_iota(jnp.int32, (tq, V_DIM), 1)

    def head_cols(h):
        return slice(h * V_DIM, (h + 1) * V_DIM)

    def score_phase(h):
        q = q_ref[0, :, head_cols(h)].astype(F32)
        qq = jnp.concatenate([jnp.where(lane < HEAD_DIM, q, 0.0),
                              jnp.where(lane >= HEAD_DIM, q, 0.0)], axis=0).astype(BF16)
        scores = []
        for i in range(n_src):
            k = kv_refs[2 * i][0, :, head_cols(h)].astype(BF16)
            scores.append(lax.dot_general(qq, k, (((1,), (1,)), ((), ())), preferred_element_type=F32))
        m = jnp.max(scores[0], axis=-1, keepdims=True)
        for s in scores[1:]:
            m = jnp.maximum(m, jnp.max(s, axis=-1, keepdims=True))
        return scores, m

    def value_phase(h, scores, m):
        pv = None
        for i in range(n_src):
            v = kv_refs[2 * i + 1][0, :, head_cols(h)].astype(BF16)
            ones_lane = lax.broadcasted_iota(jnp.int32, v.shape, 1) == 0
            v_ext = jnp.concatenate([v, jnp.where(ones_lane, 1.0, 0.0).astype(BF16)], axis=1)
            part = _dot(jnp.exp(scores[i] - m).astype(BF16), v_ext)
            pv = part if pv is None else pv + part
        return pv

    def output_phase(h, pv):
        r = 1.0 / pv[:, V_DIM:V_DIM + 1]
        o = pv[:tq, :V_DIM] * r[:tq] - pv[tq:, :V_DIM] * (r[tq:] * lam)
        ms = jnp.mean(o * o, axis=-1, keepdims=True)
        o_ref[0, :, head_cols(h)] = (o * lax.rsqrt(ms + EPS) * sub_ref[...] * out_scale).astype(BF16)

    for h0 in range(0, N_HEADS, group):
        heads = range(h0, h0 + group)
        scored = [score_phase(h) for h in heads]
        pvs = [value_phase(h, *sm) for h, sm in zip(heads, scored)]
        for h, pv in zip(heads, pvs):
            output_phase(h, pv)


def _attn_call(lam, subln, q, kvs, *, tq, group, out_scale):
    bsz, seq, d_attn = q.shape
    assert d_attn == N_HEADS * V_DIM
    in_specs = [
        _const_spec(lam.shape),
        _const_spec(subln.shape),
        pl.BlockSpec((1, tq, d_attn), lambda b, i: (b, i, 0)),
    ]
    args = [lam, subln, q]
    for kv in kvs:
        in_specs.append(pl.BlockSpec((1, kv.shape[1], d_attn), lambda b, i: (b, 0, 0)))
        args.append(kv)
    return pl.pallas_call(
        functools.partial(_attn_kernel, n_src=len(kvs) // 2, tq=tq, group=group, out_scale=out_scale),
        grid=(bsz, seq // tq),
        in_specs=in_specs,
        out_specs=pl.BlockSpec((1, tq, d_attn), lambda b, i: (b, i, 0)),
        out_shape=jax.ShapeDtypeStruct((bsz, seq, d_attn), BF16),
        compiler_params=_params("parallel", "parallel"),
        name="attn%d" % (len(kvs) // 2),
    )(*args)


def _fill_with_halo(dst_ref, prev_ref, mid, next_ref, tm):
    s = pl.program_id(1)
    last = pl.num_programs(1) - 1
    prev = prev_ref[0].astype(dst_ref.dtype)
    nxt = next_ref[0].astype(dst_ref.dtype)
    dst_ref[0:HALO] = jnp.where(s > 0, prev, jnp.zeros_like(prev))
    dst_ref[HALO:HALO + tm] = mid
    dst_ref[HALO + tm:] = jnp.where(s < last, nxt, jnp.zeros_like(nxt))


def _mix_kernel(x_ref, u_ref, uprev_ref, unext_ref, o_ref, gt_ref, gate1_ref, shift2_ref, scale2_ref,
                n2_ref, wpool_ref, pscale_ref, wpa_ref, wpb_ref, wout_ref,
                x1_ref, h2_ref, pad_ref, *, tm, seq):
    d = x_ref.shape[-1]
    uf = u_ref[0].astype(F32)
    _fill_with_halo(pad_ref, uprev_ref, uf, unext_ref, tm)
    t = pl.program_id(1) * tm + lax.broadcasted_iota(jnp.int32, (tm, 1), 0)

    mixed = []
    for gi, w in enumerate(POOL_WINDOWS):
        cols = slice(gi * LANES, (gi + 1) * LANES)
        acc = None
        for off in range(-(w // 2), w - w // 2):
            piece = pad_ref[HALO + off:HALO + off + tm, cols]
            acc = piece if acc is None else acc + piece
        lo = jnp.maximum(t - w // 2, 0)
        hi = jnp.minimum(t + (w - w // 2), seq)
        cnt = (hi - lo).astype(F32)
        pooled = acc / cnt - uf[:, cols]
        mixed.append(_dot(pooled.astype(BF16), wpool_ref[gi]) * pscale_ref[:, cols])
    mixed = jnp.concatenate(mixed, axis=1).astype(BF16)

    y_a = _dot(mixed, wpa_ref[...])
    y_b = _dot(o_ref[0], wpb_ref[...])
    g = jax.nn.sigmoid(gt_ref[0].astype(F32))
    merged = (g[:, :d] * y_a + g[:, d:] * y_b).astype(BF16)
    x1 = x_ref[0] + gate1_ref[0] * _dot(merged, wout_ref[...])
    x1_ref[0] = x1
    h2_ref[0] = _rms_modulate(x1, n2_ref[...], shift2_ref[0], scale2_ref[0]).astype(BF16)


def _halo_specs(width, tm, seq):
    blocks_per_tile = tm // HALO
    last_block = seq // HALO - 1
    prev = pl.BlockSpec((1, HALO, width), lambda b, s: (b, jnp.maximum(s * blocks_per_tile - 1, 0), 0))
    nxt = pl.BlockSpec((1, HALO, width), lambda b, s: (b, jnp.minimum((s + 1) * blocks_per_tile, last_block), 0))
    return prev, nxt


def _mix_call(x, u, o, gates, gate1, shift2, scale2, n2, w_pool, pscale, wpa, wpb, wout, *, tm):
    bsz, seq, d = x.shape
    d_pool = u.shape[-1]
    tok = lambda b, s: (b, s, 0)
    uprev, unext = _halo_specs(d_pool, tm, seq)
    in_specs = [
        pl.BlockSpec((1, tm, d), tok),
        pl.BlockSpec((1, tm, d_pool), tok), uprev, unext,
        pl.BlockSpec((1, tm, o.shape[-1]), tok),
        pl.BlockSpec((1, tm, gates.shape[-1]), tok),
        pl.BlockSpec((1, 1, d), _row_spec(gate1.shape[0])),
        pl.BlockSpec((1, 1, d), _row_spec(shift2.shape[0])),
        pl.BlockSpec((1, 1, d), _row_spec(scale2.shape[0])),
        _const_spec(n2.shape), _const_spec(w_pool.shape), _const_spec(pscale.shape),
        _const_spec(wpa.shape), _const_spec(wpb.shape), _const_spec(wout.shape),
    ]
    return pl.pallas_call(
        functools.partial(_mix_kernel, tm=tm, seq=seq),
        grid=(bsz, seq // tm),
        in_specs=in_specs,
        out_specs=[pl.BlockSpec((1, tm, d), tok), pl.BlockSpec((1, tm, d), tok)],
        out_shape=[jax.ShapeDtypeStruct((bsz, seq, d), F32), jax.ShapeDtypeStruct((bsz, seq, d), BF16)],
        scratch_shapes=[pltpu.VMEM((tm + 2 * HALO, d_pool), F32)],
        compiler_params=_params("parallel", "parallel"),
        name="mix",
    )(x, u, u, u, o, gates, gate1, shift2, scale2, n2, w_pool, pscale, wpa, wpb, wout)


def _ffn_kernel(x1_ref, h2_ref, hprev_ref, hnext_ref, win_ref, cw_ref, cb_ref, wout_ref, gate2_ref, fnw_ref,
                y_ref, hext_ref, *, tm, d_ff):
    _fill_with_halo(hext_ref, hprev_ref, h2_ref[0], hnext_ref, tm)
    he = hext_ref[...]
    h2 = h2_ref[0]
    acc = None
    for c in range(d_ff // FFN_CHUNK):
        cols = slice(c * FFN_CHUNK, (c + 1) * FFN_CHUNK)
        a_ext = _dot(he, win_ref[:, cols])
        up = _dot(h2, win_ref[:, d_ff + c * FFN_CHUNK:d_ff + (c + 1) * FFN_CHUNK])
        rows = a_ext.shape[0]
        conv = (pltpu.roll(a_ext, 1, 0)[HALO:HALO + tm] * cw_ref[0:1, cols]
                + a_ext[HALO:HALO + tm] * cw_ref[1:2, cols]
                + pltpu.roll(a_ext, rows - 1, 0)[HALO:HALO + tm] * cw_ref[2:3, cols]
                + cb_ref[:, cols])
        act = (conv * jax.nn.sigmoid(conv) * up).astype(BF16)
        part = _dot(act, wout_ref[cols, :])
        acc = part if acc is None else acc + part
    x2 = x1_ref[0] + gate2_ref[0] * acc
    ms = jnp.mean(x2 * x2, axis=-1, keepdims=True)
    y_ref[0] = x2 * lax.rsqrt(ms + EPS) * fnw_ref[...]


def _ffn_call(x1, h2, w_ffn_in, conv_w, conv_b, w_ffn_out, gate2, fnw, *, tm):
    bsz, seq, d = x1.shape
    d_ff = w_ffn_out.shape[0]
    tok = lambda b, s: (b, s, 0)
    hprev, hnext = _halo_specs(d, tm, seq)
    in_specs = [
        pl.BlockSpec((1, tm, d), tok),
        pl.BlockSpec((1, tm, d), tok), hprev, hnext,
        _const_spec(w_ffn_in.shape), _const_spec(conv_w.shape), _const_spec(conv_b.shape),
        _const_spec(w_ffn_out.shape),
        pl.BlockSpec((1, 1, d), _row_spec(gate2.shape[0])),
        _const_spec(fnw.shape),
    ]
    return pl.pallas_call(
        functools.partial(_ffn_kernel, tm=tm, d_ff=d_ff),
        grid=(bsz, seq // tm),
        in_specs=in_specs,
        out_specs=pl.BlockSpec((1, tm, d), tok),
        out_shape=jax.ShapeDtypeStruct((bsz, seq, d), F32),
        scratch_shapes=[pltpu.VMEM((tm + 2 * HALO, d), BF16)],
        compiler_params=_params("parallel", "parallel"),
        name="ffn",
    )(x1, h2, h2, h2, w_ffn_in, conv_w, conv_b, w_ffn_out, gate2, fnw)


def _rope_tables(n_tok):
    rows = n_tok // GRID_W
    row = jnp.repeat(jnp.arange(rows, dtype=F32), GRID_W)
    col = jnp.tile(jnp.arange(GRID_W, dtype=F32), rows)
    n_freq = HEAD_DIM // 4
    inv = ROPE_BASE ** (-jnp.arange(n_freq, dtype=F32) / n_freq)
    ang_row = row[:, None] * inv
    ang_col = col[:, None] * inv
    zeros = jnp.zeros_like(ang_row)
    cos_row, sin_row = jnp.cos(ang_row), jnp.sin(ang_row)
    cos_col, sin_col = jnp.cos(ang_col), jnp.sin(ang_col)
    cos = jnp.concatenate([cos_row, cos_row, cos_col, cos_col], axis=1)
    sa = jnp.concatenate([-sin_row, zeros, -sin_col, zeros], axis=1)
    sb = jnp.concatenate([zeros, sin_row, zeros, sin_col], axis=1)
    rep = LANES // HEAD_DIM
    return tuple(jnp.tile(t, (1, rep)) for t in (cos, sa, sb))


def _stream(x, mods, w, lam, ctx_kv, *, lam_init, final_norm_w):
    shift1, scale1, gate1, shift2, scale2, gate2 = mods
    tm = min(x.shape[1], TOKEN_TILE)
    tq = min(x.shape[1], QUERY_TILE)
    d_pool = w["w_proj_a"].shape[0]
    d_attn = w["w_proj_b"].shape[0]
    is_ctx = ctx_kv is None
    rope_tabs = None if is_ctx else _rope_tables(x.shape[1])
    outs = _inproj_call(x, shift1, scale1, w["norm1_w"], w["w_in"], rope_tabs,
                        tm=tm, emit_f32=is_ctx, d_pool=d_pool, d_attn=d_attn)
    u, q, k, v, gates = outs[:5]
    kvs = [k, v] if is_ctx else [ctx_kv[0], ctx_kv[1], k, v]
    o = _attn_call(lam, w["subln_w"], q, kvs, tq=tq, group=N_HEADS if is_ctx else 1,
                   out_scale=1.0 - lam_init)
    x1, h2 = _mix_call(x, u, o, gates, gate1, shift2, scale2, w["norm2_w"], w["w_pool"], w["pool_scale"],
                       w["w_proj_a"], w["w_proj_b"], w["w_out"], tm=tm)
    y = _ffn_call(x1, h2, w["w_ffn_in"], w["ffn_conv_w"], w["ffn_conv_b"], w["w_ffn_out"], gate2,
                  final_norm_w, tm=tm)
    return y, outs[5:]


def kernel(x_prompt, x_sample, cache_k, cache_v, c, c_ctx, w_ada, b_ada, w_in, w_pool, pool_scale, w_proj_a, w_proj_b, w_out, lam_q1, lam_k1, lam_q2, lam_k2, subln_w, norm1_w, norm2_w, w_ffn_in, ffn_conv_w, ffn_conv_b, w_ffn_out, final_norm_w):
    assert w_ada.shape[0] == 1, "single trunk layer"
    bsz, seq, d = x_prompt.shape
    dec_b, dec_seq, _ = x_sample.shape
    assert 1 + dec_b <= MOD_ROWS
    lam_init = 0.8 - 0.6 * math.exp(-0.3 * 0)

    cc = jnp.zeros((MOD_ROWS, d), F32).at[0].set(c_ctx).at[1:1 + dec_b].set(c)
    lamv = jnp.concatenate([lam_q1, lam_k1, lam_q2, lam_k2], axis=0)
    mod, lam = _mod_call(cc, w_ada[0], b_ada, lamv, lam_init)
    mod = mod.reshape(MOD_ROWS, 6, 1, d)
    mods_ctx = [mod[0:1, j] for j in range(6)]
    mods_lat = [mod[1:1 + dec_b, j] for j in range(6)]

    w = dict(
        w_in=w_in[0].astype(BF16), w_pool=w_pool[0].astype(BF16), pool_scale=pool_scale,
        w_proj_a=w_proj_a[0].astype(BF16), w_proj_b=w_proj_b[0].astype(BF16), w_out=w_out[0].astype(BF16),
        subln_w=subln_w, norm1_w=norm1_w, norm2_w=norm2_w,
        w_ffn_in=w_ffn_in[0].astype(BF16), ffn_conv_w=ffn_conv_w[0], ffn_conv_b=ffn_conv_b,
        w_ffn_out=w_ffn_out[0].astype(BF16),
    )
    fnw = final_norm_w.reshape(1, d)

    y_prompt, (k32, v32) = _stream(x_prompt, mods_ctx, w, lam, None, lam_init=lam_init, final_norm_w=fnw)
    d_attn = k32.shape[-1]
    ctx_kv = (cache_k[:, 0].reshape(dec_b, -1, d_attn), cache_v[:, 0].reshape(dec_b, -1, d_attn))
    y_sample, _ = _stream(x_sample, mods_lat, w, lam, ctx_kv, lam_init=lam_init, final_norm_w=fnw)

    new_cache_k = k32.reshape(bsz, 1, seq, N_HEADS, 2, HEAD_DIM)
    new_cache_v = v32.reshape(bsz, 1, seq, N_HEADS, V_DIM)
    return (y_prompt, y_sample, new_cache_k, new_cache_v)
```

```python
import functools
import math

import jax
import jax.numpy as jnp
from jax import lax
from jax.experimental import pallas as pl
from jax.experimental.pallas import tpu as pltpu

F32 = jnp.float32
BF16 = jnp.bfloat16

GRID_W = 64
N_HEADS = 8
HEAD_DIM = 64
V_DIM = 2 * HEAD_DIM
POOL_WINDOWS = (2, 4, 8, 16)
ROPE_BASE = 10000.0
EPS = 1e-6

LANES = 128
BF16_SUBLANES = 16
VMEM_LIMIT_BYTES = 56 * 1024 * 1024

MOD_ROWS = 16
MOD_TN = 512
HALO = BF16_SUBLANES
FFN_CHUNK = 256
TOKEN_TILE = 512
QUERY_TILE = 256
KEY_CHUNK = 1024


def _dot(a, b):
    return jnp.dot(a, b, preferred_element_type=F32)


def _const_spec(shape):
    zeros = (0,) * len(shape)
    return pl.BlockSpec(shape, lambda *_: zeros, pipeline_mode=pl.Buffered(1))


def _params(*sem):
    return pltpu.CompilerParams(dimension_semantics=sem, vmem_limit_bytes=VMEM_LIMIT_BYTES)


def _mod_kernel(cc_ref, w_ref, b_ref, lamv_ref, mod_ref, lam_ref, *, lam_init):
    c = cc_ref[...]
    sc = (c * jax.nn.sigmoid(c)).astype(BF16)
    mod_ref[...] = _dot(sc, w_ref[...].astype(BF16)) + b_ref[...]
    lv = lamv_ref[...]
    p1 = jnp.sum(lv[0:1] * lv[1:2], axis=-1, keepdims=True)
    p2 = jnp.sum(lv[2:3] * lv[3:4], axis=-1, keepdims=True)
    lam = jnp.exp(p1) - jnp.exp(p2) + lam_init
    lam_ref[...] = jnp.broadcast_to(lam, lam_ref.shape)


def _mod_call(cc, w_ada, b_ada, lamv, lam_init):
    d, n = w_ada.shape
    return pl.pallas_call(
        functools.partial(_mod_kernel, lam_init=lam_init),
        grid=(n // MOD_TN,),
        in_specs=[
            pl.BlockSpec((MOD_ROWS, d), lambda j: (0, 0)),
            pl.BlockSpec((d, MOD_TN), lambda j: (0, j)),
            pl.BlockSpec((1, MOD_TN), lambda j: (0, j)),
            pl.BlockSpec(lamv.shape, lambda j: (0, 0)),
        ],
        out_specs=[
            pl.BlockSpec((MOD_ROWS, MOD_TN), lambda j: (0, j)),
            pl.BlockSpec((8, LANES), lambda j: (0, 0)),
        ],
        out_shape=[
            jax.ShapeDtypeStruct((MOD_ROWS, n), F32),
            jax.ShapeDtypeStruct((8, LANES), F32),
        ],
        compiler_params=_params("arbitrary"),
        name="mod",
    )(cc, w_ada, b_ada, lamv)


def _rms_modulate(x, g, shift, scale):
    ms = jnp.mean(x * x, axis=-1, keepdims=True)
    return x * lax.rsqrt(ms + EPS) * g * (1.0 + scale) + shift


def _inproj_kernel(*refs, rope, emit_f32, d_pool, d_attn):
    x_ref, shift_ref, scale_ref, g_ref, w_ref = refs[:5]
    pos = 5
    if rope:
        cos_ref, sa_ref, sb_ref = refs[pos:pos + 3]
        pos += 3
    u_ref, q_ref, k_ref, v_ref, gt_ref = refs[pos:pos + 5]
    pos += 5
    if emit_f32:
        k32_ref, v32_ref = refs[pos:pos + 2]

    hb = _rms_modulate(x_ref[0], g_ref[...], shift_ref[0], scale_ref[0]).astype(BF16)

    def rotate(t):
        if not rope:
            return t
        cos, sa, sb = cos_ref[...], sa_ref[...], sb_ref[...]
        outs = []
        for c in range(t.shape[1] // LANES):
            tc = t[:, c * LANES:(c + 1) * LANES]
            outs.append(tc * cos + pltpu.roll(tc, LANES - HEAD_DIM // 4, 1) * sa
                        + pltpu.roll(tc, HEAD_DIM // 4, 1) * sb)
        return jnp.concatenate(outs, axis=1)

    o0 = 0
    u_ref[0] = _dot(hb, w_ref[:, o0:o0 + d_pool]).astype(BF16)
    o0 += d_pool
    q = _dot(hb, w_ref[:, o0:o0 + d_attn])
    q_ref[0] = (rotate(q) * (HEAD_DIM ** -0.5)).astype(BF16)
    o0 += d_attn
    k = _dot(hb, w_ref[:, o0:o0 + d_attn])
    if emit_f32:
        k32_ref[0] = k
    k_ref[0] = rotate(k).astype(BF16)
    o0 += d_attn
    v = _dot(hb, w_ref[:, o0:o0 + d_attn])
    if emit_f32:
        v32_ref[0] = v
    v_ref[0] = v.astype(BF16)
    o0 += d_attn
    gt_ref[0] = _dot(hb, w_ref[:, o0:]).astype(BF16)


def _row_spec(n_rows):
    if n_rows == 1:
        return lambda b, s: (0, 0, 0)
    return lambda b, s: (b, 0, 0)


def _inproj_call(x, shift, scale, g, w_in, rope_tabs, *, tm, emit_f32, d_pool, d_attn):
    bsz, seq, d = x.shape
    d_in = w_in.shape[1]
    d_gate = d_in - d_pool - 3 * d_attn
    rope = rope_tabs is not None
    tok = lambda b, s: (b, s, 0)
    in_specs = [
        pl.BlockSpec((1, tm, d), tok),
        pl.BlockSpec((1, 1, d), _row_spec(shift.shape[0])),
        pl.BlockSpec((1, 1, d), _row_spec(scale.shape[0])),
        _const_spec((1, d)),
        _const_spec(w_in.shape),
    ]
    args = [x, shift, scale, g, w_in]
    if rope:
        in_specs += [pl.BlockSpec((tm, LANES), lambda b, s: (s, 0))] * 3
        args += list(rope_tabs)
    widths = [d_pool, d_attn, d_attn, d_attn, d_gate]
    out_specs = [pl.BlockSpec((1, tm, w), tok) for w in widths]
    out_shape = [jax.ShapeDtypeStruct((bsz, seq, w), BF16) for w in widths]
    if emit_f32:
        out_specs += [pl.BlockSpec((1, tm, d_attn), tok)] * 2
        out_shape += [jax.ShapeDtypeStruct((bsz, seq, d_attn), F32)] * 2
    return pl.pallas_call(
        functools.partial(_inproj_kernel, rope=rope, emit_f32=emit_f32, d_pool=d_pool, d_attn=d_attn),
        grid=(bsz, seq // tm),
        in_specs=in_specs,
        out_specs=out_specs,
        out_shape=out_shape,
        compiler_params=_params("parallel", "parallel"),
        name="in_proj_rope" if rope else "in_proj",
    )(*args)


def _attn_kernel(*refs, n_src, tq, group, out_scale):
    lam_ref, sub_ref, q_ref = refs[:3]
    kv_refs = refs[3:3 + 2 * n_src]
    o_ref = refs[3 + 2 * n_src]
    lam = lam_ref[0:1, 0:1]
    lane = lax.broadcasted_iota(jnp.int32, (tq, V_DIM), 1)
    chunks = []
    for i in range(n_src):
        n_keys = kv_refs[2 * i].shape[1]
        step = min(n_keys, KEY_CHUNK)
        chunks += [(i, r, r + step) for r in range(0, n_keys, step)]

    def head_cols(h):
        return slice(h * V_DIM, (h + 1) * V_DIM)

    def score_phase(h):
        q = q_ref[0, :, head_cols(h)].astype(F32)
        qq = jnp.concatenate([jnp.where(lane < HEAD_DIM, q, 0.0),
                              jnp.where(lane >= HEAD_DIM, q, 0.0)], axis=0).astype(BF16)
        scores = []
        for i, r0, r1 in chunks:
            k = kv_refs[2 * i][0, r0:r1, head_cols(h)].astype(BF16)
            scores.append(lax.dot_general(qq, k, (((1,), (1,)), ((), ())), preferred_element_type=F32))
        m = jnp.max(scores[0], axis=-1, keepdims=True)
        for s in scores[1:]:
            m = jnp.maximum(m, jnp.max(s, axis=-1, keepdims=True))
        return scores, m

    def value_phase(h, scores, m):
        pv = None
        for s, (i, r0, r1) in zip(scores, chunks):
            v = kv_refs[2 * i + 1][0, r0:r1, head_cols(h)].astype(BF16)
            ones_lane = lax.broadcasted_iota(jnp.int32, v.shape, 1) == 0
            v_ext = jnp.concatenate([v, jnp.where(ones_lane, 1.0, 0.0).astype(BF16)], axis=1)
            part = _dot(jnp.exp(s - m).astype(BF16), v_ext)
            pv = part if pv is None else pv + part
        return pv

    def output_phase(h, pv):
        r = 1.0 / pv[:, V_DIM:V_DIM + 1]
        o = pv[:tq, :V_DIM] * r[:tq] - pv[tq:, :V_DIM] * (r[tq:] * lam)
        ms = jnp.mean(o * o, axis=-1, keepdims=True)
        o_ref[0, :, head_cols(h)] = (o * lax.rsqrt(ms + EPS) * sub_ref[...] * out_scale).astype(BF16)

    for h0 in range(0, N_HEADS, group):
        heads = range(h0, h0 + group)
        scored = [score_phase(h) for h in heads]
        pvs = [value_phase(h, *sm) for h, sm in zip(heads, scored)]
        for h, pv in zip(heads, pvs):
            output_phase(h, pv)


def _attn_call(lam, subln, q, kvs, *, tq, group, out_scale):
    bsz, seq, d_attn = q.shape
    assert d_attn == N_HEADS * V_DIM
    in_specs = [
        _const_spec(lam.shape),
        _const_spec(subln.shape),
        pl.BlockSpec((1, tq, d_attn), lambda b, i: (b, i, 0)),
    ]
    args = [lam, subln, q]
    for kv in kvs:
        in_specs.append(pl.BlockSpec((1, kv.shape[1], d_attn), lambda b, i: (b, 0, 0)))
        args.append(kv)
    return pl.pallas_call(
        functools.partial(_attn_kernel, n_src=len(kvs) // 2, tq=tq, group=group, out_scale=out_scale),
        grid=(bsz, seq // tq),
        in_specs=in_specs,
        out_specs=pl.BlockSpec((1, tq, d_attn), lambda b, i: (b, i, 0)),
        out_shape=jax.ShapeDtypeStruct((bsz, seq, d_attn), BF16),
        compiler_params=_params("parallel", "parallel"),
        name="attn%d" % (len(kvs) // 2),
    )(*args)


def _fill_with_halo(dst_ref, prev_ref, mid, next_ref, tm):
    s = pl.program_id(1)
    last = pl.num_programs(1) - 1
    prev = prev_ref[0].astype(dst_ref.dtype)
    nxt = next_ref[0].astype(dst_ref.dtype)
    dst_ref[0:HALO] = jnp.where(s > 0, prev, jnp.zeros_like(prev))
    dst_ref[HALO:HALO + tm] = mid
    dst_ref[HALO + tm:] = jnp.where(s < last, nxt, jnp.zeros_like(nxt))


def _mix_kernel(x_ref, u_ref, uprev_ref, unext_ref, o_ref, gt_ref, gate1_ref, shift2_ref, scale2_ref,
                n2_ref, wpool_ref, pscale_ref, wpa_ref, wpb_ref, wout_ref,
                x1_ref, h2_ref, pad_ref, *, tm, seq):
    d = x_ref.shape[-1]
    uf = u_ref[0].astype(F32)
    _fill_with_halo(pad_ref, uprev_ref, uf, unext_ref, tm)
    t = pl.program_id(1) * tm + lax.broadcasted_iota(jnp.int32, (tm, 1), 0)

    n_groups = len(POOL_WINDOWS)
    cols_per_group = d // n_groups
    mixed, y_b = [], []
    for gi, w in enumerate(POOL_WINDOWS):
        cols = slice(gi * LANES, (gi + 1) * LANES)
        y_b.append(_dot(o_ref[0], wpb_ref[:, gi * cols_per_group:(gi + 1) * cols_per_group]))
        acc = None
        for off in range(-(w // 2), w - w // 2):
            piece = pad_ref[HALO + off:HALO + off + tm, cols]
            acc = piece if acc is None else acc + piece
        lo = jnp.maximum(t - w // 2, 0)
        hi = jnp.minimum(t + (w - w // 2), seq)
        cnt = (hi - lo).astype(F32)
        pooled = acc / cnt - uf[:, cols]
        mixed.append(_dot(pooled.astype(BF16), wpool_ref[gi]) * pscale_ref[:, cols])
    mixed = jnp.concatenate(mixed, axis=1).astype(BF16)
    y_b = jnp.concatenate(y_b, axis=1)

    y_a = _dot(mixed, wpa_ref[...])
    g = jax.nn.sigmoid(gt_ref[0].astype(F32))
    merged = (g[:, :d] * y_a + g[:, d:] * y_b).astype(BF16)
    x1 = x_ref[0] + gate1_ref[0] * _dot(merged, wout_ref[...])
    x1_ref[0] = x1
    h2_ref[0] = _rms_modulate(x1, n2_ref[...], shift2_ref[0], scale2_ref[0]).astype(BF16)


def _halo_specs(width, tm, seq):
    blocks_per_tile = tm // HALO
    last_block = seq // HALO - 1
    prev = pl.BlockSpec((1, HALO, width), lambda b, s: (b, jnp.maximum(s * blocks_per_tile - 1, 0), 0))
    nxt = pl.BlockSpec((1, HALO, width), lambda b, s: (b, jnp.minimum((s + 1) * blocks_per_tile, last_block), 0))
    return prev, nxt


def _mix_call(x, u, o, gates, gate1, shift2, scale2, n2, w_pool, pscale, wpa, wpb, wout, *, tm):
    bsz, seq, d = x.shape
    d_pool = u.shape[-1]
    tok = lambda b, s: (b, s, 0)
    uprev, unext = _halo_specs(d_pool, tm, seq)
    in_specs = [
        pl.BlockSpec((1, tm, d), tok),
        pl.BlockSpec((1, tm, d_pool), tok), uprev, unext,
        pl.BlockSpec((1, tm, o.shape[-1]), tok),
        pl.BlockSpec((1, tm, gates.shape[-1]), tok),
        pl.BlockSpec((1, 1, d), _row_spec(gate1.shape[0])),
        pl.BlockSpec((1, 1, d), _row_spec(shift2.shape[0])),
        pl.BlockSpec((1, 1, d), _row_spec(scale2.shape[0])),
        _const_spec(n2.shape), _const_spec(w_pool.shape), _const_spec(pscale.shape),
        _const_spec(wpa.shape), _const_spec(wpb.shape), _const_spec(wout.shape),
    ]
    return pl.pallas_call(
        functools.partial(_mix_kernel, tm=tm, seq=seq),
        grid=(bsz, seq // tm),
        in_specs=in_specs,
        out_specs=[pl.BlockSpec((1, tm, d), tok), pl.BlockSpec((1, tm, d), tok)],
        out_shape=[jax.ShapeDtypeStruct((bsz, seq, d), F32), jax.ShapeDtypeStruct((bsz, seq, d), BF16)],
        scratch_shapes=[pltpu.VMEM((tm + 2 * HALO, d_pool), F32)],
        compiler_params=_params("parallel", "parallel"),
        name="mix",
    )(x, u, u, u, o, gates, gate1, shift2, scale2, n2, w_pool, pscale, wpa, wpb, wout)


def _ffn_kernel(x1_ref, h2_ref, hprev_ref, hnext_ref, win_ref, cw_ref, cb_ref, wout_ref, gate2_ref, fnw_ref,
                y_ref, hext_ref, act_ref, *, tm, d_ff):
    _fill_with_halo(hext_ref, hprev_ref, h2_ref[0], hnext_ref, tm)
    he = hext_ref[...]
    h2 = h2_ref[0]
    for c in range(d_ff // FFN_CHUNK):
        cols = slice(c * FFN_CHUNK, (c + 1) * FFN_CHUNK)
        a_ext = _dot(he, win_ref[:, cols])
        up = _dot(h2, win_ref[:, d_ff + c * FFN_CHUNK:d_ff + (c + 1) * FFN_CHUNK])
        rows = a_ext.shape[0]
        conv = (pltpu.roll(a_ext, 1, 0)[HALO:HALO + tm] * cw_ref[0:1, cols]
                + a_ext[HALO:HALO + tm] * cw_ref[1:2, cols]
                + pltpu.roll(a_ext, rows - 1, 0)[HALO:HALO + tm] * cw_ref[2:3, cols]
                + cb_ref[:, cols])
        act_ref[:, cols] = (conv * jax.nn.sigmoid(conv) * up).astype(BF16)
    x2 = x1_ref[0] + gate2_ref[0] * _dot(act_ref[...], wout_ref[...])
    ms = jnp.mean(x2 * x2, axis=-1, keepdims=True)
    y_ref[0] = x2 * lax.rsqrt(ms + EPS) * fnw_ref[...]


def _ffn_call(x1, h2, w_ffn_in, conv_w, conv_b, w_ffn_out, gate2, fnw, *, tm):
    bsz, seq, d = x1.shape
    d_ff = w_ffn_out.shape[0]
    tok = lambda b, s: (b, s, 0)
    hprev, hnext = _halo_specs(d, tm, seq)
    in_specs = [
        pl.BlockSpec((1, tm, d), tok),
        pl.BlockSpec((1, tm, d), tok), hprev, hnext,
        _const_spec(w_ffn_in.shape), _const_spec(conv_w.shape), _const_spec(conv_b.shape),
        _const_spec(w_ffn_out.shape),
        pl.BlockSpec((1, 1, d), _row_spec(gate2.shape[0])),
        _const_spec(fnw.shape),
    ]
    return pl.pallas_call(
        functools.partial(_ffn_kernel, tm=tm, d_ff=d_ff),
        grid=(bsz, seq // tm),
        in_specs=in_specs,
        out_specs=pl.BlockSpec((1, tm, d), tok),
        out_shape=jax.ShapeDtypeStruct((bsz, seq, d), F32),
        scratch_shapes=[pltpu.VMEM((tm + 2 * HALO, d), BF16), pltpu.VMEM((tm, d_ff), BF16)],
        compiler_params=_params("parallel", "parallel"),
        name="ffn",
    )(x1, h2, h2, h2, w_ffn_in, conv_w, conv_b, w_ffn_out, gate2, fnw)


def _rope_tables(n_tok):
    rows = n_tok // GRID_W
    row = jnp.repeat(jnp.arange(rows, dtype=F32), GRID_W)
    col = jnp.tile(jnp.arange(GRID_W, dtype=F32), rows)
    n_freq = HEAD_DIM // 4
    inv = ROPE_BASE ** (-jnp.arange(n_freq, dtype=F32) / n_freq)
    ang_row = row[:, None] * inv
    ang_col = col[:, None] * inv
    zeros = jnp.zeros_like(ang_row)
    cos_row, sin_row = jnp.cos(ang_row), jnp.sin(ang_row)
    cos_col, sin_col = jnp.cos(ang_col), jnp.sin(ang_col)
    cos = jnp.concatenate([cos_row, cos_row, cos_col, cos_col], axis=1)
    sa = jnp.concatenate([-sin_row, zeros, -sin_col, zeros], axis=1)
    sb = jnp.concatenate([zeros, sin_row, zeros, sin_col], axis=1)
    rep = LANES // HEAD_DIM
    return tuple(jnp.tile(t, (1, rep)) for t in (cos, sa, sb))


def _stream(x, mods, w, lam, ctx_kv, *, lam_init, final_norm_w):
    shift1, scale1, gate1, shift2, scale2, gate2 = mods
    tm = min(x.shape[1], TOKEN_TILE)
    tq = min(x.shape[1], QUERY_TILE)
    d_pool = w["w_proj_a"].shape[0]
    d_attn = w["w_proj_b"].shape[0]
    is_ctx = ctx_kv is None
    rope_tabs = None if is_ctx else _rope_tables(x.shape[1])
    outs = _inproj_call(x, shift1, scale1, w["norm1_w"], w["w_in"], rope_tabs,
                        tm=tm, emit_f32=is_ctx, d_pool=d_pool, d_attn=d_attn)
    u, q, k, v, gates = outs[:5]
    kvs = [k, v] if is_ctx else [ctx_kv[0], ctx_kv[1], k, v]
    o = _attn_call(lam, w["subln_w"], q, kvs, tq=tq, group=N_HEADS if is_ctx else 1,
                   out_scale=1.0 - lam_init)
    x1, h2 = _mix_call(x, u, o, gates, gate1, shift2, scale2, w["norm2_w"], w["w_pool"], w["pool_scale"],
                       w["w_proj_a"], w["w_proj_b"], w["w_out"], tm=tm)
    y = _ffn_call(x1, h2, w["w_ffn_in"], w["ffn_conv_w"], w["ffn_conv_b"], w["w_ffn_out"], gate2,
                  final_norm_w, tm=tm)
    return y, outs[5:]


def kernel(x_prompt, x_sample, cache_k, cache_v, c, c_ctx, w_ada, b_ada, w_in, w_pool, pool_scale, w_proj_a, w_proj_b, w_out, lam_q1, lam_k1, lam_q2, lam_k2, subln_w, norm1_w, norm2_w, w_ffn_in, ffn_conv_w, ffn_conv_b, w_ffn_out, final_norm_w):
    assert w_ada.shape[0] == 1, "single trunk layer"
    bsz, seq, d = x_prompt.shape
    dec_b, dec_seq, _ = x_sample.shape
    assert 1 + dec_b <= MOD_ROWS
    lam_init = 0.8 - 0.6 * math.exp(-0.3 * 0)

    cc = jnp.zeros((MOD_ROWS, d), F32).at[0].set(c_ctx).at[1:1 + dec_b].set(c)
    lamv = jnp.concatenate([lam_q1, lam_k1, lam_q2, lam_k2], axis=0)
    mod, lam = _mod_call(cc, w_ada[0], b_ada, lamv, lam_init)
    mod = mod.reshape(MOD_ROWS, 6, 1, d)
    mods_ctx = [mod[0:1, j] for j in range(6)]
    mods_lat = [mod[1:1 + dec_b, j] for j in range(6)]

    w = dict(
        w_in=w_in[0].astype(BF16), w_pool=w_pool[0].astype(BF16), pool_scale=pool_scale,
        w_proj_a=w_proj_a[0].astype(BF16), w_proj_b=w_proj_b[0].astype(BF16), w_out=w_out[0].astype(BF16),
        subln_w=subln_w, norm1_w=norm1_w, norm2_w=norm2_w,
        w_ffn_in=w_ffn_in[0].astype(BF16), ffn_conv_w=ffn_conv_w[0], ffn_conv_b=ffn_conv_b,
        w_ffn_out=w_ffn_out[0].astype(BF16),
    )
    fnw = final_norm_w.reshape(1, d)

    y_prompt, (k32, v32) = _stream(x_prompt, mods_ctx, w, lam, None, lam_init=lam_init, final_norm_w=fnw)
    d_attn = k32.shape[-1]
    ctx_kv = (cache_k[:, 0].reshape(dec_b, -1, d_attn), cache_v[:, 0].reshape(dec_b, -1, d_attn))
    y_sample, _ = _stream(x_sample, mods_lat, w, lam, ctx_kv, lam_init=lam_init, final_norm_w=fnw)

    new_cache_k = k32.reshape(bsz, 1, seq, N_HEADS, 2, HEAD_DIM)
    new_cache_v = v32.reshape(bsz, 1, seq, N_HEADS, V_DIM)
    return (y_prompt, y_sample, new_cache_k, new_cache_v)
```

```python
import functools
import math

import jax
import jax.numpy as jnp
from jax import lax
from jax.experimental import pallas as pl
from jax.experimental.pallas import tpu as pltpu

F32 = jnp.float32
BF16 = jnp.bfloat16

GRID_W = 64
N_HEADS = 8
HEAD_DIM = 64
V_DIM = 2 * HEAD_DIM
POOL_WINDOWS = (2, 4, 8, 16)
ROPE_BASE = 10000.0
EPS = 1e-6

LANES = 128
BF16_SUBLANES = 16
VMEM_LIMIT_BYTES = 56 * 1024 * 1024

MOD_ROWS = 16
MOD_TN = 512
HALO = BF16_SUBLANES
FFN_CHUNK = 256
TOKEN_TILE = 512
QUERY_TILE = 256
KEY_CHUNK = 256


def _dot(a, b):
    return jnp.dot(a, b, preferred_element_type=F32)


def _const_spec(shape):
    zeros = (0,) * len(shape)
    return pl.BlockSpec(shape, lambda *_: zeros, pipeline_mode=pl.Buffered(1))


def _params(*sem):
    return pltpu.CompilerParams(dimension_semantics=sem, vmem_limit_bytes=VMEM_LIMIT_BYTES)


def _mod_kernel(cc_ref, w_ref, b_ref, lamv_ref, mod_ref, lam_ref, *, lam_init):
    c = cc_ref[...]
    sc = (c * jax.nn.sigmoid(c)).astype(BF16)
    mod_ref[...] = _dot(sc, w_ref[...].astype(BF16)) + b_ref[...]
    lv = lamv_ref[...]
    p1 = jnp.sum(lv[0:1] * lv[1:2], axis=-1, keepdims=True)
    p2 = jnp.sum(lv[2:3] * lv[3:4], axis=-1, keepdims=True)
    lam = jnp.exp(p1) - jnp.exp(p2) + lam_init
    lam_ref[...] = jnp.broadcast_to(lam, lam_ref.shape)


def _mod_call(cc, w_ada, b_ada, lamv, lam_init):
    d, n = w_ada.shape
    return pl.pallas_call(
        functools.partial(_mod_kernel, lam_init=lam_init),
        grid=(n // MOD_TN,),
        in_specs=[
            pl.BlockSpec((MOD_ROWS, d), lambda j: (0, 0)),
            pl.BlockSpec((d, MOD_TN), lambda j: (0, j)),
            pl.BlockSpec((1, MOD_TN), lambda j: (0, j)),
            pl.BlockSpec(lamv.shape, lambda j: (0, 0)),
        ],
        out_specs=[
            pl.BlockSpec((MOD_ROWS, MOD_TN), lambda j: (0, j)),
            pl.BlockSpec((8, LANES), lambda j: (0, 0)),
        ],
        out_shape=[
            jax.ShapeDtypeStruct((MOD_ROWS, n), F32),
            jax.ShapeDtypeStruct((8, LANES), F32),
        ],
        compiler_params=_params("arbitrary"),
        name="mod",
    )(cc, w_ada, b_ada, lamv)


def _rms_modulate(x, g, shift, scale):
    ms = jnp.mean(x * x, axis=-1, keepdims=True)
    return x * lax.rsqrt(ms + EPS) * g * (1.0 + scale) + shift


def _inproj_kernel(*refs, rope, emit_f32, d_pool, d_attn):
    x_ref, shift_ref, scale_ref, g_ref, w_ref = refs[:5]
    pos = 5
    if rope:
        cos_ref, sa_ref, sb_ref = refs[pos:pos + 3]
        pos += 3
    u_ref, q_ref, k_ref, v_ref, gt_ref = refs[pos:pos + 5]
    pos += 5
    if emit_f32:
        k32_ref, v32_ref = refs[pos:pos + 2]

    hb = _rms_modulate(x_ref[0], g_ref[...], shift_ref[0], scale_ref[0]).astype(BF16)

    def rotate(t):
        if not rope:
            return t
        cos, sa, sb = cos_ref[...], sa_ref[...], sb_ref[...]
        outs = []
        for c in range(t.shape[1] // LANES):
            tc = t[:, c * LANES:(c + 1) * LANES]
            outs.append(tc * cos + pltpu.roll(tc, LANES - HEAD_DIM // 4, 1) * sa
                        + pltpu.roll(tc, HEAD_DIM // 4, 1) * sb)
        return jnp.concatenate(outs, axis=1)

    o0 = 0
    u_ref[0] = _dot(hb, w_ref[:, o0:o0 + d_pool]).astype(BF16)
    o0 += d_pool
    q = _dot(hb, w_ref[:, o0:o0 + d_attn])
    q_ref[0] = (rotate(q) * (HEAD_DIM ** -0.5)).astype(BF16)
    o0 += d_attn
    k = _dot(hb, w_ref[:, o0:o0 + d_attn])
    if emit_f32:
        k32_ref[0] = k
    k_ref[0] = rotate(k).astype(BF16)
    o0 += d_attn
    v = _dot(hb, w_ref[:, o0:o0 + d_attn])
    if emit_f32:
        v32_ref[0] = v
    v_ref[0] = v.T.astype(BF16)
    o0 += d_attn
    gt_ref[0] = _dot(hb, w_ref[:, o0:]).astype(BF16)


def _row_spec(n_rows):
    if n_rows == 1:
        return lambda b, s: (0, 0, 0)
    return lambda b, s: (b, 0, 0)


def _inproj_call(x, shift, scale, g, w_in, rope_tabs, *, tm, emit_f32, d_pool, d_attn):
    bsz, seq, d = x.shape
    d_in = w_in.shape[1]
    d_gate = d_in - d_pool - 3 * d_attn
    rope = rope_tabs is not None
    tok = lambda b, s: (b, s, 0)
    in_specs = [
        pl.BlockSpec((1, tm, d), tok),
        pl.BlockSpec((1, 1, d), _row_spec(shift.shape[0])),
        pl.BlockSpec((1, 1, d), _row_spec(scale.shape[0])),
        _const_spec((1, d)),
        _const_spec(w_in.shape),
    ]
    args = [x, shift, scale, g, w_in]
    if rope:
        in_specs += [pl.BlockSpec((tm, LANES), lambda b, s: (s, 0))] * 3
        args += list(rope_tabs)
    widths = [d_pool, d_attn, d_attn, d_attn, d_gate]
    out_specs = [pl.BlockSpec((1, tm, w), tok) for w in widths]
    out_shape = [jax.ShapeDtypeStruct((bsz, seq, w), BF16) for w in widths]
    out_specs[3] = pl.BlockSpec((1, d_attn, tm), lambda b, s: (b, 0, s))
    out_shape[3] = jax.ShapeDtypeStruct((bsz, d_attn, seq), BF16)
    if emit_f32:
        out_specs += [pl.BlockSpec((1, tm, d_attn), tok)] * 2
        out_shape += [jax.ShapeDtypeStruct((bsz, seq, d_attn), F32)] * 2
    return pl.pallas_call(
        functools.partial(_inproj_kernel, rope=rope, emit_f32=emit_f32, d_pool=d_pool, d_attn=d_attn),
        grid=(bsz, seq // tm),
        in_specs=in_specs,
        out_specs=out_specs,
        out_shape=out_shape,
        compiler_params=_params("parallel", "parallel"),
        name="in_proj_rope" if rope else "in_proj",
    )(*args)


def _attn_kernel(*refs, n_src, tq, group, out_scale):
    lam_ref, sub_ref, q_ref = refs[:3]
    kv_refs = refs[3:3 + 2 * n_src]
    o_ref = refs[3 + 2 * n_src]
    lam = lam_ref[0:1, 0:1]
    lane = lax.broadcasted_iota(jnp.int32, (tq, V_DIM), 1)
    chunks = []
    for i in range(n_src):
        n_keys = kv_refs[2 * i].shape[1]
        step = min(n_keys, KEY_CHUNK)
        chunks += [(i, r, r + step) for r in range(0, n_keys, step)]

    def head_cols(h):
        return slice(h * V_DIM, (h + 1) * V_DIM)

    def stacked_queries(h):
        q = q_ref[0, :, head_cols(h)].astype(F32)
        return jnp.concatenate([jnp.where(lane < HEAD_DIM, q, 0.0),
                                jnp.where(lane >= HEAD_DIM, q, 0.0)], axis=0).astype(BF16)

    def score_chunk(h, qq, chunk):
        i, r0, r1 = chunk
        k = kv_refs[2 * i][0, r0:r1, head_cols(h)].astype(BF16)
        return lax.dot_general(k, qq, (((1,), (1,)), ((), ())), preferred_element_type=F32)

    def value_chunk(h, e, chunk):
        i, r0, r1 = chunk
        v_t = kv_refs[2 * i + 1][0, head_cols(h), r0:r1].astype(BF16)
        ones_row = lax.broadcasted_iota(jnp.int32, (BF16_SUBLANES, r1 - r0), 0) == 0
        v_ext = jnp.concatenate([v_t, jnp.where(ones_row, 1.0, 0.0).astype(BF16)], axis=0)
        return _dot(v_ext, e)

    def column_max(scores):
        m = jnp.max(scores[0], axis=0, keepdims=True)
        for s in scores[1:]:
            m = jnp.maximum(m, jnp.max(s, axis=0, keepdims=True))
        return m

    def output_phase(h, pv):
        r = 1.0 / pv[V_DIM:V_DIM + 1, :]
        o_t = pv[:V_DIM, :tq] * r[:, :tq] - pv[:V_DIM, tq:] * (r[:, tq:] * lam)
        ms = jnp.mean(o_t * o_t, axis=0, keepdims=True)
        o = (o_t * lax.rsqrt(ms + EPS)).T
        o_ref[0, :, head_cols(h)] = (o * sub_ref[...] * out_scale).astype(BF16)

    scores, exps = {}, {}
    for t in range(N_HEADS + 2):
        h_exp, h_val = t - 1, t - 2
        do_score, do_exp, do_val = t < N_HEADS, 0 <= h_exp < N_HEADS, 0 <= h_val < N_HEADS
        if do_score:
            qq = stacked_queries(t)
            scores[t] = []
        if do_exp:
            m = column_max(scores[h_exp])
            exps[h_exp] = []
        pv = None
        for ci, c in enumerate(chunks):
            if do_score:
                scores[t].append(score_chunk(t, qq, c))
            if do_exp:
                exps[h_exp].append(jnp.exp(scores[h_exp][ci] - m).astype(BF16))
            if do_val:
                part = value_chunk(h_val, exps[h_val][ci], c)
                pv = part if pv is None else pv + part
        if do_exp:
            del scores[h_exp]
        if do_val:
            del exps[h_val]
            output_phase(h_val, pv)


def _attn_call(lam, subln, q, kvs, *, tq, group, out_scale):
    bsz, seq, d_attn = q.shape
    assert d_attn == N_HEADS * V_DIM
    in_specs = [
        _const_spec(lam.shape),
        _const_spec(subln.shape),
        pl.BlockSpec((1, tq, d_attn), lambda b, i: (b, i, 0)),
    ]
    args = [lam, subln, q]
    for kv in kvs:
        in_specs.append(pl.BlockSpec((1,) + kv.shape[1:], lambda b, i: (b, 0, 0)))
        args.append(kv)
    return pl.pallas_call(
        functools.partial(_attn_kernel, n_src=len(kvs) // 2, tq=tq, group=group, out_scale=out_scale),
        grid=(bsz, seq // tq),
        in_specs=in_specs,
        out_specs=pl.BlockSpec((1, tq, d_attn), lambda b, i: (b, i, 0)),
        out_shape=jax.ShapeDtypeStruct((bsz, seq, d_attn), BF16),
        compiler_params=_params("parallel", "parallel"),
        name="attn%d" % (len(kvs) // 2),
    )(*args)


def _fill_with_halo(dst_ref, prev_ref, mid, next_ref, tm):
    s = pl.program_id(1)
    last = pl.num_programs(1) - 1
    prev = prev_ref[0].astype(dst_ref.dtype)
    nxt = next_ref[0].astype(dst_ref.dtype)
    dst_ref[0:HALO] = jnp.where(s > 0, prev, jnp.zeros_like(prev))
    dst_ref[HALO:HALO + tm] = mid
    dst_ref[HALO + tm:] = jnp.where(s < last, nxt, jnp.zeros_like(nxt))


def _mix_kernel(x_ref, u_ref, uprev_ref, unext_ref, o_ref, gt_ref, gate1_ref, shift2_ref, scale2_ref,
                n2_ref, wpool_ref, pscale_ref, wpa_ref, wpb_ref, wout_ref,
                x1_ref, h2_ref, pad_ref, *, tm, seq):
    d = x_ref.shape[-1]
    uf = u_ref[0].astype(F32)
    _fill_with_halo(pad_ref, uprev_ref, uf, unext_ref, tm)
    t = pl.program_id(1) * tm + lax.broadcasted_iota(jnp.int32, (tm, 1), 0)

    n_groups = len(POOL_WINDOWS)
    cols_per_group = d // n_groups
    mixed, y_b = [], []
    for gi, w in enumerate(POOL_WINDOWS):
        cols = slice(gi * LANES, (gi + 1) * LANES)
        y_b.append(_dot(o_ref[0], wpb_ref[:, gi * cols_per_group:(gi + 1) * cols_per_group]))
        acc = None
        for off in range(-(w // 2), w - w // 2):
            piece = pad_ref[HALO + off:HALO + off + tm, cols]
            acc = piece if acc is None else acc + piece
        lo = jnp.maximum(t - w // 2, 0)
        hi = jnp.minimum(t + (w - w // 2), seq)
        cnt = (hi - lo).astype(F32)
        pooled = acc / cnt - uf[:, cols]
        mixed.append(_dot(pooled.astype(BF16), wpool_ref[gi]) * pscale_ref[:, cols])
    mixed = jnp.concatenate(mixed, axis=1).astype(BF16)
    y_b = jnp.concatenate(y_b, axis=1)

    y_a = _dot(mixed, wpa_ref[...])
    g = jax.nn.sigmoid(gt_ref[0].astype(F32))
    merged = (g[:, :d] * y_a + g[:, d:] * y_b).astype(BF16)
    x1 = x_ref[0] + gate1_ref[0] * _dot(merged, wout_ref[...])
    x1_ref[0] = x1
    h2_ref[0] = _rms_modulate(x1, n2_ref[...], shift2_ref[0], scale2_ref[0]).astype(BF16)


def _halo_specs(width, tm, seq):
    blocks_per_tile = tm // HALO
    last_block = seq // HALO - 1
    prev = pl.BlockSpec((1, HALO, width), lambda b, s: (b, jnp.maximum(s * blocks_per_tile - 1, 0), 0))
    nxt = pl.BlockSpec((1, HALO, width), lambda b, s: (b, jnp.minimum((s + 1) * blocks_per_tile, last_block), 0))
    return prev, nxt


def _mix_call(x, u, o, gates, gate1, shift2, scale2, n2, w_pool, pscale, wpa, wpb, wout, *, tm):
    bsz, seq, d = x.shape
    d_pool = u.shape[-1]
    tok = lambda b, s: (b, s, 0)
    uprev, unext = _halo_specs(d_pool, tm, seq)
    in_specs = [
        pl.BlockSpec((1, tm, d), tok),
        pl.BlockSpec((1, tm, d_pool), tok), uprev, unext,
        pl.BlockSpec((1, tm, o.shape[-1]), tok),
        pl.BlockSpec((1, tm, gates.shape[-1]), tok),
        pl.BlockSpec((1, 1, d), _row_spec(gate1.shape[0])),
        pl.BlockSpec((1, 1, d), _row_spec(shift2.shape[0])),
        pl.BlockSpec((1, 1, d), _row_spec(scale2.shape[0])),
        _const_spec(n2.shape), _const_spec(w_pool.shape), _const_spec(pscale.shape),
        _const_spec(wpa.shape), _const_spec(wpb.shape), _const_spec(wout.shape),
    ]
    return pl.pallas_call(
        functools.partial(_mix_kernel, tm=tm, seq=seq),
        grid=(bsz, seq // tm),
        in_specs=in_specs,
        out_specs=[pl.BlockSpec((1, tm, d), tok), pl.BlockSpec((1, tm, d), tok)],
        out_shape=[jax.ShapeDtypeStruct((bsz, seq, d), F32), jax.ShapeDtypeStruct((bsz, seq, d), BF16)],
        scratch_shapes=[pltpu.VMEM((tm + 2 * HALO, d_pool), F32)],
        compiler_params=_params("parallel", "parallel"),
        name="mix",
    )(x, u, u, u, o, gates, gate1, shift2, scale2, n2, w_pool, pscale, wpa, wpb, wout)


def _ffn_kernel(x1_ref, h2_ref, hprev_ref, hnext_ref, win_ref, cw_ref, cb_ref, wout_ref, gate2_ref, fnw_ref,
                y_ref, hext_ref, act_ref, *, tm, d_ff):
    _fill_with_halo(hext_ref, hprev_ref, h2_ref[0], hnext_ref, tm)
    he = hext_ref[...]
    h2 = h2_ref[0]
    for c in range(d_ff // FFN_CHUNK):
        cols = slice(c * FFN_CHUNK, (c + 1) * FFN_CHUNK)
        a_ext = _dot(he, win_ref[:, cols])
        up = _dot(h2, win_ref[:, d_ff + c * FFN_CHUNK:d_ff + (c + 1) * FFN_CHUNK])
        rows = a_ext.shape[0]
        conv = (pltpu.roll(a_ext, 1, 0)[HALO:HALO + tm] * cw_ref[0:1, cols]
                + a_ext[HALO:HALO + tm] * cw_ref[1:2, cols]
                + pltpu.roll(a_ext, rows - 1, 0)[HALO:HALO + tm] * cw_ref[2:3, cols]
                + cb_ref[:, cols])
        act_ref[:, cols] = (conv * jax.nn.sigmoid(conv) * up).astype(BF16)
    x2 = x1_ref[0] + gate2_ref[0] * _dot(act_ref[...], wout_ref[...])
    ms = jnp.mean(x2 * x2, axis=-1, keepdims=True)
    y_ref[0] = x2 * lax.rsqrt(ms + EPS) * fnw_ref[...]


def _ffn_call(x1, h2, w_ffn_in, conv_w, conv_b, w_ffn_out, gate2, fnw, *, tm):
    bsz, seq, d = x1.shape
    d_ff = w_ffn_out.shape[0]
    tok = lambda b, s: (b, s, 0)
    hprev, hnext = _halo_specs(d, tm, seq)
    in_specs = [
        pl.BlockSpec((1, tm, d), tok),
        pl.BlockSpec((1, tm, d), tok), hprev, hnext,
        _const_spec(w_ffn_in.shape), _const_spec(conv_w.shape), _const_spec(conv_b.shape),
        _const_spec(w_ffn_out.shape),
        pl.BlockSpec((1, 1, d), _row_spec(gate2.shape[0])),
        _const_spec(fnw.shape),
    ]
    return pl.pallas_call(
        functools.partial(_ffn_kernel, tm=tm, d_ff=d_ff),
        grid=(bsz, seq // tm),
        in_specs=in_specs,
        out_specs=pl.BlockSpec((1, tm, d), tok),
        out_shape=jax.ShapeDtypeStruct((bsz, seq, d), F32),
        scratch_shapes=[pltpu.VMEM((tm + 2 * HALO, d), BF16), pltpu.VMEM((tm, d_ff), BF16)],
        compiler_params=_params("parallel", "parallel"),
        name="ffn",
    )(x1, h2, h2, h2, w_ffn_in, conv_w, conv_b, w_ffn_out, gate2, fnw)


def _rope_tables(n_tok):
    rows = n_tok // GRID_W
    row = jnp.repeat(jnp.arange(rows, dtype=F32), GRID_W)
    col = jnp.tile(jnp.arange(GRID_W, dtype=F32), rows)
    n_freq = HEAD_DIM // 4
    inv = ROPE_BASE ** (-jnp.arange(n_freq, dtype=F32) / n_freq)
    ang_row = row[:, None] * inv
    ang_col = col[:, None] * inv
    zeros = jnp.zeros_like(ang_row)
    cos_row, sin_row = jnp.cos(ang_row), jnp.sin(ang_row)
    cos_col, sin_col = jnp.cos(ang_col), jnp.sin(ang_col)
    cos = jnp.concatenate([cos_row, cos_row, cos_col, cos_col], axis=1)
    sa = jnp.concatenate([-sin_row, zeros, -sin_col, zeros], axis=1)
    sb = jnp.concatenate([zeros, sin_row, zeros, sin_col], axis=1)
    rep = LANES // HEAD_DIM
    return tuple(jnp.tile(t, (1, rep)) for t in (cos, sa, sb))


def _stream(x, mods, w, lam, ctx_kv, *, lam_init, final_norm_w):
    shift1, scale1, gate1, shift2, scale2, gate2 = mods
    tm = min(x.shape[1], TOKEN_TILE)
    tq = min(x.shape[1], QUERY_TILE)
    d_pool = w["w_proj_a"].shape[0]
    d_attn = w["w_proj_b"].shape[0]
    is_ctx = ctx_kv is None
    rope_tabs = None if is_ctx else _rope_tables(x.shape[1])
    outs = _inproj_call(x, shift1, scale1, w["norm1_w"], w["w_in"], rope_tabs,
                        tm=tm, emit_f32=is_ctx, d_pool=d_pool, d_attn=d_attn)
    u, q, k, v, gates = outs[:5]
    kvs = [k, v] if is_ctx else [ctx_kv[0], ctx_kv[1], k, v]
    o = _attn_call(lam, w["subln_w"], q, kvs, tq=tq, group=N_HEADS if is_ctx else 1,
                   out_scale=1.0 - lam_init)
    x1, h2 = _mix_call(x, u, o, gates, gate1, shift2, scale2, w["norm2_w"], w["w_pool"], w["pool_scale"],
                       w["w_proj_a"], w["w_proj_b"], w["w_out"], tm=tm)
    y = _ffn_call(x1, h2, w["w_ffn_in"], w["ffn_conv_w"], w["ffn_conv_b"], w["w_ffn_out"], gate2,
                  final_norm_w, tm=tm)
    return y, outs[5:]


def kernel(x_prompt, x_sample, cache_k, cache_v, c, c_ctx, w_ada, b_ada, w_in, w_pool, pool_scale, w_proj_a, w_proj_b, w_out, lam_q1, lam_k1, lam_q2, lam_k2, subln_w, norm1_w, norm2_w, w_ffn_in, ffn_conv_w, ffn_conv_b, w_ffn_out, final_norm_w):
    assert w_ada.shape[0] == 1, "single trunk layer"
    bsz, seq, d = x_prompt.shape
    dec_b, dec_seq, _ = x_sample.shape
    assert 1 + dec_b <= MOD_ROWS
    lam_init = 0.8 - 0.6 * math.exp(-0.3 * 0)

    cc = jnp.zeros((MOD_ROWS, d), F32).at[0].set(c_ctx).at[1:1 + dec_b].set(c)
    lamv = jnp.concatenate([lam_q1, lam_k1, lam_q2, lam_k2], axis=0)
    mod, lam = _mod_call(cc, w_ada[0], b_ada, lamv, lam_init)
    mod = mod.reshape(MOD_ROWS, 6, 1, d)
    mods_ctx = [mod[0:1, j] for j in range(6)]
    mods_lat = [mod[1:1 + dec_b, j] for j in range(6)]

    w = dict(
        w_in=w_in[0].astype(BF16), w_pool=w_pool[0].astype(BF16), pool_scale=pool_scale,
        w_proj_a=w_proj_a[0].astype(BF16), w_proj_b=w_proj_b[0].astype(BF16), w_out=w_out[0].astype(BF16),
        subln_w=subln_w, norm1_w=norm1_w, norm2_w=norm2_w,
        w_ffn_in=w_ffn_in[0].astype(BF16), ffn_conv_w=ffn_conv_w[0], ffn_conv_b=ffn_conv_b,
        w_ffn_out=w_ffn_out[0].astype(BF16),
    )
    fnw = final_norm_w.reshape(1, d)

    y_prompt, (k32, v32) = _stream(x_prompt, mods_ctx, w, lam, None, lam_init=lam_init, final_norm_w=fnw)
    d_attn = k32.shape[-1]
    ctx_kv = (cache_k[:, 0].reshape(dec_b, -1, d_attn),
              jnp.swapaxes(cache_v[:, 0].reshape(dec_b, -1, d_attn), 1, 2))
    y_sample, _ = _stream(x_sample, mods_lat, w, lam, ctx_kv, lam_init=lam_init, final_norm_w=fnw)

    new_cache_k = k32.reshape(bsz, 1, seq, N_HEADS, 2, HEAD_DIM)
    new_cache_v = v32.reshape(bsz, 1, seq, N_HEADS, V_DIM)
    return (y_prompt, y_sample, new_cache_k, new_cache_v)
```

```python
import functools
import math

import jax
import jax.numpy as jnp
from jax import lax
from jax.experimental import pallas as pl
from jax.experimental.pallas import tpu as pltpu

F32 = jnp.float32
BF16 = jnp.bfloat16

GRID_W = 64
N_HEADS = 8
HEAD_DIM = 64
V_DIM = 2 * HEAD_DIM
POOL_WINDOWS = (2, 4, 8, 16)
ROPE_BASE = 10000.0
EPS = 1e-6

LANES = 128
BF16_SUBLANES = 16
VMEM_LIMIT_BYTES = 56 * 1024 * 1024

MOD_ROWS = 16
MOD_TN = 512
HALO = BF16_SUBLANES
FFN_CHUNK = 256
TOKEN_TILE = 512
QUERY_TILE = 512
QUERY_UNIT = 256
KEY_CHUNK = 256


def _dot(a, b):
    return jnp.dot(a, b, preferred_element_type=F32)


def _const_spec(shape):
    zeros = (0,) * len(shape)
    return pl.BlockSpec(shape, lambda *_: zeros, pipeline_mode=pl.Buffered(1))


def _params(*sem):
    return pltpu.CompilerParams(dimension_semantics=sem, vmem_limit_bytes=VMEM_LIMIT_BYTES)


def _mod_kernel(cc_ref, w_ref, b_ref, lamv_ref, mod_ref, lam_ref, *, lam_init):
    c = cc_ref[...]
    sc = (c * jax.nn.sigmoid(c)).astype(BF16)
    mod_ref[...] = _dot(sc, w_ref[...].astype(BF16)) + b_ref[...]
    lv = lamv_ref[...]
    p1 = jnp.sum(lv[0:1] * lv[1:2], axis=-1, keepdims=True)
    p2 = jnp.sum(lv[2:3] * lv[3:4], axis=-1, keepdims=True)
    lam = jnp.exp(p1) - jnp.exp(p2) + lam_init
    lam_ref[...] = jnp.broadcast_to(lam, lam_ref.shape)


def _mod_call(cc, w_ada, b_ada, lamv, lam_init):
    d, n = w_ada.shape
    return pl.pallas_call(
        functools.partial(_mod_kernel, lam_init=lam_init),
        grid=(n // MOD_TN,),
        in_specs=[
            pl.BlockSpec((MOD_ROWS, d), lambda j: (0, 0)),
            pl.BlockSpec((d, MOD_TN), lambda j: (0, j)),
            pl.BlockSpec((1, MOD_TN), lambda j: (0, j)),
            pl.BlockSpec(lamv.shape, lambda j: (0, 0)),
        ],
        out_specs=[
            pl.BlockSpec((MOD_ROWS, MOD_TN), lambda j: (0, j)),
            pl.BlockSpec((8, LANES), lambda j: (0, 0)),
        ],
        out_shape=[
            jax.ShapeDtypeStruct((MOD_ROWS, n), F32),
            jax.ShapeDtypeStruct((8, LANES), F32),
        ],
        compiler_params=_params("arbitrary"),
        name="mod",
    )(cc, w_ada, b_ada, lamv)


def _rms_modulate(x, g, shift, scale):
    ms = jnp.mean(x * x, axis=-1, keepdims=True)
    return x * lax.rsqrt(ms + EPS) * g * (1.0 + scale) + shift


def _inproj_kernel(*refs, rope, emit_f32, d_pool, d_attn):
    x_ref, shift_ref, scale_ref, g_ref, w_ref = refs[:5]
    pos = 5
    if rope:
        cos_ref, sa_ref, sb_ref = refs[pos:pos + 3]
        pos += 3
    u_ref, q_ref, k_ref, v_ref, gt_ref = refs[pos:pos + 5]
    pos += 5
    if emit_f32:
        k32_ref, v32_ref = refs[pos:pos + 2]

    hb = _rms_modulate(x_ref[0], g_ref[...], shift_ref[0], scale_ref[0]).astype(BF16)

    def rotate(t):
        if not rope:
            return t
        cos, sa, sb = cos_ref[...], sa_ref[...], sb_ref[...]
        outs = []
        for c in range(t.shape[1] // LANES):
            tc = t[:, c * LANES:(c + 1) * LANES]
            outs.append(tc * cos + pltpu.roll(tc, LANES - HEAD_DIM // 4, 1) * sa
                        + pltpu.roll(tc, HEAD_DIM // 4, 1) * sb)
        return jnp.concatenate(outs, axis=1)

    o0 = 0
    u_ref[0] = _dot(hb, w_ref[:, o0:o0 + d_pool]).astype(BF16)
    o0 += d_pool
    q = _dot(hb, w_ref[:, o0:o0 + d_attn])
    q_ref[0] = (rotate(q) * (HEAD_DIM ** -0.5)).astype(BF16)
    o0 += d_attn
    k = _dot(hb, w_ref[:, o0:o0 + d_attn])
    if emit_f32:
        k32_ref[0] = k
    k_ref[0] = rotate(k).astype(BF16)
    o0 += d_attn
    v = _dot(hb, w_ref[:, o0:o0 + d_attn])
    if emit_f32:
        v32_ref[0] = v
    v_ref[0] = v.T.astype(BF16)
    o0 += d_attn
    gt_ref[0] = _dot(hb, w_ref[:, o0:]).astype(BF16)


def _row_spec(n_rows):
    if n_rows == 1:
        return lambda b, s: (0, 0, 0)
    return lambda b, s: (b, 0, 0)


def _inproj_call(x, shift, scale, g, w_in, rope_tabs, *, tm, emit_f32, d_pool, d_attn):
    bsz, seq, d = x.shape
    d_in = w_in.shape[1]
    d_gate = d_in - d_pool - 3 * d_attn
    rope = rope_tabs is not None
    tok = lambda b, s: (b, s, 0)
    in_specs = [
        pl.BlockSpec((1, tm, d), tok),
        pl.BlockSpec((1, 1, d), _row_spec(shift.shape[0])),
        pl.BlockSpec((1, 1, d), _row_spec(scale.shape[0])),
        _const_spec((1, d)),
        _const_spec(w_in.shape),
    ]
    args = [x, shift, scale, g, w_in]
    if rope:
        in_specs += [pl.BlockSpec((tm, LANES), lambda b, s: (s, 0))] * 3
        args += list(rope_tabs)
    widths = [d_pool, d_attn, d_attn, d_attn, d_gate]
    out_specs = [pl.BlockSpec((1, tm, w), tok) for w in widths]
    out_shape = [jax.ShapeDtypeStruct((bsz, seq, w), BF16) for w in widths]
    out_specs[3] = pl.BlockSpec((1, d_attn, tm), lambda b, s: (b, 0, s))
    out_shape[3] = jax.ShapeDtypeStruct((bsz, d_attn, seq), BF16)
    if emit_f32:
        out_specs += [pl.BlockSpec((1, tm, d_attn), tok)] * 2
        out_shape += [jax.ShapeDtypeStruct((bsz, seq, d_attn), F32)] * 2
    return pl.pallas_call(
        functools.partial(_inproj_kernel, rope=rope, emit_f32=emit_f32, d_pool=d_pool, d_attn=d_attn),
        grid=(bsz, seq // tm),
        in_specs=in_specs,
        out_specs=out_specs,
        out_shape=out_shape,
        compiler_params=_params("parallel", "parallel"),
        name="in_proj_rope" if rope else "in_proj",
    )(*args)


def _attn_kernel(*refs, n_src, tq, out_scale):
    lam_ref, sub_ref, q_ref = refs[:3]
    kv_refs = refs[3:3 + 2 * n_src]
    o_ref = refs[3 + 2 * n_src]
    lam = lam_ref[0:1, 0:1]
    tu = min(tq, QUERY_UNIT)
    lane = lax.broadcasted_iota(jnp.int32, (tu, V_DIM), 1)
    chunks = []
    for i in range(n_src):
        n_keys = kv_refs[2 * i].shape[1]
        step = min(n_keys, KEY_CHUNK)
        chunks += [(i, r, r + step) for r in range(0, n_keys, step)]
    units = [(h, slice(j * tu, (j + 1) * tu)) for h in range(N_HEADS) for j in range(tq // tu)]

    def head_cols(h):
        return slice(h * V_DIM, (h + 1) * V_DIM)

    def stacked_queries(unit):
        h, rows = unit
        q = q_ref[0, rows, head_cols(h)].astype(F32)
        return jnp.concatenate([jnp.where(lane < HEAD_DIM, q, 0.0),
                                jnp.where(lane >= HEAD_DIM, q, 0.0)], axis=0).astype(BF16)

    def score_chunk(unit, qq, chunk):
        i, r0, r1 = chunk
        k = kv_refs[2 * i][0, r0:r1, head_cols(unit[0])].astype(BF16)
        return lax.dot_general(k, qq, (((1,), (1,)), ((), ())), preferred_element_type=F32)

    def value_chunk(unit, e, chunk):
        i, r0, r1 = chunk
        v_t = kv_refs[2 * i + 1][0, head_cols(unit[0]), r0:r1].astype(BF16)
        ones_row = lax.broadcasted_iota(jnp.int32, (BF16_SUBLANES, r1 - r0), 0) == 0
        v_ext = jnp.concatenate([v_t, jnp.where(ones_row, 1.0, 0.0).astype(BF16)], axis=0)
        return _dot(v_ext, e)

    def column_max(scores):
        m = jnp.max(scores[0], axis=0, keepdims=True)
        for s in scores[1:]:
            m = jnp.maximum(m, jnp.max(s, axis=0, keepdims=True))
        return m

    def output_phase(unit, pv):
        h, rows = unit
        r = 1.0 / pv[V_DIM:V_DIM + 1, :]
        o_t = pv[:V_DIM, :tu] * r[:, :tu] - pv[:V_DIM, tu:] * (r[:, tu:] * lam)
        ms = jnp.mean(o_t * o_t, axis=0, keepdims=True)
        o = (o_t * lax.rsqrt(ms + EPS)).T
        o_ref[0, rows, head_cols(h)] = (o * sub_ref[...] * out_scale).astype(BF16)

    n_units = len(units)
    scores, exps = {}, {}
    for t in range(n_units + 2):
        u_exp, u_val = t - 1, t - 2
        do_score, do_exp, do_val = t < n_units, 0 <= u_exp < n_units, 0 <= u_val < n_units
        if do_score:
            qq = stacked_queries(units[t])
            scores[t] = []
        if do_exp:
            m = column_max(scores[u_exp])
            exps[u_exp] = []
        pv = None
        for ci, c in enumerate(chunks):
            if do_score:
                scores[t].append(score_chunk(units[t], qq, c))
            if do_exp:
                exps[u_exp].append(jnp.exp(scores[u_exp][ci] - m).astype(BF16))
            if do_val:
                part = value_chunk(units[u_val], exps[u_val][ci], c)
                pv = part if pv is None else pv + part
        if do_exp:
            del scores[u_exp]
        if do_val:
            del exps[u_val]
            output_phase(units[u_val], pv)


def _attn_call(lam, subln, q, kvs, *, tq, out_scale):
    bsz, seq, d_attn = q.shape
    assert d_attn == N_HEADS * V_DIM
    in_specs = [
        _const_spec(lam.shape),
        _const_spec(subln.shape),
        pl.BlockSpec((1, tq, d_attn), lambda b, i: (b, i, 0)),
    ]
    args = [lam, subln, q]
    for kv in kvs:
        in_specs.append(pl.BlockSpec((1,) + kv.shape[1:], lambda b, i: (b, 0, 0)))
        args.append(kv)
    return pl.pallas_call(
        functools.partial(_attn_kernel, n_src=len(kvs) // 2, tq=tq, out_scale=out_scale),
        grid=(bsz, seq // tq),
        in_specs=in_specs,
        out_specs=pl.BlockSpec((1, tq, d_attn), lambda b, i: (b, i, 0)),
        out_shape=jax.ShapeDtypeStruct((bsz, seq, d_attn), BF16),
        compiler_params=_params("parallel", "parallel"),
        name="attn%d" % (len(kvs) // 2),
    )(*args)


def _fill_with_halo(dst_ref, prev_ref, mid, next_ref, tm):
    s = pl.program_id(1)
    last = pl.num_programs(1) - 1
    prev = prev_ref[0].astype(dst_ref.dtype)
    nxt = next_ref[0].astype(dst_ref.dtype)
    dst_ref[0:HALO] = jnp.where(s > 0, prev, jnp.zeros_like(prev))
    dst_ref[HALO:HALO + tm] = mid
    dst_ref[HALO + tm:] = jnp.where(s < last, nxt, jnp.zeros_like(nxt))


def _mix_kernel(x_ref, u_ref, uprev_ref, unext_ref, o_ref, gt_ref, gate1_ref, shift2_ref, scale2_ref,
                n2_ref, wpool_ref, pscale_ref, wpa_ref, wpb_ref, wout_ref,
                x1_ref, h2_ref, pad_ref, *, tm, seq):
    d = x_ref.shape[-1]
    uf = u_ref[0].astype(F32)
    _fill_with_halo(pad_ref, uprev_ref, uf, unext_ref, tm)
    t = pl.program_id(1) * tm + lax.broadcasted_iota(jnp.int32, (tm, 1), 0)

    n_groups = len(POOL_WINDOWS)
    cols_per_group = d // n_groups
    mixed, y_b = [], []
    for gi, w in enumerate(POOL_WINDOWS):
        cols = slice(gi * LANES, (gi + 1) * LANES)
        y_b.append(_dot(o_ref[0], wpb_ref[:, gi * cols_per_group:(gi + 1) * cols_per_group]))
        acc = None
        for off in range(-(w // 2), w - w // 2):
            piece = pad_ref[HALO + off:HALO + off + tm, cols]
            acc = piece if acc is None else acc + piece
        lo = jnp.maximum(t - w // 2, 0)
        hi = jnp.minimum(t + (w - w // 2), seq)
        cnt = (hi - lo).astype(F32)
        pooled = acc / cnt - uf[:, cols]
        mixed.append(_dot(pooled.astype(BF16), wpool_ref[gi]) * pscale_ref[:, cols])
    mixed = jnp.concatenate(mixed, axis=1).astype(BF16)
    y_b = jnp.concatenate(y_b, axis=1)

    y_a = _dot(mixed, wpa_ref[...])
    g = jax.nn.sigmoid(gt_ref[0].astype(F32))
    merged = (g[:, :d] * y_a + g[:, d:] * y_b).astype(BF16)
    x1 = x_ref[0] + gate1_ref[0] * _dot(merged, wout_ref[...])
    x1_ref[0] = x1
    h2_ref[0] = _rms_modulate(x1, n2_ref[...], shift2_ref[0], scale2_ref[0]).astype(BF16)


def _halo_specs(width, tm, seq):
    blocks_per_tile = tm // HALO
    last_block = seq // HALO - 1
    prev = pl.BlockSpec((1, HALO, width), lambda b, s: (b, jnp.maximum(s * blocks_per_tile - 1, 0), 0))
    nxt = pl.BlockSpec((1, HALO, width), lambda b, s: (b, jnp.minimum((s + 1) * blocks_per_tile, last_block), 0))
    return prev, nxt


def _mix_call(x, u, o, gates, gate1, shift2, scale2, n2, w_pool, pscale, wpa, wpb, wout, *, tm):
    bsz, seq, d = x.shape
    d_pool = u.shape[-1]
    tok = lambda b, s: (b, s, 0)
    uprev, unext = _halo_specs(d_pool, tm, seq)
    in_specs = [
        pl.BlockSpec((1, tm, d), tok),
        pl.BlockSpec((1, tm, d_pool), tok), uprev, unext,
        pl.BlockSpec((1, tm, o.shape[-1]), tok),
        pl.BlockSpec((1, tm, gates.shape[-1]), tok),
        pl.BlockSpec((1, 1, d), _row_spec(gate1.shape[0])),
        pl.BlockSpec((1, 1, d), _row_spec(shift2.shape[0])),
        pl.BlockSpec((1, 1, d), _row_spec(scale2.shape[0])),
        _const_spec(n2.shape), _const_spec(w_pool.shape), _const_spec(pscale.shape),
        _const_spec(wpa.shape), _const_spec(wpb.shape), _const_spec(wout.shape),
    ]
    return pl.pallas_call(
        functools.partial(_mix_kernel, tm=tm, seq=seq),
        grid=(bsz, seq // tm),
        in_specs=in_specs,
        out_specs=[pl.BlockSpec((1, tm, d), tok), pl.BlockSpec((1, tm, d), tok)],
        out_shape=[jax.ShapeDtypeStruct((bsz, seq, d), F32), jax.ShapeDtypeStruct((bsz, seq, d), BF16)],
        scratch_shapes=[pltpu.VMEM((tm + 2 * HALO, d_pool), F32)],
        compiler_params=_params("parallel", "parallel"),
        name="mix",
    )(x, u, u, u, o, gates, gate1, shift2, scale2, n2, w_pool, pscale, wpa, wpb, wout)


def _ffn_kernel(x1_ref, h2_ref, hprev_ref, hnext_ref, win_ref, cw_ref, cb_ref, wout_ref, gate2_ref, fnw_ref,
                y_ref, hext_ref, act_ref, *, tm, d_ff):
    _fill_with_halo(hext_ref, hprev_ref, h2_ref[0], hnext_ref, tm)
    he = hext_ref[...]
    h2 = h2_ref[0]
    for c in range(d_ff // FFN_CHUNK):
        cols = slice(c * FFN_CHUNK, (c + 1) * FFN_CHUNK)
        a_ext = _dot(he, win_ref[:, cols])
        up = _dot(h2, win_ref[:, d_ff + c * FFN_CHUNK:d_ff + (c + 1) * FFN_CHUNK])
        rows = a_ext.shape[0]
        conv = (pltpu.roll(a_ext, 1, 0)[HALO:HALO + tm] * cw_ref[0:1, cols]
                + a_ext[HALO:HALO + tm] * cw_ref[1:2, cols]
                + pltpu.roll(a_ext, rows - 1, 0)[HALO:HALO + tm] * cw_ref[2:3, cols]
                + cb_ref[:, cols])
        act_ref[:, cols] = (conv * jax.nn.sigmoid(conv) * up).astype(BF16)
    x2 = x1_ref[0] + gate2_ref[0] * _dot(act_ref[...], wout_ref[...])
    ms = jnp.mean(x2 * x2, axis=-1, keepdims=True)
    y_ref[0] = x2 * lax.rsqrt(ms + EPS) * fnw_ref[...]


def _ffn_call(x1, h2, w_ffn_in, conv_w, conv_b, w_ffn_out, gate2, fnw, *, tm):
    bsz, seq, d = x1.shape
    d_ff = w_ffn_out.shape[0]
    tok = lambda b, s: (b, s, 0)
    hprev, hnext = _halo_specs(d, tm, seq)
    in_specs = [
        pl.BlockSpec((1, tm, d), tok),
        pl.BlockSpec((1, tm, d), tok), hprev, hnext,
        _const_spec(w_ffn_in.shape), _const_spec(conv_w.shape), _const_spec(conv_b.shape),
        _const_spec(w_ffn_out.shape),
        pl.BlockSpec((1, 1, d), _row_spec(gate2.shape[0])),
        _const_spec(fnw.shape),
    ]
    return pl.pallas_call(
        functools.partial(_ffn_kernel, tm=tm, d_ff=d_ff),
        grid=(bsz, seq // tm),
        in_specs=in_specs,
        out_specs=pl.BlockSpec((1, tm, d), tok),
        out_shape=jax.ShapeDtypeStruct((bsz, seq, d), F32),
        scratch_shapes=[pltpu.VMEM((tm + 2 * HALO, d), BF16), pltpu.VMEM((tm, d_ff), BF16)],
        compiler_params=_params("parallel", "parallel"),
        name="ffn",
    )(x1, h2, h2, h2, w_ffn_in, conv_w, conv_b, w_ffn_out, gate2, fnw)


def _rope_tables(n_tok):
    rows = n_tok // GRID_W
    row = jnp.repeat(jnp.arange(rows, dtype=F32), GRID_W)
    col = jnp.tile(jnp.arange(GRID_W, dtype=F32), rows)
    n_freq = HEAD_DIM // 4
    inv = ROPE_BASE ** (-jnp.arange(n_freq, dtype=F32) / n_freq)
    ang_row = row[:, None] * inv
    ang_col = col[:, None] * inv
    zeros = jnp.zeros_like(ang_row)
    cos_row, sin_row = jnp.cos(ang_row), jnp.sin(ang_row)
    cos_col, sin_col = jnp.cos(ang_col), jnp.sin(ang_col)
    cos = jnp.concatenate([cos_row, cos_row, cos_col, cos_col], axis=1)
    sa = jnp.concatenate([-sin_row, zeros, -sin_col, zeros], axis=1)
    sb = jnp.concatenate([zeros, sin_row, zeros, sin_col], axis=1)
    rep = LANES // HEAD_DIM
    return tuple(jnp.tile(t, (1, rep)) for t in (cos, sa, sb))


def _stream(x, mods, w, lam, ctx_kv, *, lam_init, final_norm_w):
    shift1, scale1, gate1, shift2, scale2, gate2 = mods
    tm = min(x.shape[1], TOKEN_TILE)
    tq = min(x.shape[1], QUERY_TILE)
    d_pool = w["w_proj_a"].shape[0]
    d_attn = w["w_proj_b"].shape[0]
    is_ctx = ctx_kv is None
    rope_tabs = None if is_ctx else _rope_tables(x.shape[1])
    outs = _inproj_call(x, shift1, scale1, w["norm1_w"], w["w_in"], rope_tabs,
                        tm=tm, emit_f32=is_ctx, d_pool=d_pool, d_attn=d_attn)
    u, q, k, v, gates = outs[:5]
    kvs = [k, v] if is_ctx else [ctx_kv[0], ctx_kv[1], k, v]
    o = _attn_call(lam, w["subln_w"], q, kvs, tq=tq, out_scale=1.0 - lam_init)
    x1, h2 = _mix_call(x, u, o, gates, gate1, shift2, scale2, w["norm2_w"], w["w_pool"], w["pool_scale"],
                       w["w_proj_a"], w["w_proj_b"], w["w_out"], tm=tm)
    y = _ffn_call(x1, h2, w["w_ffn_in"], w["ffn_conv_w"], w["ffn_conv_b"], w["w_ffn_out"], gate2,
                  final_norm_w, tm=tm)
    return y, outs[5:]


def kernel(x_prompt, x_sample, cache_k, cache_v, c, c_ctx, w_ada, b_ada, w_in, w_pool, pool_scale, w_proj_a, w_proj_b, w_out, lam_q1, lam_k1, lam_q2, lam_k2, subln_w, norm1_w, norm2_w, w_ffn_in, ffn_conv_w, ffn_conv_b, w_ffn_out, final_norm_w):
    assert w_ada.shape[0] == 1, "single trunk layer"
    bsz, seq, d = x_prompt.shape
    dec_b, dec_seq, _ = x_sample.shape
    assert 1 + dec_b <= MOD_ROWS
    lam_init = 0.8 - 0.6 * math.exp(-0.3 * 0)

    cc = jnp.zeros((MOD_ROWS, d), F32).at[0].set(c_ctx).at[1:1 + dec_b].set(c)
    lamv = jnp.concatenate([lam_q1, lam_k1, lam_q2, lam_k2], axis=0)
    mod, lam = _mod_call(cc, w_ada[0], b_ada, lamv, lam_init)
    mod = mod.reshape(MOD_ROWS, 6, 1, d)
    mods_ctx = [mod[0:1, j] for j in range(6)]
    mods_lat = [mod[1:1 + dec_b, j] for j in range(6)]

    w = dict(
        w_in=w_in[0].astype(BF16), w_pool=w_pool[0].astype(BF16), pool_scale=pool_scale,
        w_proj_a=w_proj_a[0].astype(BF16), w_proj_b=w_proj_b[0].astype(BF16), w_out=w_out[0].astype(BF16),
        subln_w=subln_w, norm1_w=norm1_w, norm2_w=norm2_w,
        w_ffn_in=w_ffn_in[0].astype(BF16), ffn_conv_w=ffn_conv_w[0], ffn_conv_b=ffn_conv_b,
        w_ffn_out=w_ffn_out[0].astype(BF16),
    )
    fnw = final_norm_w.reshape(1, d)

    y_prompt, (k32, v32) = _stream(x_prompt, mods_ctx, w, lam, None, lam_init=lam_init, final_norm_w=fnw)
    d_attn = k32.shape[-1]
    ctx_kv = (cache_k[:, 0].reshape(dec_b, -1, d_attn),
              jnp.swapaxes(cache_v[:, 0].reshape(dec_b, -1, d_attn), 1, 2))
    y_sample, _ = _stream(x_sample, mods_lat, w, lam, ctx_kv, lam_init=lam_init, final_norm_w=fnw)

    new_cache_k = k32.reshape(bsz, 1, seq, N_HEADS, 2, HEAD_DIM)
    new_cache_v = v32.reshape(bsz, 1, seq, N_HEADS, V_DIM)
    return (y_prompt, y_sample, new_cache_k, new_cache_v)
```

```python
import functools
import math

import jax
import jax.numpy as jnp
from jax import lax
from jax.experimental import pallas as pl
from jax.experimental.pallas import tpu as pltpu

F32 = jnp.float32
BF16 = jnp.bfloat16

GRID_W = 64
N_HEADS = 8
HEAD_DIM = 64
V_DIM = 2 * HEAD_DIM
POOL_WINDOWS = (2, 4, 8, 16)
ROPE_BASE = 10000.0
EPS = 1e-6

LANES = 128
BF16_SUBLANES = 16
VMEM_LIMIT_BYTES = 56 * 1024 * 1024

MOD_ROWS = 16
MOD_TN = 512
HALO = BF16_SUBLANES
FFN_CHUNK = 256
TOKEN_TILE = 512
QUERY_TILE = 512
QUERY_UNIT = 256
KEY_CHUNK = 256


def _dot(a, b):
    return jnp.dot(a, b, preferred_element_type=F32)


def _const_spec(shape):
    zeros = (0,) * len(shape)
    return pl.BlockSpec(shape, lambda *_: zeros, pipeline_mode=pl.Buffered(1))


def _params(*sem):
    return pltpu.CompilerParams(dimension_semantics=sem, vmem_limit_bytes=VMEM_LIMIT_BYTES)


def _mod_kernel(cc_ref, w_ref, b_ref, lamv_ref, mod_ref, lam_ref, *, lam_init):
    c = cc_ref[...]
    sc = (c * jax.nn.sigmoid(c)).astype(BF16)
    mod_ref[...] = _dot(sc, w_ref[...].astype(BF16)) + b_ref[...]
    lv = lamv_ref[...]
    p1 = jnp.sum(lv[0:1] * lv[1:2], axis=-1, keepdims=True)
    p2 = jnp.sum(lv[2:3] * lv[3:4], axis=-1, keepdims=True)
    lam = jnp.exp(p1) - jnp.exp(p2) + lam_init
    lam_ref[...] = jnp.broadcast_to(lam, lam_ref.shape)


def _mod_call(cc, w_ada, b_ada, lamv, lam_init):
    d, n = w_ada.shape
    return pl.pallas_call(
        functools.partial(_mod_kernel, lam_init=lam_init),
        grid=(n // MOD_TN,),
        in_specs=[
            pl.BlockSpec((MOD_ROWS, d), lambda j: (0, 0)),
            pl.BlockSpec((d, MOD_TN), lambda j: (0, j)),
            pl.BlockSpec((1, MOD_TN), lambda j: (0, j)),
            pl.BlockSpec(lamv.shape, lambda j: (0, 0)),
        ],
        out_specs=[
            pl.BlockSpec((MOD_ROWS, MOD_TN), lambda j: (0, j)),
            pl.BlockSpec((8, LANES), lambda j: (0, 0)),
        ],
        out_shape=[
            jax.ShapeDtypeStruct((MOD_ROWS, n), F32),
            jax.ShapeDtypeStruct((8, LANES), F32),
        ],
        compiler_params=_params("arbitrary"),
        name="mod",
    )(cc, w_ada, b_ada, lamv)


def _rms_modulate(x, g, shift, scale):
    ms = jnp.mean(x * x, axis=-1, keepdims=True)
    return x * lax.rsqrt(ms + EPS) * g * (1.0 + scale) + shift


def _inproj_kernel(*refs, rope, emit_f32, d_pool, d_attn):
    x_ref, shift_ref, scale_ref, g_ref, w_ref = refs[:5]
    pos = 5
    if rope:
        cos_ref, sa_ref, sb_ref = refs[pos:pos + 3]
        pos += 3
    u_ref, q_ref, k_ref, v_ref, gt_ref = refs[pos:pos + 5]
    pos += 5
    if emit_f32:
        k32_ref, v32_ref = refs[pos:pos + 2]

    nb, ts, d = x_ref.shape
    rows = nb * ts
    x = x_ref[...].reshape(rows, d)
    hb = _rms_modulate(x, g_ref[...], shift_ref[0], scale_ref[0]).astype(BF16)

    def put(ref, val):
        ref[...] = val.reshape(nb, ts, val.shape[-1])

    def rotate(t):
        if not rope:
            return t
        cos, sa, sb = cos_ref[...], sa_ref[...], sb_ref[...]
        outs = []
        for c in range(t.shape[1] // LANES):
            tc = t[:, c * LANES:(c + 1) * LANES]
            outs.append(tc * cos + pltpu.roll(tc, LANES - HEAD_DIM // 4, 1) * sa
                        + pltpu.roll(tc, HEAD_DIM // 4, 1) * sb)
        return jnp.concatenate(outs, axis=1)

    o0 = 0
    put(u_ref, _dot(hb, w_ref[:, o0:o0 + d_pool]).astype(BF16))
    o0 += d_pool
    q = _dot(hb, w_ref[:, o0:o0 + d_attn])
    put(q_ref, (rotate(q) * (HEAD_DIM ** -0.5)).astype(BF16))
    o0 += d_attn
    k = _dot(hb, w_ref[:, o0:o0 + d_attn])
    if emit_f32:
        put(k32_ref, k)
    put(k_ref, rotate(k).astype(BF16))
    o0 += d_attn
    v = _dot(hb, w_ref[:, o0:o0 + d_attn])
    if emit_f32:
        put(v32_ref, v)
    v_t = v.T.astype(BF16)
    for b in range(nb):
        v_ref[b] = v_t[:, b * ts:(b + 1) * ts]
    o0 += d_attn
    put(gt_ref, _dot(hb, w_ref[:, o0:]).astype(BF16))


def _row_spec(n_rows):
    if n_rows == 1:
        return lambda b, s: (0, 0, 0)
    return lambda b, s: (b, 0, 0)


def _inproj_call(x, shift, scale, g, w_in, rope_tabs, *, nb, ts, emit_f32, d_pool, d_attn):
    bsz, seq, d = x.shape
    d_in = w_in.shape[1]
    d_gate = d_in - d_pool - 3 * d_attn
    rope = rope_tabs is not None
    assert nb == 1 or (shift.shape[0] == 1 and not rope)
    tok = lambda b, s: (b, s, 0)
    in_specs = [
        pl.BlockSpec((nb, ts, d), tok),
        pl.BlockSpec((1, 1, d), _row_spec(shift.shape[0])),
        pl.BlockSpec((1, 1, d), _row_spec(scale.shape[0])),
        _const_spec((1, d)),
        _const_spec(w_in.shape),
    ]
    args = [x, shift, scale, g, w_in]
    if rope:
        in_specs += [pl.BlockSpec((ts, LANES), lambda b, s: (s, 0))] * 3
        args += list(rope_tabs)
    widths = [d_pool, d_attn, d_attn, d_attn, d_gate]
    out_specs = [pl.BlockSpec((nb, ts, w), tok) for w in widths]
    out_shape = [jax.ShapeDtypeStruct((bsz, seq, w), BF16) for w in widths]
    out_specs[3] = pl.BlockSpec((nb, d_attn, ts), lambda b, s: (b, 0, s))
    out_shape[3] = jax.ShapeDtypeStruct((bsz, d_attn, seq), BF16)
    if emit_f32:
        out_specs += [pl.BlockSpec((nb, ts, d_attn), tok)] * 2
        out_shape += [jax.ShapeDtypeStruct((bsz, seq, d_attn), F32)] * 2
    return pl.pallas_call(
        functools.partial(_inproj_kernel, rope=rope, emit_f32=emit_f32, d_pool=d_pool, d_attn=d_attn),
        grid=(bsz // nb, seq // ts),
        in_specs=in_specs,
        out_specs=out_specs,
        out_shape=out_shape,
        compiler_params=_params("parallel", "parallel"),
        name="in_proj_rope" if rope else "in_proj",
    )(*args)


def _attn_kernel(*refs, n_src, tq, out_scale):
    lam_ref, sub_ref, q_ref = refs[:3]
    kv_refs = refs[3:3 + 2 * n_src]
    o_ref = refs[3 + 2 * n_src]
    lam = lam_ref[0:1, 0:1]
    tu = min(tq, QUERY_UNIT)
    lane = lax.broadcasted_iota(jnp.int32, (tu, V_DIM), 1)
    chunks = []
    for i in range(n_src):
        n_keys = kv_refs[2 * i].shape[1]
        step = min(n_keys, KEY_CHUNK)
        chunks += [(i, r, r + step) for r in range(0, n_keys, step)]
    units = [(b, h, slice(j * tu, (j + 1) * tu))
             for b in range(q_ref.shape[0]) for h in range(N_HEADS) for j in range(tq // tu)]

    def head_cols(h):
        return slice(h * V_DIM, (h + 1) * V_DIM)

    def stacked_queries(unit):
        b, h, rows = unit
        q = q_ref[b, rows, head_cols(h)].astype(F32)
        return jnp.concatenate([jnp.where(lane < HEAD_DIM, q, 0.0),
                                jnp.where(lane >= HEAD_DIM, q, 0.0)], axis=0).astype(BF16)

    def score_chunk(unit, qq, chunk):
        i, r0, r1 = chunk
        k = kv_refs[2 * i][unit[0], r0:r1, head_cols(unit[1])].astype(BF16)
        return lax.dot_general(k, qq, (((1,), (1,)), ((), ())), preferred_element_type=F32)

    def value_chunk(unit, e, chunk):
        i, r0, r1 = chunk
        v_t = kv_refs[2 * i + 1][unit[0], head_cols(unit[1]), r0:r1].astype(BF16)
        ones_row = lax.broadcasted_iota(jnp.int32, (BF16_SUBLANES, r1 - r0), 0) == 0
        v_ext = jnp.concatenate([v_t, jnp.where(ones_row, 1.0, 0.0).astype(BF16)], axis=0)
        return _dot(v_ext, e)

    def column_max(scores):
        m = jnp.max(scores[0], axis=0, keepdims=True)
        for s in scores[1:]:
            m = jnp.maximum(m, jnp.max(s, axis=0, keepdims=True))
        return m

    def output_phase(unit, pv):
        b, h, rows = unit
        r = 1.0 / pv[V_DIM:V_DIM + 1, :]
        o_t = pv[:V_DIM, :tu] * r[:, :tu] - pv[:V_DIM, tu:] * (r[:, tu:] * lam)
        ms = jnp.mean(o_t * o_t, axis=0, keepdims=True)
        o = (o_t * lax.rsqrt(ms + EPS)).T
        o_ref[b, rows, head_cols(h)] = (o * sub_ref[...] * out_scale).astype(BF16)

    n_units = len(units)
    scores, exps = {}, {}
    for t in range(n_units + 2):
        u_exp, u_val = t - 1, t - 2
        do_score, do_exp, do_val = t < n_units, 0 <= u_exp < n_units, 0 <= u_val < n_units
        if do_score:
            qq = stacked_queries(units[t])
            scores[t] = []
        if do_exp:
            m = column_max(scores[u_exp])
            exps[u_exp] = []
        pv = None
        for ci, c in enumerate(chunks):
            if do_val:
                part = value_chunk(units[u_val], exps[u_val][ci], c)
                pv = part if pv is None else pv + part
            if do_score:
                scores[t].append(score_chunk(units[t], qq, c))
            if do_exp:
                exps[u_exp].append(jnp.exp(scores[u_exp][ci] - m).astype(BF16))
        if do_exp:
            del scores[u_exp]
        if do_val:
            del exps[u_val]
            output_phase(units[u_val], pv)


def _attn_call(lam, subln, q, kvs, *, nb, tq, out_scale):
    bsz, seq, d_attn = q.shape
    assert d_attn == N_HEADS * V_DIM
    in_specs = [
        _const_spec(lam.shape),
        _const_spec(subln.shape),
        pl.BlockSpec((nb, tq, d_attn), lambda b, i: (b, i, 0)),
    ]
    args = [lam, subln, q]
    for kv in kvs:
        in_specs.append(pl.BlockSpec((nb,) + kv.shape[1:], lambda b, i: (b, 0, 0)))
        args.append(kv)
    return pl.pallas_call(
        functools.partial(_attn_kernel, n_src=len(kvs) // 2, tq=tq, out_scale=out_scale),
        grid=(bsz // nb, seq // tq),
        in_specs=in_specs,
        out_specs=pl.BlockSpec((nb, tq, d_attn), lambda b, i: (b, i, 0)),
        out_shape=jax.ShapeDtypeStruct((bsz, seq, d_attn), BF16),
        compiler_params=_params("parallel", "parallel"),
        name="attn%d" % (len(kvs) // 2),
    )(*args)


def _fill_with_halo(dst_ref, prev_ref, mids, next_ref, tm):
    s = pl.program_id(1)
    last = pl.num_programs(1) - 1
    prev = prev_ref[0].astype(dst_ref.dtype)
    nxt = next_ref[0].astype(dst_ref.dtype)
    for b, mid in enumerate(mids):
        base = b * (tm + 2 * HALO)
        dst_ref[base:base + HALO] = jnp.where(s > 0, prev, jnp.zeros_like(prev))
        dst_ref[base + HALO:base + HALO + tm] = mid
        dst_ref[base + HALO + tm:base + 2 * HALO + tm] = jnp.where(s < last, nxt, jnp.zeros_like(nxt))


def _mix_kernel(x_ref, u_ref, uprev_ref, unext_ref, o_ref, gt_ref, gate1_ref, shift2_ref, scale2_ref,
                n2_ref, wpool_ref, pscale_ref, wpa_ref, wpb_ref, wout_ref,
                x1_ref, h2_ref, pad_ref, *, tm, seq):
    d = x_ref.shape[-1]
    uf = u_ref[0].astype(F32)
    _fill_with_halo(pad_ref, uprev_ref, [uf], unext_ref, tm)
    t = pl.program_id(1) * tm + lax.broadcasted_iota(jnp.int32, (tm, 1), 0)

    n_groups = len(POOL_WINDOWS)
    cols_per_group = d // n_groups
    mixed, y_b = [], []
    for gi, w in enumerate(POOL_WINDOWS):
        cols = slice(gi * LANES, (gi + 1) * LANES)
        y_b.append(_dot(o_ref[0], wpb_ref[:, gi * cols_per_group:(gi + 1) * cols_per_group]))
        acc = None
        for off in range(-(w // 2), w - w // 2):
            piece = pad_ref[HALO + off:HALO + off + tm, cols]
            acc = piece if acc is None else acc + piece
        lo = jnp.maximum(t - w // 2, 0)
        hi = jnp.minimum(t + (w - w // 2), seq)
        cnt = (hi - lo).astype(F32)
        pooled = acc / cnt - uf[:, cols]
        mixed.append(_dot(pooled.astype(BF16), wpool_ref[gi]) * pscale_ref[:, cols])
    mixed = jnp.concatenate(mixed, axis=1).astype(BF16)
    y_b = jnp.concatenate(y_b, axis=1)

    y_a = _dot(mixed, wpa_ref[...])
    g = jax.nn.sigmoid(gt_ref[0].astype(F32))
    merged = (g[:, :d] * y_a + g[:, d:] * y_b).astype(BF16)
    x1 = x_ref[0] + gate1_ref[0] * _dot(merged, wout_ref[...])
    x1_ref[0] = x1
    h2_ref[0] = _rms_modulate(x1, n2_ref[...], shift2_ref[0], scale2_ref[0]).astype(BF16)


def _halo_specs(width, tm, seq, nb=1):
    blocks_per_tile = tm // HALO
    last_block = seq // HALO - 1
    prev = pl.BlockSpec((1, HALO, width), lambda b, s: (b * nb, jnp.maximum(s * blocks_per_tile - 1, 0), 0))
    nxt = pl.BlockSpec((1, HALO, width),
                       lambda b, s: (b * nb, jnp.minimum((s + 1) * blocks_per_tile, last_block), 0))
    return prev, nxt


def _mix_call(x, u, o, gates, gate1, shift2, scale2, n2, w_pool, pscale, wpa, wpb, wout, *, tm):
    bsz, seq, d = x.shape
    d_pool = u.shape[-1]
    tok = lambda b, s: (b, s, 0)
    uprev, unext = _halo_specs(d_pool, tm, seq)
    in_specs = [
        pl.BlockSpec((1, tm, d), tok),
        pl.BlockSpec((1, tm, d_pool), tok), uprev, unext,
        pl.BlockSpec((1, tm, o.shape[-1]), tok),
        pl.BlockSpec((1, tm, gates.shape[-1]), tok),
        pl.BlockSpec((1, 1, d), _row_spec(gate1.shape[0])),
        pl.BlockSpec((1, 1, d), _row_spec(shift2.shape[0])),
        pl.BlockSpec((1, 1, d), _row_spec(scale2.shape[0])),
        _const_spec(n2.shape), _const_spec(w_pool.shape), _const_spec(pscale.shape),
        _const_spec(wpa.shape), _const_spec(wpb.shape), _const_spec(wout.shape),
    ]
    return pl.pallas_call(
        functools.partial(_mix_kernel, tm=tm, seq=seq),
        grid=(bsz, seq // tm),
        in_specs=in_specs,
        out_specs=[pl.BlockSpec((1, tm, d), tok), pl.BlockSpec((1, tm, d), tok)],
        out_shape=[jax.ShapeDtypeStruct((bsz, seq, d), F32), jax.ShapeDtypeStruct((bsz, seq, d), BF16)],
        scratch_shapes=[pltpu.VMEM((tm + 2 * HALO, d_pool), F32)],
        compiler_params=_params("parallel", "parallel"),
        name="mix",
    )(x, u, u, u, o, gates, gate1, shift2, scale2, n2, w_pool, pscale, wpa, wpb, wout)


def _ffn_kernel(x1_ref, h2_ref, hprev_ref, hnext_ref, win_ref, cw_ref, cb_ref, wout_ref, gate2_ref, fnw_ref,
                y_ref, hext_ref, act_ref, *, tm, d_ff):
    nb, _, d = x1_ref.shape
    _fill_with_halo(hext_ref, hprev_ref, [h2_ref[b] for b in range(nb)], hnext_ref, tm)
    he = hext_ref[...]
    h2 = h2_ref[...].reshape(nb * tm, d)

    def tile_rows(t):
        parts = [t[b * (tm + 2 * HALO) + HALO:b * (tm + 2 * HALO) + HALO + tm] for b in range(nb)]
        return parts[0] if nb == 1 else jnp.concatenate(parts, axis=0)

    for c in range(d_ff // FFN_CHUNK):
        cols = slice(c * FFN_CHUNK, (c + 1) * FFN_CHUNK)
        a_ext = _dot(he, win_ref[:, cols])
        up = _dot(h2, win_ref[:, d_ff + c * FFN_CHUNK:d_ff + (c + 1) * FFN_CHUNK])
        rows = a_ext.shape[0]
        conv = (tile_rows(pltpu.roll(a_ext, 1, 0)) * cw_ref[0:1, cols]
                + tile_rows(a_ext) * cw_ref[1:2, cols]
                + tile_rows(pltpu.roll(a_ext, rows - 1, 0)) * cw_ref[2:3, cols]
                + cb_ref[:, cols])
        act_ref[:, cols] = (conv * jax.nn.sigmoid(conv) * up).astype(BF16)
    x2 = x1_ref[...].reshape(nb * tm, d) + gate2_ref[0] * _dot(act_ref[...], wout_ref[...])
    ms = jnp.mean(x2 * x2, axis=-1, keepdims=True)
    y_ref[...] = (x2 * lax.rsqrt(ms + EPS) * fnw_ref[...]).reshape(nb, tm, d)


def _ffn_call(x1, h2, w_ffn_in, conv_w, conv_b, w_ffn_out, gate2, fnw, *, nb, tm):
    bsz, seq, d = x1.shape
    d_ff = w_ffn_out.shape[0]
    assert nb == 1 or (tm == seq and gate2.shape[0] == 1)
    tok = lambda b, s: (b, s, 0)
    hprev, hnext = _halo_specs(d, tm, seq, nb)
    in_specs = [
        pl.BlockSpec((nb, tm, d), tok),
        pl.BlockSpec((nb, tm, d), tok), hprev, hnext,
        _const_spec(w_ffn_in.shape), _const_spec(conv_w.shape), _const_spec(conv_b.shape),
        _const_spec(w_ffn_out.shape),
        pl.BlockSpec((1, 1, d), _row_spec(gate2.shape[0])),
        _const_spec(fnw.shape),
    ]
    return pl.pallas_call(
        functools.partial(_ffn_kernel, tm=tm, d_ff=d_ff),
        grid=(bsz // nb, seq // tm),
        in_specs=in_specs,
        out_specs=pl.BlockSpec((nb, tm, d), tok),
        out_shape=jax.ShapeDtypeStruct((bsz, seq, d), F32),
        scratch_shapes=[pltpu.VMEM((nb * (tm + 2 * HALO), d), BF16), pltpu.VMEM((nb * tm, d_ff), BF16)],
        compiler_params=_params("parallel", "parallel"),
        name="ffn",
    )(x1, h2, h2, h2, w_ffn_in, conv_w, conv_b, w_ffn_out, gate2, fnw)


def _rope_tables(n_tok):
    rows = n_tok // GRID_W
    row = jnp.repeat(jnp.arange(rows, dtype=F32), GRID_W)
    col = jnp.tile(jnp.arange(GRID_W, dtype=F32), rows)
    n_freq = HEAD_DIM // 4
    inv = ROPE_BASE ** (-jnp.arange(n_freq, dtype=F32) / n_freq)
    ang_row = row[:, None] * inv
    ang_col = col[:, None] * inv
    zeros = jnp.zeros_like(ang_row)
    cos_row, sin_row = jnp.cos(ang_row), jnp.sin(ang_row)
    cos_col, sin_col = jnp.cos(ang_col), jnp.sin(ang_col)
    cos = jnp.concatenate([cos_row, cos_row, cos_col, cos_col], axis=1)
    sa = jnp.concatenate([-sin_row, zeros, -sin_col, zeros], axis=1)
    sb = jnp.concatenate([zeros, sin_row, zeros, sin_col], axis=1)
    rep = LANES // HEAD_DIM
    return tuple(jnp.tile(t, (1, rep)) for t in (cos, sa, sb))


def _stream(x, mods, w, lam, ctx_kv, *, lam_init, final_norm_w):
    shift1, scale1, gate1, shift2, scale2, gate2 = mods
    seq = x.shape[1]
    tm = min(seq, TOKEN_TILE)
    tq = min(seq, QUERY_TILE)
    is_ctx = ctx_kv is None
    nb = TOKEN_TILE // tm if (is_ctx and tm == seq) else 1
    nbq = QUERY_TILE // tq if tq == seq else 1
    d_pool = w["w_proj_a"].shape[0]
    d_attn = w["w_proj_b"].shape[0]
    rope_tabs = None if is_ctx else _rope_tables(seq)
    outs = _inproj_call(x, shift1, scale1, w["norm1_w"], w["w_in"], rope_tabs,
                        nb=nb, ts=tm, emit_f32=is_ctx, d_pool=d_pool, d_attn=d_attn)
    u, q, k, v, gates = outs[:5]
    kvs = [k, v] if is_ctx else [ctx_kv[0], ctx_kv[1], k, v]
    o = _attn_call(lam, w["subln_w"], q, kvs, nb=nbq, tq=tq, out_scale=1.0 - lam_init)
    x1, h2 = _mix_call(x, u, o, gates, gate1, shift2, scale2, w["norm2_w"], w["w_pool"], w["pool_scale"],
                       w["w_proj_a"], w["w_proj_b"], w["w_out"], tm=tm)
    y = _ffn_call(x1, h2, w["w_ffn_in"], w["ffn_conv_w"], w["ffn_conv_b"], w["w_ffn_out"], gate2,
                  final_norm_w, nb=nb, tm=tm)
    return y, outs[5:]


def kernel(x_prompt, x_sample, cache_k, cache_v, c, c_ctx, w_ada, b_ada, w_in, w_pool, pool_scale, w_proj_a, w_proj_b, w_out, lam_q1, lam_k1, lam_q2, lam_k2, subln_w, norm1_w, norm2_w, w_ffn_in, ffn_conv_w, ffn_conv_b, w_ffn_out, final_norm_w):
    assert w_ada.shape[0] == 1, "single trunk layer"
    bsz, seq, d = x_prompt.shape
    dec_b, dec_seq, _ = x_sample.shape
    assert 1 + dec_b <= MOD_ROWS
    lam_init = 0.8 - 0.6 * math.exp(-0.3 * 0)

    cc = jnp.zeros((MOD_ROWS, d), F32).at[0].set(c_ctx).at[1:1 + dec_b].set(c)
    lamv = jnp.concatenate([lam_q1, lam_k1, lam_q2, lam_k2], axis=0)
    mod, lam = _mod_call(cc, w_ada[0], b_ada, lamv, lam_init)
    mod = mod.reshape(MOD_ROWS, 6, 1, d)
    mods_ctx = [mod[0:1, j] for j in range(6)]
    mods_lat = [mod[1:1 + dec_b, j] for j in range(6)]

    w = dict(
        w_in=w_in[0].astype(BF16), w_pool=w_pool[0].astype(BF16), pool_scale=pool_scale,
        w_proj_a=w_proj_a[0].astype(BF16), w_proj_b=w_proj_b[0].astype(BF16), w_out=w_out[0].astype(BF16),
        subln_w=subln_w, norm1_w=norm1_w, norm2_w=norm2_w,
        w_ffn_in=w_ffn_in[0].astype(BF16), ffn_conv_w=ffn_conv_w[0], ffn_conv_b=ffn_conv_b,
        w_ffn_out=w_ffn_out[0].astype(BF16),
    )
    fnw = final_norm_w.reshape(1, d)

    y_prompt, (k32, v32) = _stream(x_prompt, mods_ctx, w, lam, None, lam_init=lam_init, final_norm_w=fnw)
    d_attn = k32.shape[-1]
    ctx_kv = (cache_k[:, 0].reshape(dec_b, -1, d_attn),
              jnp.swapaxes(cache_v[:, 0].reshape(dec_b, -1, d_attn), 1, 2))
    y_sample, _ = _stream(x_sample, mods_lat, w, lam, ctx_kv, lam_init=lam_init, final_norm_w=fnw)

    new_cache_k = k32.reshape(bsz, 1, seq, N_HEADS, 2, HEAD_DIM)
    new_cache_v = v32.reshape(bsz, 1, seq, N_HEADS, V_DIM)
    return (y_prompt, y_sample, new_cache_k, new_cache_v)
```

```python
import functools
import math

import jax
import jax.numpy as jnp
from jax import lax
from jax.experimental import pallas as pl
from jax.experimental.pallas import tpu as pltpu

F32 = jnp.float32
BF16 = jnp.bfloat16

GRID_W = 64
N_HEADS = 8
HEAD_DIM = 64
V_DIM = 2 * HEAD_DIM
POOL_WINDOWS = (2, 4, 8, 16)
ROPE_BASE = 10000.0
EPS = 1e-6

LANES = 128
BF16_SUBLANES = 16
VMEM_LIMIT_BYTES = 56 * 1024 * 1024

MOD_ROWS = 16
MOD_TN = 512
HALO = BF16_SUBLANES
FFN_CHUNK = 256
TOKEN_TILE = 512
QUERY_TILE = 512
QUERY_UNIT = 256
KEY_CHUNK = 256
POOL_BLOCK = 256


def _dot(a, b):
    return jnp.dot(a, b, preferred_element_type=F32)


def _const_spec(shape):
    zeros = (0,) * len(shape)
    return pl.BlockSpec(shape, lambda *_: zeros, pipeline_mode=pl.Buffered(1))


def _params(*sem):
    return pltpu.CompilerParams(dimension_semantics=sem, vmem_limit_bytes=VMEM_LIMIT_BYTES)


def _mod_kernel(cc_ref, w_ref, b_ref, lamv_ref, mod_ref, lam_ref, *, lam_init):
    c = cc_ref[...]
    sc = (c * jax.nn.sigmoid(c)).astype(BF16)
    mod_ref[...] = _dot(sc, w_ref[...].astype(BF16)) + b_ref[...]
    lv = lamv_ref[...]
    p1 = jnp.sum(lv[0:1] * lv[1:2], axis=-1, keepdims=True)
    p2 = jnp.sum(lv[2:3] * lv[3:4], axis=-1, keepdims=True)
    lam = jnp.exp(p1) - jnp.exp(p2) + lam_init
    lam_ref[...] = jnp.broadcast_to(lam, lam_ref.shape)


def _mod_call(cc, w_ada, b_ada, lamv, lam_init):
    d, n = w_ada.shape
    return pl.pallas_call(
        functools.partial(_mod_kernel, lam_init=lam_init),
        grid=(n // MOD_TN,),
        in_specs=[
            pl.BlockSpec((MOD_ROWS, d), lambda j: (0, 0)),
            pl.BlockSpec((d, MOD_TN), lambda j: (0, j)),
            pl.BlockSpec((1, MOD_TN), lambda j: (0, j)),
            pl.BlockSpec(lamv.shape, lambda j: (0, 0)),
        ],
        out_specs=[
            pl.BlockSpec((MOD_ROWS, MOD_TN), lambda j: (0, j)),
            pl.BlockSpec((8, LANES), lambda j: (0, 0)),
        ],
        out_shape=[
            jax.ShapeDtypeStruct((MOD_ROWS, n), F32),
            jax.ShapeDtypeStruct((8, LANES), F32),
        ],
        compiler_params=_params("arbitrary"),
        name="mod",
    )(cc, w_ada, b_ada, lamv)


def _rms_modulate(x, g, shift, scale):
    ms = jnp.mean(x * x, axis=-1, keepdims=True)
    return x * lax.rsqrt(ms + EPS) * g * (1.0 + scale) + shift


def _inproj_kernel(*refs, rope, emit_f32, d_pool, d_attn):
    x_ref, shift_ref, scale_ref, g_ref, w_ref = refs[:5]
    pos = 5
    if rope:
        cos_ref, sa_ref, sb_ref = refs[pos:pos + 3]
        pos += 3
    u_ref, q_ref, k_ref, v_ref, gt_ref = refs[pos:pos + 5]
    pos += 5
    if emit_f32:
        k32_ref, v32_ref = refs[pos:pos + 2]

    nb, ts, d = x_ref.shape
    rows = nb * ts
    x = x_ref[...].reshape(rows, d)
    hb = _rms_modulate(x, g_ref[...], shift_ref[0], scale_ref[0]).astype(BF16)

    def put(ref, val):
        ref[...] = val.reshape(nb, ts, val.shape[-1])

    def rotate(t):
        if not rope:
            return t
        cos, sa, sb = cos_ref[...], sa_ref[...], sb_ref[...]
        outs = []
        for c in range(t.shape[1] // LANES):
            tc = t[:, c * LANES:(c + 1) * LANES]
            outs.append(tc * cos + pltpu.roll(tc, LANES - HEAD_DIM // 4, 1) * sa
                        + pltpu.roll(tc, HEAD_DIM // 4, 1) * sb)
        return jnp.concatenate(outs, axis=1)

    o0 = 0
    put(u_ref, _dot(hb, w_ref[:, o0:o0 + d_pool]).astype(BF16))
    o0 += d_pool
    q = _dot(hb, w_ref[:, o0:o0 + d_attn])
    put(q_ref, (rotate(q) * (HEAD_DIM ** -0.5)).astype(BF16))
    o0 += d_attn
    k = _dot(hb, w_ref[:, o0:o0 + d_attn])
    if emit_f32:
        put(k32_ref, k)
    put(k_ref, rotate(k).astype(BF16))
    o0 += d_attn
    v = _dot(hb, w_ref[:, o0:o0 + d_attn])
    if emit_f32:
        put(v32_ref, v)
    v_t = v.T.astype(BF16)
    for b in range(nb):
        v_ref[b] = v_t[:, b * ts:(b + 1) * ts]
    o0 += d_attn
    put(gt_ref, _dot(hb, w_ref[:, o0:]).astype(BF16))


def _row_spec(n_rows):
    if n_rows == 1:
        return lambda b, s: (0, 0, 0)
    return lambda b, s: (b, 0, 0)


def _inproj_call(x, shift, scale, g, w_in, rope_tabs, *, nb, ts, emit_f32, d_pool, d_attn):
    bsz, seq, d = x.shape
    d_in = w_in.shape[1]
    d_gate = d_in - d_pool - 3 * d_attn
    rope = rope_tabs is not None
    assert nb == 1 or (shift.shape[0] == 1 and not rope)
    tok = lambda b, s: (b, s, 0)
    in_specs = [
        pl.BlockSpec((nb, ts, d), tok),
        pl.BlockSpec((1, 1, d), _row_spec(shift.shape[0])),
        pl.BlockSpec((1, 1, d), _row_spec(scale.shape[0])),
        _const_spec((1, d)),
        _const_spec(w_in.shape),
    ]
    args = [x, shift, scale, g, w_in]
    if rope:
        in_specs += [pl.BlockSpec((ts, LANES), lambda b, s: (s, 0))] * 3
        args += list(rope_tabs)
    widths = [d_pool, d_attn, d_attn, d_attn, d_gate]
    out_specs = [pl.BlockSpec((nb, ts, w), tok) for w in widths]
    out_shape = [jax.ShapeDtypeStruct((bsz, seq, w), BF16) for w in widths]
    out_specs[3] = pl.BlockSpec((nb, d_attn, ts), lambda b, s: (b, 0, s))
    out_shape[3] = jax.ShapeDtypeStruct((bsz, d_attn, seq), BF16)
    if emit_f32:
        out_specs += [pl.BlockSpec((nb, ts, d_attn), tok)] * 2
        out_shape += [jax.ShapeDtypeStruct((bsz, seq, d_attn), F32)] * 2
    return pl.pallas_call(
        functools.partial(_inproj_kernel, rope=rope, emit_f32=emit_f32, d_pool=d_pool, d_attn=d_attn),
        grid=(bsz // nb, seq // ts),
        in_specs=in_specs,
        out_specs=out_specs,
        out_shape=out_shape,
        compiler_params=_params("parallel", "parallel"),
        name="in_proj_rope" if rope else "in_proj",
    )(*args)


def _attn_kernel(*refs, n_src, tq, out_scale):
    lam_ref, sub_ref, q_ref = refs[:3]
    kv_refs = refs[3:3 + 2 * n_src]
    o_ref = refs[3 + 2 * n_src]
    lam = lam_ref[0:1, 0:1]
    tu = min(tq, QUERY_UNIT)
    lane = lax.broadcasted_iota(jnp.int32, (tu, V_DIM), 1)
    chunks = []
    for i in range(n_src):
        n_keys = kv_refs[2 * i].shape[1]
        step = min(n_keys, KEY_CHUNK)
        chunks += [(i, r, r + step) for r in range(0, n_keys, step)]
    units = [(b, h, slice(j * tu, (j + 1) * tu))
             for b in range(q_ref.shape[0]) for h in range(N_HEADS) for j in range(tq // tu)]

    def head_cols(h):
        return slice(h * V_DIM, (h + 1) * V_DIM)

    def stacked_queries(unit):
        b, h, rows = unit
        q = q_ref[b, rows, head_cols(h)].astype(F32)
        return jnp.concatenate([jnp.where(lane < HEAD_DIM, q, 0.0),
                                jnp.where(lane >= HEAD_DIM, q, 0.0)], axis=0).astype(BF16)

    def score_chunk(unit, qq, chunk):
        i, r0, r1 = chunk
        k = kv_refs[2 * i][unit[0], r0:r1, head_cols(unit[1])].astype(BF16)
        return lax.dot_general(k, qq, (((1,), (1,)), ((), ())), preferred_element_type=F32)

    def value_chunk(unit, e, chunk):
        i, r0, r1 = chunk
        v_t = kv_refs[2 * i + 1][unit[0], head_cols(unit[1]), r0:r1].astype(BF16)
        ones_row = lax.broadcasted_iota(jnp.int32, (BF16_SUBLANES, r1 - r0), 0) == 0
        v_ext = jnp.concatenate([v_t, jnp.where(ones_row, 1.0, 0.0).astype(BF16)], axis=0)
        return _dot(v_ext, e)

    def column_max(scores):
        m = jnp.max(scores[0], axis=0, keepdims=True)
        for s in scores[1:]:
            m = jnp.maximum(m, jnp.max(s, axis=0, keepdims=True))
        return m

    def output_phase(unit, pv):
        b, h, rows = unit
        r = 1.0 / pv[V_DIM:V_DIM + 1, :]
        o_t = pv[:V_DIM, :tu] * r[:, :tu] - pv[:V_DIM, tu:] * (r[:, tu:] * lam)
        ms = jnp.mean(o_t * o_t, axis=0, keepdims=True)
        o = (o_t * lax.rsqrt(ms + EPS)).T
        o_ref[b, rows, head_cols(h)] = (o * sub_ref[...] * out_scale).astype(BF16)

    n_units = len(units)
    scores, exps = {}, {}
    for t in range(n_units + 2):
        u_exp, u_val = t - 1, t - 2
        do_score, do_exp, do_val = t < n_units, 0 <= u_exp < n_units, 0 <= u_val < n_units
        if do_score:
            qq = stacked_queries(units[t])
            scores[t] = []
        if do_exp:
            m = column_max(scores[u_exp])
            exps[u_exp] = []
        pv = None
        for ci, c in enumerate(chunks):
            if do_score:
                scores[t].append(score_chunk(units[t], qq, c))
            if do_exp:
                exps[u_exp].append(jnp.exp(scores[u_exp][ci] - m).astype(BF16))
            if do_val:
                part = value_chunk(units[u_val], exps[u_val][ci], c)
                pv = part if pv is None else pv + part
        if do_exp:
            del scores[u_exp]
        if do_val:
            del exps[u_val]
            output_phase(units[u_val], pv)


def _attn_call(lam, subln, q, kvs, *, nb, tq, out_scale):
    bsz, seq, d_attn = q.shape
    assert d_attn == N_HEADS * V_DIM
    in_specs = [
        _const_spec(lam.shape),
        _const_spec(subln.shape),
        pl.BlockSpec((nb, tq, d_attn), lambda b, i: (b, i, 0)),
    ]
    args = [lam, subln, q]
    for kv in kvs:
        in_specs.append(pl.BlockSpec((nb,) + kv.shape[1:], lambda b, i: (b, 0, 0)))
        args.append(kv)
    return pl.pallas_call(
        functools.partial(_attn_kernel, n_src=len(kvs) // 2, tq=tq, out_scale=out_scale),
        grid=(bsz // nb, seq // tq),
        in_specs=in_specs,
        out_specs=pl.BlockSpec((nb, tq, d_attn), lambda b, i: (b, i, 0)),
        out_shape=jax.ShapeDtypeStruct((bsz, seq, d_attn), BF16),
        compiler_params=_params("parallel", "parallel"),
        name="attn%d" % (len(kvs) // 2),
    )(*args)


def _fill_with_halo(dst_ref, prev_ref, mids, next_ref, tm):
    s = pl.program_id(1)
    last = pl.num_programs(1) - 1
    prev = prev_ref[0].astype(dst_ref.dtype)
    nxt = next_ref[0].astype(dst_ref.dtype)
    for b, mid in enumerate(mids):
        base = b * (tm + 2 * HALO)
        dst_ref[base:base + HALO] = jnp.where(s > 0, prev, jnp.zeros_like(prev))
        dst_ref[base + HALO:base + HALO + tm] = mid
        dst_ref[base + HALO + tm:base + 2 * HALO + tm] = jnp.where(s < last, nxt, jnp.zeros_like(nxt))


def _mix_kernel(x_ref, u_ref, uprev_ref, unext_ref, o_ref, gt_ref, gate1_ref, shift2_ref, scale2_ref,
                n2_ref, wpool_ref, pscale_ref, wpa_ref, wpb_ref, wout_ref, band_ref,
                x1_ref, h2_ref, pad_ref, *, tm, seq):
    d = x_ref.shape[-1]
    u = u_ref[0]
    uf = u.astype(F32)
    _fill_with_halo(pad_ref, uprev_ref, [u], unext_ref, tm)
    t = pl.program_id(1) * tm + lax.broadcasted_iota(jnp.int32, (tm, 1), 0)

    n_groups = len(POOL_WINDOWS)
    cols_per_group = d // n_groups
    mixed, y_b = [], []
    for gi, w in enumerate(POOL_WINDOWS):
        cols = slice(gi * LANES, (gi + 1) * LANES)
        y_b.append(_dot(o_ref[0], wpb_ref[:, gi * cols_per_group:(gi + 1) * cols_per_group]))
        sums = [_dot(band_ref[gi], pad_ref[r:r + POOL_BLOCK + 2 * HALO, cols]) for r in range(0, tm, POOL_BLOCK)]
        acc = sums[0] if len(sums) == 1 else jnp.concatenate(sums, axis=0)
        lo = jnp.maximum(t - w // 2, 0)
        hi = jnp.minimum(t + (w - w // 2), seq)
        cnt = (hi - lo).astype(F32)
        pooled = acc / cnt - uf[:, cols]
        mixed.append(_dot(pooled.astype(BF16), wpool_ref[gi]) * pscale_ref[:, cols])
    mixed = jnp.concatenate(mixed, axis=1).astype(BF16)
    y_b = jnp.concatenate(y_b, axis=1)

    y_a = _dot(mixed, wpa_ref[...])
    g = jax.nn.sigmoid(gt_ref[0].astype(F32))
    merged = (g[:, :d] * y_a + g[:, d:] * y_b).astype(BF16)
    x1 = x_ref[0] + gate1_ref[0] * _dot(merged, wout_ref[...])
    x1_ref[0] = x1
    h2_ref[0] = _rms_modulate(x1, n2_ref[...], shift2_ref[0], scale2_ref[0]).astype(BF16)


def _halo_specs(width, tm, seq, nb=1):
    blocks_per_tile = tm // HALO
    last_block = seq // HALO - 1
    prev = pl.BlockSpec((1, HALO, width), lambda b, s: (b * nb, jnp.maximum(s * blocks_per_tile - 1, 0), 0))
    nxt = pl.BlockSpec((1, HALO, width),
                       lambda b, s: (b * nb, jnp.minimum((s + 1) * blocks_per_tile, last_block), 0))
    return prev, nxt


def _pool_bands():
    t = jnp.arange(POOL_BLOCK, dtype=jnp.int32)[:, None]
    j = jnp.arange(POOL_BLOCK + 2 * HALO, dtype=jnp.int32)[None, :]
    rel = j - HALO - t
    return jnp.stack([((rel >= -(w // 2)) & (rel < w - w // 2)).astype(BF16) for w in POOL_WINDOWS])


def _mix_call(x, u, o, gates, gate1, shift2, scale2, n2, w_pool, pscale, wpa, wpb, wout, *, tm):
    bsz, seq, d = x.shape
    d_pool = u.shape[-1]
    assert tm % POOL_BLOCK == 0 and max(POOL_WINDOWS) // 2 <= HALO
    bands = _pool_bands()
    tok = lambda b, s: (b, s, 0)
    uprev, unext = _halo_specs(d_pool, tm, seq)
    in_specs = [
        pl.BlockSpec((1, tm, d), tok),
        pl.BlockSpec((1, tm, d_pool), tok), uprev, unext,
        pl.BlockSpec((1, tm, o.shape[-1]), tok),
        pl.BlockSpec((1, tm, gates.shape[-1]), tok),
        pl.BlockSpec((1, 1, d), _row_spec(gate1.shape[0])),
        pl.BlockSpec((1, 1, d), _row_spec(shift2.shape[0])),
        pl.BlockSpec((1, 1, d), _row_spec(scale2.shape[0])),
        _const_spec(n2.shape), _const_spec(w_pool.shape), _const_spec(pscale.shape),
        _const_spec(wpa.shape), _const_spec(wpb.shape), _const_spec(wout.shape), _const_spec(bands.shape),
    ]
    return pl.pallas_call(
        functools.partial(_mix_kernel, tm=tm, seq=seq),
        grid=(bsz, seq // tm),
        in_specs=in_specs,
        out_specs=[pl.BlockSpec((1, tm, d), tok), pl.BlockSpec((1, tm, d), tok)],
        out_shape=[jax.ShapeDtypeStruct((bsz, seq, d), F32), jax.ShapeDtypeStruct((bsz, seq, d), BF16)],
        scratch_shapes=[pltpu.VMEM((tm + 2 * HALO, d_pool), BF16)],
        compiler_params=_params("parallel", "parallel"),
        name="mix",
    )(x, u, u, u, o, gates, gate1, shift2, scale2, n2, w_pool, pscale, wpa, wpb, wout, bands)


def _ffn_kernel(x1_ref, h2_ref, hprev_ref, hnext_ref, win_ref, cw_ref, cb_ref, wout_ref, gate2_ref, fnw_ref,
                y_ref, hext_ref, act_ref, *, tm, d_ff):
    nb, _, d = x1_ref.shape
    _fill_with_halo(hext_ref, hprev_ref, [h2_ref[b] for b in range(nb)], hnext_ref, tm)
    he = hext_ref[...]
    h2 = h2_ref[...].reshape(nb * tm, d)

    def tile_rows(t):
        parts = [t[b * (tm + 2 * HALO) + HALO:b * (tm + 2 * HALO) + HALO + tm] for b in range(nb)]
        return parts[0] if nb == 1 else jnp.concatenate(parts, axis=0)

    for c in range(d_ff // FFN_CHUNK):
        cols = slice(c * FFN_CHUNK, (c + 1) * FFN_CHUNK)
        a_ext = _dot(he, win_ref[:, cols])
        up = _dot(h2, win_ref[:, d_ff + c * FFN_CHUNK:d_ff + (c + 1) * FFN_CHUNK])
        rows = a_ext.shape[0]
        conv = (tile_rows(pltpu.roll(a_ext, 1, 0)) * cw_ref[0:1, cols]
                + tile_rows(a_ext) * cw_ref[1:2, cols]
                + tile_rows(pltpu.roll(a_ext, rows - 1, 0)) * cw_ref[2:3, cols]
                + cb_ref[:, cols])
        act_ref[:, cols] = (conv * jax.nn.sigmoid(conv) * up).astype(BF16)
    x2 = x1_ref[...].reshape(nb * tm, d) + gate2_ref[0] * _dot(act_ref[...], wout_ref[...])
    ms = jnp.mean(x2 * x2, axis=-1, keepdims=True)
    y_ref[...] = (x2 * lax.rsqrt(ms + EPS) * fnw_ref[...]).reshape(nb, tm, d)


def _ffn_call(x1, h2, w_ffn_in, conv_w, conv_b, w_ffn_out, gate2, fnw, *, nb, tm):
    bsz, seq, d = x1.shape
    d_ff = w_ffn_out.shape[0]
    assert nb == 1 or (tm == seq and gate2.shape[0] == 1)
    tok = lambda b, s: (b, s, 0)
    hprev, hnext = _halo_specs(d, tm, seq, nb)
    in_specs = [
        pl.BlockSpec((nb, tm, d), tok),
        pl.BlockSpec((nb, tm, d), tok), hprev, hnext,
        _const_spec(w_ffn_in.shape), _const_spec(conv_w.shape), _const_spec(conv_b.shape),
        _const_spec(w_ffn_out.shape),
        pl.BlockSpec((1, 1, d), _row_spec(gate2.shape[0])),
        _const_spec(fnw.shape),
    ]
    return pl.pallas_call(
        functools.partial(_ffn_kernel, tm=tm, d_ff=d_ff),
        grid=(bsz // nb, seq // tm),
        in_specs=in_specs,
        out_specs=pl.BlockSpec((nb, tm, d), tok),
        out_shape=jax.ShapeDtypeStruct((bsz, seq, d), F32),
        scratch_shapes=[pltpu.VMEM((nb * (tm + 2 * HALO), d), BF16), pltpu.VMEM((nb * tm, d_ff), BF16)],
        compiler_params=_params("parallel", "parallel"),
        name="ffn",
    )(x1, h2, h2, h2, w_ffn_in, conv_w, conv_b, w_ffn_out, gate2, fnw)


def _rope_tables(n_tok):
    rows = n_tok // GRID_W
    row = jnp.repeat(jnp.arange(rows, dtype=F32), GRID_W)
    col = jnp.tile(jnp.arange(GRID_W, dtype=F32), rows)
    n_freq = HEAD_DIM // 4
    inv = ROPE_BASE ** (-jnp.arange(n_freq, dtype=F32) / n_freq)
    ang_row = row[:, None] * inv
    ang_col = col[:, None] * inv
    zeros = jnp.zeros_like(ang_row)
    cos_row, sin_row = jnp.cos(ang_row), jnp.sin(ang_row)
    cos_col, sin_col = jnp.cos(ang_col), jnp.sin(ang_col)
    cos = jnp.concatenate([cos_row, cos_row, cos_col, cos_col], axis=1)
    sa = jnp.concatenate([-sin_row, zeros, -sin_col, zeros], axis=1)
    sb = jnp.concatenate([zeros, sin_row, zeros, sin_col], axis=1)
    rep = LANES // HEAD_DIM
    return tuple(jnp.tile(t, (1, rep)) for t in (cos, sa, sb))


def _stream(x, mods, w, lam, ctx_kv, *, lam_init, final_norm_w):
    shift1, scale1, gate1, shift2, scale2, gate2 = mods
    seq = x.shape[1]
    tm = min(seq, TOKEN_TILE)
    tq = min(seq, QUERY_TILE)
    is_ctx = ctx_kv is None
    nb = TOKEN_TILE // tm if (is_ctx and tm == seq) else 1
    d_pool = w["w_proj_a"].shape[0]
    d_attn = w["w_proj_b"].shape[0]
    rope_tabs = None if is_ctx else _rope_tables(seq)
    outs = _inproj_call(x, shift1, scale1, w["norm1_w"], w["w_in"], rope_tabs,
                        nb=nb, ts=tm, emit_f32=is_ctx, d_pool=d_pool, d_attn=d_attn)
    u, q, k, v, gates = outs[:5]
    kvs = [k, v] if is_ctx else [ctx_kv[0], ctx_kv[1], k, v]
    o = _attn_call(lam, w["subln_w"], q, kvs, nb=1, tq=tq, out_scale=1.0 - lam_init)
    x1, h2 = _mix_call(x, u, o, gates, gate1, shift2, scale2, w["norm2_w"], w["w_pool"], w["pool_scale"],
                       w["w_proj_a"], w["w_proj_b"], w["w_out"], tm=tm)
    y = _ffn_call(x1, h2, w["w_ffn_in"], w["ffn_conv_w"], w["ffn_conv_b"], w["w_ffn_out"], gate2,
                  final_norm_w, nb=nb, tm=tm)
    return y, outs[5:]


def kernel(x_prompt, x_sample, cache_k, cache_v, c, c_ctx, w_ada, b_ada, w_in, w_pool, pool_scale, w_proj_a, w_proj_b, w_out, lam_q1, lam_k1, lam_q2, lam_k2, subln_w, norm1_w, norm2_w, w_ffn_in, ffn_conv_w, ffn_conv_b, w_ffn_out, final_norm_w):
    assert w_ada.shape[0] == 1, "single trunk layer"
    bsz, seq, d = x_prompt.shape
    dec_b, dec_seq, _ = x_sample.shape
    assert 1 + dec_b <= MOD_ROWS
    lam_init = 0.8 - 0.6 * math.exp(-0.3 * 0)

    cc = jnp.zeros((MOD_ROWS, d), F32).at[0].set(c_ctx).at[1:1 + dec_b].set(c)
    lamv = jnp.concatenate([lam_q1, lam_k1, lam_q2, lam_k2], axis=0)
    mod, lam = _mod_call(cc, w_ada[0], b_ada, lamv, lam_init)
    mod = mod.reshape(MOD_ROWS, 6, 1, d)
    mods_ctx = [mod[0:1, j] for j in range(6)]
    mods_lat = [mod[1:1 + dec_b, j] for j in range(6)]

    w = dict(
        w_in=w_in[0].astype(BF16), w_pool=w_pool[0].astype(BF16), pool_scale=pool_scale,
        w_proj_a=w_proj_a[0].astype(BF16), w_proj_b=w_proj_b[0].astype(BF16), w_out=w_out[0].astype(BF16),
        subln_w=subln_w, norm1_w=norm1_w, norm2_w=norm2_w,
        w_ffn_in=w_ffn_in[0].astype(BF16), ffn_conv_w=ffn_conv_w[0], ffn_conv_b=ffn_conv_b,
        w_ffn_out=w_ffn_out[0].astype(BF16),
    )
    fnw = final_norm_w.reshape(1, d)

    y_prompt, (k32, v32) = _stream(x_prompt, mods_ctx, w, lam, None, lam_init=lam_init, final_norm_w=fnw)
    d_attn = k32.shape[-1]
    ctx_kv = (cache_k[:, 0].reshape(dec_b, -1, d_attn),
              jnp.swapaxes(cache_v[:, 0].reshape(dec_b, -1, d_attn), 1, 2))
    y_sample, _ = _stream(x_sample, mods_lat, w, lam, ctx_kv, lam_init=lam_init, final_norm_w=fnw)

    new_cache_k = k32.reshape(bsz, 1, seq, N_HEADS, 2, HEAD_DIM)
    new_cache_v = v32.reshape(bsz, 1, seq, N_HEADS, V_DIM)
    return (y_prompt, y_sample, new_cache_k, new_cache_v)
```

```python
import functools
import math

import jax
import jax.numpy as jnp
from jax import lax
from jax.experimental import pallas as pl
from jax.experimental.pallas import tpu as pltpu

F32 = jnp.float32
BF16 = jnp.bfloat16

GRID_W = 64
N_HEADS = 8
HEAD_DIM = 64
V_DIM = 2 * HEAD_DIM
POOL_WINDOWS = (2, 4, 8, 16)
ROPE_BASE = 10000.0
EPS = 1e-6

LANES = 128
BF16_SUBLANES = 16
VMEM_LIMIT_BYTES = 56 * 1024 * 1024

MOD_ROWS = 16
MOD_TN = 1536
HALO = BF16_SUBLANES
FFN_CHUNK = 256
TOKEN_TILE = 512
QUERY_TILE = 512
QUERY_UNIT = 256
KEY_CHUNK = 256
POOL_BLOCK = 256


def _dot(a, b):
    return jnp.dot(a, b, preferred_element_type=F32)


def _const_spec(shape):
    zeros = (0,) * len(shape)
    return pl.BlockSpec(shape, lambda *_: zeros, pipeline_mode=pl.Buffered(1))


def _params(*sem):
    return pltpu.CompilerParams(dimension_semantics=sem, vmem_limit_bytes=VMEM_LIMIT_BYTES)


def _mod_kernel(cc_ref, w_ref, b_ref, lamv_ref, mod_ref, lam_ref, *, lam_init):
    c = cc_ref[...]
    sc = (c * jax.nn.sigmoid(c)).astype(BF16)
    mod_ref[...] = _dot(sc, w_ref[...].astype(BF16)) + b_ref[...]
    lv = lamv_ref[...]
    p1 = jnp.sum(lv[0:1] * lv[1:2], axis=-1, keepdims=True)
    p2 = jnp.sum(lv[2:3] * lv[3:4], axis=-1, keepdims=True)
    lam = jnp.exp(p1) - jnp.exp(p2) + lam_init
    lam_ref[...] = jnp.broadcast_to(lam, lam_ref.shape)


def _mod_call(cc, w_ada, b_ada, lamv, lam_init):
    d, n = w_ada.shape
    return pl.pallas_call(
        functools.partial(_mod_kernel, lam_init=lam_init),
        grid=(n // MOD_TN,),
        in_specs=[
            pl.BlockSpec((MOD_ROWS, d), lambda j: (0, 0)),
            pl.BlockSpec((d, MOD_TN), lambda j: (0, j)),
            pl.BlockSpec((1, MOD_TN), lambda j: (0, j)),
            pl.BlockSpec(lamv.shape, lambda j: (0, 0)),
        ],
        out_specs=[
            pl.BlockSpec((MOD_ROWS, MOD_TN), lambda j: (0, j)),
            pl.BlockSpec((8, LANES), lambda j: (0, 0)),
        ],
        out_shape=[
            jax.ShapeDtypeStruct((MOD_ROWS, n), F32),
            jax.ShapeDtypeStruct((8, LANES), F32),
        ],
        compiler_params=_params("arbitrary"),
        name="mod",
    )(cc, w_ada, b_ada, lamv)


def _rms_modulate(x, g, shift, scale):
    ms = jnp.mean(x * x, axis=-1, keepdims=True)
    return x * lax.rsqrt(ms + EPS) * g * (1.0 + scale) + shift


def _inproj_kernel(*refs, rope, emit_f32, d_pool, d_attn):
    x_ref, shift_ref, scale_ref, g_ref, w_ref = refs[:5]
    pos = 5
    if rope:
        cos_ref, sa_ref, sb_ref = refs[pos:pos + 3]
        pos += 3
    u_ref, q_ref, k_ref, v_ref, gt_ref = refs[pos:pos + 5]
    pos += 5
    if emit_f32:
        k32_ref, v32_ref = refs[pos:pos + 2]

    nb, ts, d = x_ref.shape
    rows = nb * ts
    x = x_ref[...].reshape(rows, d)
    hb = _rms_modulate(x, g_ref[...], shift_ref[0], scale_ref[0]).astype(BF16)

    def put(ref, val):
        ref[...] = val.reshape(nb, ts, val.shape[-1])

    def rotate(t):
        if not rope:
            return t
        cos, sa, sb = cos_ref[...], sa_ref[...], sb_ref[...]
        outs = []
        for c in range(t.shape[1] // LANES):
            tc = t[:, c * LANES:(c + 1) * LANES]
            outs.append(tc * cos + pltpu.roll(tc, LANES - HEAD_DIM // 4, 1) * sa
                        + pltpu.roll(tc, HEAD_DIM // 4, 1) * sb)
        return jnp.concatenate(outs, axis=1)

    o0 = 0
    put(u_ref, _dot(hb, w_ref[:, o0:o0 + d_pool]).astype(BF16))
    o0 += d_pool
    q = _dot(hb, w_ref[:, o0:o0 + d_attn])
    put(q_ref, (rotate(q) * (HEAD_DIM ** -0.5)).astype(BF16))
    o0 += d_attn
    k = _dot(hb, w_ref[:, o0:o0 + d_attn])
    if emit_f32:
        put(k32_ref, k)
    put(k_ref, rotate(k).astype(BF16))
    o0 += d_attn
    v = _dot(hb, w_ref[:, o0:o0 + d_attn])
    if emit_f32:
        put(v32_ref, v)
    v_t = v.T.astype(BF16)
    for b in range(nb):
        v_ref[b] = v_t[:, b * ts:(b + 1) * ts]
    o0 += d_attn
    put(gt_ref, _dot(hb, w_ref[:, o0:]).astype(BF16))


def _row_spec(n_rows):
    if n_rows == 1:
        return lambda b, s: (0, 0, 0)
    return lambda b, s: (b, 0, 0)


def _inproj_call(x, shift, scale, g, w_in, rope_tabs, *, nb, ts, emit_f32, d_pool, d_attn):
    bsz, seq, d = x.shape
    d_in = w_in.shape[1]
    d_gate = d_in - d_pool - 3 * d_attn
    rope = rope_tabs is not None
    assert nb == 1 or (shift.shape[0] == 1 and not rope)
    tok = lambda b, s: (b, s, 0)
    in_specs = [
        pl.BlockSpec((nb, ts, d), tok),
        pl.BlockSpec((1, 1, d), _row_spec(shift.shape[0])),
        pl.BlockSpec((1, 1, d), _row_spec(scale.shape[0])),
        _const_spec((1, d)),
        _const_spec(w_in.shape),
    ]
    args = [x, shift, scale, g, w_in]
    if rope:
        in_specs += [pl.BlockSpec((ts, LANES), lambda b, s: (s, 0))] * 3
        args += list(rope_tabs)
    widths = [d_pool, d_attn, d_attn, d_attn, d_gate]
    out_specs = [pl.BlockSpec((nb, ts, w), tok) for w in widths]
    out_shape = [jax.ShapeDtypeStruct((bsz, seq, w), BF16) for w in widths]
    out_specs[3] = pl.BlockSpec((nb, d_attn, ts), lambda b, s: (b, 0, s))
    out_shape[3] = jax.ShapeDtypeStruct((bsz, d_attn, seq), BF16)
    if emit_f32:
        out_specs += [pl.BlockSpec((nb, ts, d_attn), tok)] * 2
        out_shape += [jax.ShapeDtypeStruct((bsz, seq, d_attn), F32)] * 2
    return pl.pallas_call(
        functools.partial(_inproj_kernel, rope=rope, emit_f32=emit_f32, d_pool=d_pool, d_attn=d_attn),
        grid=(bsz // nb, seq // ts),
        in_specs=in_specs,
        out_specs=out_specs,
        out_shape=out_shape,
        compiler_params=_params("parallel", "parallel"),
        name="in_proj_rope" if rope else "in_proj",
    )(*args)


def _attn_kernel(*refs, n_src, tq, out_scale):
    lam_ref, sub_ref, q_ref = refs[:3]
    kv_refs = refs[3:3 + 2 * n_src]
    o_ref = refs[3 + 2 * n_src]
    lam = lam_ref[0:1, 0:1]
    tu = min(tq, QUERY_UNIT)
    lane = lax.broadcasted_iota(jnp.int32, (tu, V_DIM), 1)
    chunks = []
    for i in range(n_src):
        n_keys = kv_refs[2 * i].shape[1]
        step = min(n_keys, KEY_CHUNK)
        chunks += [(i, r, r + step) for r in range(0, n_keys, step)]
    units = [(b, h, slice(j * tu, (j + 1) * tu))
             for b in range(q_ref.shape[0]) for h in range(N_HEADS) for j in range(tq // tu)]

    def head_cols(h):
        return slice(h * V_DIM, (h + 1) * V_DIM)

    def stacked_queries(unit):
        b, h, rows = unit
        q = q_ref[b, rows, head_cols(h)].astype(F32)
        return jnp.concatenate([jnp.where(lane < HEAD_DIM, q, 0.0),
                                jnp.where(lane >= HEAD_DIM, q, 0.0)], axis=0).astype(BF16)

    def score_chunk(unit, qq, chunk):
        i, r0, r1 = chunk
        k = kv_refs[2 * i][unit[0], r0:r1, head_cols(unit[1])].astype(BF16)
        return lax.dot_general(k, qq, (((1,), (1,)), ((), ())), preferred_element_type=F32)

    def value_chunk(unit, e, chunk):
        i, r0, r1 = chunk
        v_t = kv_refs[2 * i + 1][unit[0], head_cols(unit[1]), r0:r1].astype(BF16)
        ones_row = lax.broadcasted_iota(jnp.int32, (BF16_SUBLANES, r1 - r0), 0) == 0
        v_ext = jnp.concatenate([v_t, jnp.where(ones_row, 1.0, 0.0).astype(BF16)], axis=0)
        return _dot(v_ext, e)

    def column_max(scores):
        m = jnp.max(scores[0], axis=0, keepdims=True)
        for s in scores[1:]:
            m = jnp.maximum(m, jnp.max(s, axis=0, keepdims=True))
        return m

    def output_phase(unit, pv):
        b, h, rows = unit
        r = 1.0 / pv[V_DIM:V_DIM + 1, :]
        o_t = pv[:V_DIM, :tu] * r[:, :tu] - pv[:V_DIM, tu:] * (r[:, tu:] * lam)
        ms = jnp.mean(o_t * o_t, axis=0, keepdims=True)
        o = (o_t * lax.rsqrt(ms + EPS)).T
        o_ref[b, rows, head_cols(h)] = (o * sub_ref[...] * out_scale).astype(BF16)

    n_units = len(units)
    scores, exps = {}, {}
    for t in range(n_units + 2):
        u_exp, u_val = t - 1, t - 2
        do_score, do_exp, do_val = t < n_units, 0 <= u_exp < n_units, 0 <= u_val < n_units
        if do_score:
            qq = stacked_queries(units[t])
            scores[t] = []
        if do_exp:
            m = column_max(scores[u_exp])
            exps[u_exp] = []
        pv = None
        for ci, c in enumerate(chunks):
            if do_score:
                scores[t].append(score_chunk(units[t], qq, c))
            if do_exp:
                exps[u_exp].append(jnp.exp(scores[u_exp][ci] - m).astype(BF16))
            if do_val:
                part = value_chunk(units[u_val], exps[u_val][ci], c)
                pv = part if pv is None else pv + part
        if do_exp:
            del scores[u_exp]
        if do_val:
            del exps[u_val]
            output_phase(units[u_val], pv)


def _attn_call(lam, subln, q, kvs, *, nb, tq, out_scale):
    bsz, seq, d_attn = q.shape
    assert d_attn == N_HEADS * V_DIM
    in_specs = [
        _const_spec(lam.shape),
        _const_spec(subln.shape),
        pl.BlockSpec((nb, tq, d_attn), lambda b, i: (b, i, 0)),
    ]
    args = [lam, subln, q]
    for kv in kvs:
        in_specs.append(pl.BlockSpec((nb,) + kv.shape[1:], lambda b, i: (b, 0, 0)))
        args.append(kv)
    return pl.pallas_call(
        functools.partial(_attn_kernel, n_src=len(kvs) // 2, tq=tq, out_scale=out_scale),
        grid=(bsz // nb, seq // tq),
        in_specs=in_specs,
        out_specs=pl.BlockSpec((nb, tq, d_attn), lambda b, i: (b, i, 0)),
        out_shape=jax.ShapeDtypeStruct((bsz, seq, d_attn), BF16),
        compiler_params=_params("parallel", "parallel"),
        name="attn%d" % (len(kvs) // 2),
    )(*args)


def _fill_with_halo(dst_ref, prev_ref, mids, next_ref, tm):
    s = pl.program_id(1)
    last = pl.num_programs(1) - 1
    prev = prev_ref[0].astype(dst_ref.dtype)
    nxt = next_ref[0].astype(dst_ref.dtype)
    for b, mid in enumerate(mids):
        base = b * (tm + 2 * HALO)
        dst_ref[base:base + HALO] = jnp.where(s > 0, prev, jnp.zeros_like(prev))
        dst_ref[base + HALO:base + HALO + tm] = mid
        dst_ref[base + HALO + tm:base + 2 * HALO + tm] = jnp.where(s < last, nxt, jnp.zeros_like(nxt))


def _mix_kernel(x_ref, u_ref, uprev_ref, unext_ref, o_ref, gt_ref, gate1_ref, shift2_ref, scale2_ref,
                n2_ref, wpool_ref, pscale_ref, wpa_ref, wpb_ref, wout_ref, band_ref,
                x1_ref, h2_ref, pad_ref, *, tm, seq):
    nb, _, d = x_ref.shape
    rows = nb * tm
    _fill_with_halo(pad_ref, uprev_ref, [u_ref[b] for b in range(nb)], unext_ref, tm)
    uf = u_ref[...].reshape(rows, u_ref.shape[-1]).astype(F32)
    o = o_ref[...].reshape(rows, o_ref.shape[-1])
    t = pl.program_id(1) * tm + lax.rem(lax.broadcasted_iota(jnp.int32, (rows, 1), 0), tm)

    n_groups = len(POOL_WINDOWS)
    cols_per_group = d // n_groups
    block_starts = [b * (tm + 2 * HALO) + r for b in range(nb) for r in range(0, tm, POOL_BLOCK)]
    mixed, y_b = [], []
    for gi, w in enumerate(POOL_WINDOWS):
        cols = slice(gi * LANES, (gi + 1) * LANES)
        y_b.append(_dot(o, wpb_ref[:, gi * cols_per_group:(gi + 1) * cols_per_group]))
        sums = [_dot(band_ref[gi], pad_ref[r:r + POOL_BLOCK + 2 * HALO, cols]) for r in block_starts]
        acc = sums[0] if len(sums) == 1 else jnp.concatenate(sums, axis=0)
        lo = jnp.maximum(t - w // 2, 0)
        hi = jnp.minimum(t + (w - w // 2), seq)
        cnt = (hi - lo).astype(F32)
        pooled = acc / cnt - uf[:, cols]
        mixed.append(_dot(pooled.astype(BF16), wpool_ref[gi]) * pscale_ref[:, cols])
    mixed = jnp.concatenate(mixed, axis=1).astype(BF16)
    y_b = jnp.concatenate(y_b, axis=1)

    y_a = _dot(mixed, wpa_ref[...])
    g = jax.nn.sigmoid(gt_ref[...].reshape(rows, gt_ref.shape[-1]).astype(F32))
    merged = (g[:, :d] * y_a + g[:, d:] * y_b).astype(BF16)
    x1 = x_ref[...].reshape(rows, d) + gate1_ref[0] * _dot(merged, wout_ref[...])
    x1_ref[...] = x1.reshape(nb, tm, d)
    h2 = _rms_modulate(x1, n2_ref[...], shift2_ref[0], scale2_ref[0]).astype(BF16)
    h2_ref[...] = h2.reshape(nb, tm, d)


def _halo_specs(width, tm, seq, nb=1):
    blocks_per_tile = tm // HALO
    last_block = seq // HALO - 1
    prev = pl.BlockSpec((1, HALO, width), lambda b, s: (b * nb, jnp.maximum(s * blocks_per_tile - 1, 0), 0))
    nxt = pl.BlockSpec((1, HALO, width),
                       lambda b, s: (b * nb, jnp.minimum((s + 1) * blocks_per_tile, last_block), 0))
    return prev, nxt


def _pool_bands():
    t = jnp.arange(POOL_BLOCK, dtype=jnp.int32)[:, None]
    j = jnp.arange(POOL_BLOCK + 2 * HALO, dtype=jnp.int32)[None, :]
    rel = j - HALO - t
    return jnp.stack([((rel >= -(w // 2)) & (rel < w - w // 2)).astype(BF16) for w in POOL_WINDOWS])


def _mix_call(x, u, o, gates, gate1, shift2, scale2, n2, w_pool, pscale, wpa, wpb, wout, *, nb, tm):
    bsz, seq, d = x.shape
    d_pool = u.shape[-1]
    assert tm % POOL_BLOCK == 0 and max(POOL_WINDOWS) // 2 <= HALO
    assert nb == 1 or (tm == seq and gate1.shape[0] == 1)
    bands = _pool_bands()
    tok = lambda b, s: (b, s, 0)
    uprev, unext = _halo_specs(d_pool, tm, seq, nb)
    in_specs = [
        pl.BlockSpec((nb, tm, d), tok),
        pl.BlockSpec((nb, tm, d_pool), tok), uprev, unext,
        pl.BlockSpec((nb, tm, o.shape[-1]), tok),
        pl.BlockSpec((nb, tm, gates.shape[-1]), tok),
        pl.BlockSpec((1, 1, d), _row_spec(gate1.shape[0])),
        pl.BlockSpec((1, 1, d), _row_spec(shift2.shape[0])),
        pl.BlockSpec((1, 1, d), _row_spec(scale2.shape[0])),
        _const_spec(n2.shape), _const_spec(w_pool.shape), _const_spec(pscale.shape),
        _const_spec(wpa.shape), _const_spec(wpb.shape), _const_spec(wout.shape), _const_spec(bands.shape),
    ]
    return pl.pallas_call(
        functools.partial(_mix_kernel, tm=tm, seq=seq),
        grid=(bsz // nb, seq // tm),
        in_specs=in_specs,
        out_specs=[pl.BlockSpec((nb, tm, d), tok), pl.BlockSpec((nb, tm, d), tok)],
        out_shape=[jax.ShapeDtypeStruct((bsz, seq, d), F32), jax.ShapeDtypeStruct((bsz, seq, d), BF16)],
        scratch_shapes=[pltpu.VMEM((nb * (tm + 2 * HALO), d_pool), BF16)],
        compiler_params=_params("parallel", "parallel"),
        name="mix",
    )(x, u, u, u, o, gates, gate1, shift2, scale2, n2, w_pool, pscale, wpa, wpb, wout, bands)


def _ffn_kernel(x1_ref, h2_ref, hprev_ref, hnext_ref, win_ref, cw_ref, cb_ref, wout_ref, gate2_ref, fnw_ref,
                y_ref, hext_ref, act_ref, *, tm, d_ff):
    nb, _, d = x1_ref.shape
    _fill_with_halo(hext_ref, hprev_ref, [h2_ref[b] for b in range(nb)], hnext_ref, tm)
    he = hext_ref[...]
    h2 = h2_ref[...].reshape(nb * tm, d)

    def tile_rows(t):
        parts = [t[b * (tm + 2 * HALO) + HALO:b * (tm + 2 * HALO) + HALO + tm] for b in range(nb)]
        return parts[0] if nb == 1 else jnp.concatenate(parts, axis=0)

    for c in range(d_ff // FFN_CHUNK):
        cols = slice(c * FFN_CHUNK, (c + 1) * FFN_CHUNK)
        a_ext = _dot(he, win_ref[:, cols])
        up = _dot(h2, win_ref[:, d_ff + c * FFN_CHUNK:d_ff + (c + 1) * FFN_CHUNK])
        rows = a_ext.shape[0]
        conv = (tile_rows(pltpu.roll(a_ext, 1, 0)) * cw_ref[0:1, cols]
                + tile_rows(a_ext) * cw_ref[1:2, cols]
                + tile_rows(pltpu.roll(a_ext, rows - 1, 0)) * cw_ref[2:3, cols]
                + cb_ref[:, cols])
        act_ref[:, cols] = (conv * jax.nn.sigmoid(conv) * up).astype(BF16)
    x2 = x1_ref[...].reshape(nb * tm, d) + gate2_ref[0] * _dot(act_ref[...], wout_ref[...])
    ms = jnp.mean(x2 * x2, axis=-1, keepdims=True)
    y_ref[...] = (x2 * lax.rsqrt(ms + EPS) * fnw_ref[...]).reshape(nb, tm, d)


def _ffn_call(x1, h2, w_ffn_in, conv_w, conv_b, w_ffn_out, gate2, fnw, *, nb, tm):
    bsz, seq, d = x1.shape
    d_ff = w_ffn_out.shape[0]
    assert nb == 1 or (tm == seq and gate2.shape[0] == 1)
    tok = lambda b, s: (b, s, 0)
    hprev, hnext = _halo_specs(d, tm, seq, nb)
    in_specs = [
        pl.BlockSpec((nb, tm, d), tok),
        pl.BlockSpec((nb, tm, d), tok), hprev, hnext,
        _const_spec(w_ffn_in.shape), _const_spec(conv_w.shape), _const_spec(conv_b.shape),
        _const_spec(w_ffn_out.shape),
        pl.BlockSpec((1, 1, d), _row_spec(gate2.shape[0])),
        _const_spec(fnw.shape),
    ]
    return pl.pallas_call(
        functools.partial(_ffn_kernel, tm=tm, d_ff=d_ff),
        grid=(bsz // nb, seq // tm),
        in_specs=in_specs,
        out_specs=pl.BlockSpec((nb, tm, d), tok),
        out_shape=jax.ShapeDtypeStruct((bsz, seq, d), F32),
        scratch_shapes=[pltpu.VMEM((nb * (tm + 2 * HALO), d), BF16), pltpu.VMEM((nb * tm, d_ff), BF16)],
        compiler_params=_params("parallel", "parallel"),
        name="ffn",
    )(x1, h2, h2, h2, w_ffn_in, conv_w, conv_b, w_ffn_out, gate2, fnw)


def _rope_tables(n_tok):
    rows = n_tok // GRID_W
    row = jnp.repeat(jnp.arange(rows, dtype=F32), GRID_W)
    col = jnp.tile(jnp.arange(GRID_W, dtype=F32), rows)
    n_freq = HEAD_DIM // 4
    inv = ROPE_BASE ** (-jnp.arange(n_freq, dtype=F32) / n_freq)
    ang_row = row[:, None] * inv
    ang_col = col[:, None] * inv
    zeros = jnp.zeros_like(ang_row)
    cos_row, sin_row = jnp.cos(ang_row), jnp.sin(ang_row)
    cos_col, sin_col = jnp.cos(ang_col), jnp.sin(ang_col)
    cos = jnp.concatenate([cos_row, cos_row, cos_col, cos_col], axis=1)
    sa = jnp.concatenate([-sin_row, zeros, -sin_col, zeros], axis=1)
    sb = jnp.concatenate([zeros, sin_row, zeros, sin_col], axis=1)
    rep = LANES // HEAD_DIM
    return tuple(jnp.tile(t, (1, rep)) for t in (cos, sa, sb))


def _stream(x, mods, w, lam, ctx_kv, *, lam_init, final_norm_w):
    shift1, scale1, gate1, shift2, scale2, gate2 = mods
    seq = x.shape[1]
    tm = min(seq, TOKEN_TILE)
    tq = min(seq, QUERY_TILE)
    is_ctx = ctx_kv is None
    nb = TOKEN_TILE // tm if (is_ctx and tm == seq) else 1
    d_pool = w["w_proj_a"].shape[0]
    d_attn = w["w_proj_b"].shape[0]
    rope_tabs = None if is_ctx else _rope_tables(seq)
    outs = _inproj_call(x, shift1, scale1, w["norm1_w"], w["w_in"], rope_tabs,
                        nb=nb, ts=tm, emit_f32=is_ctx, d_pool=d_pool, d_attn=d_attn)
    u, q, k, v, gates = outs[:5]
    kvs = [k, v] if is_ctx else [ctx_kv[0], ctx_kv[1], k, v]
    o = _attn_call(lam, w["subln_w"], q, kvs, nb=1, tq=tq, out_scale=1.0 - lam_init)
    x1, h2 = _mix_call(x, u, o, gates, gate1, shift2, scale2, w["norm2_w"], w["w_pool"], w["pool_scale"],
                       w["w_proj_a"], w["w_proj_b"], w["w_out"], nb=nb, tm=tm)
    y = _ffn_call(x1, h2, w["w_ffn_in"], w["ffn_conv_w"], w["ffn_conv_b"], w["w_ffn_out"], gate2,
                  final_norm_w, nb=nb, tm=tm)
    return y, outs[5:]


def kernel(x_prompt, x_sample, cache_k, cache_v, c, c_ctx, w_ada, b_ada, w_in, w_pool, pool_scale, w_proj_a, w_proj_b, w_out, lam_q1, lam_k1, lam_q2, lam_k2, subln_w, norm1_w, norm2_w, w_ffn_in, ffn_conv_w, ffn_conv_b, w_ffn_out, final_norm_w):
    assert w_ada.shape[0] == 1, "single trunk layer"
    bsz, seq, d = x_prompt.shape
    dec_b, dec_seq, _ = x_sample.shape
    assert 1 + dec_b <= MOD_ROWS
    lam_init = 0.8 - 0.6 * math.exp(-0.3 * 0)

    cc = jnp.zeros((MOD_ROWS, d), F32).at[0].set(c_ctx).at[1:1 + dec_b].set(c)
    lamv = jnp.concatenate([lam_q1, lam_k1, lam_q2, lam_k2], axis=0)
    mod, lam = _mod_call(cc, w_ada[0], b_ada, lamv, lam_init)
    mod = mod.reshape(MOD_ROWS, 6, 1, d)
    mods_ctx = [mod[0:1, j] for j in range(6)]
    mods_lat = [mod[1:1 + dec_b, j] for j in range(6)]

    w = dict(
        w_in=w_in[0].astype(BF16), w_pool=w_pool[0].astype(BF16), pool_scale=pool_scale,
        w_proj_a=w_proj_a[0].astype(BF16), w_proj_b=w_proj_b[0].astype(BF16), w_out=w_out[0].astype(BF16),
        subln_w=subln_w, norm1_w=norm1_w, norm2_w=norm2_w,
        w_ffn_in=w_ffn_in[0].astype(BF16), ffn_conv_w=ffn_conv_w[0], ffn_conv_b=ffn_conv_b,
        w_ffn_out=w_ffn_out[0].astype(BF16),
    )
    fnw = final_norm_w.reshape(1, d)

    y_prompt, (k32, v32) = _stream(x_prompt, mods_ctx, w, lam, None, lam_init=lam_init, final_norm_w=fnw)
    d_attn = k32.shape[-1]
    ctx_kv = (cache_k[:, 0].reshape(dec_b, -1, d_attn),
              jnp.swapaxes(cache_v[:, 0].reshape(dec_b, -1, d_attn), 1, 2))
    y_sample, _ = _stream(x_sample, mods_lat, w, lam, ctx_kv, lam_init=lam_init, final_norm_w=fnw)

    new_cache_k = k32.reshape(bsz, 1, seq, N_HEADS, 2, HEAD_DIM)
    new_cache_v = v32.reshape(bsz, 1, seq, N_HEADS, V_DIM)
    return (y_prompt, y_sample, new_cache_k, new_cache_v)
```

```python
import functools
import math

import jax
import jax.numpy as jnp
from jax import lax
from jax.experimental import pallas as pl
from jax.experimental.pallas import tpu as pltpu

F32 = jnp.float32
BF16 = jnp.bfloat16

GRID_W = 64
N_HEADS = 8
HEAD_DIM = 64
V_DIM = 2 * HEAD_DIM
POOL_WINDOWS = (2, 4, 8, 16)
ROPE_BASE = 10000.0
EPS = 1e-6

LANES = 128
BF16_SUBLANES = 16
VMEM_LIMIT_BYTES = 56 * 1024 * 1024

MOD_ROWS = 16
MOD_TN = 1536
HALO = BF16_SUBLANES
FFN_CHUNK = 256
TOKEN_TILE = 512
QUERY_TILE = 1024
QUERY_UNIT = 256
KEY_CHUNK = 256
POOL_BLOCK = 256


def _dot(a, b):
    return jnp.dot(a, b, preferred_element_type=F32)


def _const_spec(shape):
    zeros = (0,) * len(shape)
    return pl.BlockSpec(shape, lambda *_: zeros, pipeline_mode=pl.Buffered(1))


def _params(*sem):
    return pltpu.CompilerParams(dimension_semantics=sem, vmem_limit_bytes=VMEM_LIMIT_BYTES)


def _mod_kernel(cc_ref, w_ref, b_ref, lamv_ref, mod_ref, lam_ref, *, lam_init):
    c = cc_ref[...]
    sc = (c * jax.nn.sigmoid(c)).astype(BF16)
    mod_ref[...] = _dot(sc, w_ref[...].astype(BF16)) + b_ref[...]
    lv = lamv_ref[...]
    p1 = jnp.sum(lv[0:1] * lv[1:2], axis=-1, keepdims=True)
    p2 = jnp.sum(lv[2:3] * lv[3:4], axis=-1, keepdims=True)
    lam = jnp.exp(p1) - jnp.exp(p2) + lam_init
    lam_ref[...] = jnp.broadcast_to(lam, lam_ref.shape)


def _mod_call(cc, w_ada, b_ada, lamv, lam_init):
    d, n = w_ada.shape
    return pl.pallas_call(
        functools.partial(_mod_kernel, lam_init=lam_init),
        grid=(n // MOD_TN,),
        in_specs=[
            pl.BlockSpec((MOD_ROWS, d), lambda j: (0, 0)),
            pl.BlockSpec((d, MOD_TN), lambda j: (0, j)),
            pl.BlockSpec((1, MOD_TN), lambda j: (0, j)),
            pl.BlockSpec(lamv.shape, lambda j: (0, 0)),
        ],
        out_specs=[
            pl.BlockSpec((MOD_ROWS, MOD_TN), lambda j: (0, j)),
            pl.BlockSpec((8, LANES), lambda j: (0, 0)),
        ],
        out_shape=[
            jax.ShapeDtypeStruct((MOD_ROWS, n), F32),
            jax.ShapeDtypeStruct((8, LANES), F32),
        ],
        compiler_params=_params("arbitrary"),
        name="mod",
    )(cc, w_ada, b_ada, lamv)


def _rms_modulate(x, g, shift, scale):
    ms = jnp.mean(x * x, axis=-1, keepdims=True)
    return x * lax.rsqrt(ms + EPS) * g * (1.0 + scale) + shift


def _inproj_kernel(*refs, rope, emit_f32, d_pool, d_attn):
    x_ref, shift_ref, scale_ref, g_ref, w_ref = refs[:5]
    pos = 5
    if rope:
        cos_ref, sa_ref, sb_ref = refs[pos:pos + 3]
        pos += 3
    u_ref, q_ref, k_ref, v_ref, gt_ref = refs[pos:pos + 5]
    pos += 5
    if emit_f32:
        k32_ref, v32_ref = refs[pos:pos + 2]

    nb, ts, d = x_ref.shape
    rows = nb * ts
    x = x_ref[...].reshape(rows, d)
    hb = _rms_modulate(x, g_ref[...], shift_ref[0], scale_ref[0]).astype(BF16)

    def put(ref, val):
        ref[...] = val.reshape(nb, ts, val.shape[-1])

    def rotate(t):
        if not rope:
            return t
        cos, sa, sb = cos_ref[...], sa_ref[...], sb_ref[...]
        outs = []
        for c in range(t.shape[1] // LANES):
            tc = t[:, c * LANES:(c + 1) * LANES]
            outs.append(tc * cos + pltpu.roll(tc, LANES - HEAD_DIM // 4, 1) * sa
                        + pltpu.roll(tc, HEAD_DIM // 4, 1) * sb)
        return jnp.concatenate(outs, axis=1)

    o0 = 0
    put(u_ref, _dot(hb, w_ref[:, o0:o0 + d_pool]).astype(BF16))
    o0 += d_pool
    q = _dot(hb, w_ref[:, o0:o0 + d_attn])
    put(q_ref, (rotate(q) * (HEAD_DIM ** -0.5)).astype(BF16))
    o0 += d_attn
    k = _dot(hb, w_ref[:, o0:o0 + d_attn])
    if emit_f32:
        put(k32_ref, k)
    put(k_ref, rotate(k).astype(BF16))
    o0 += d_attn
    v = _dot(hb, w_ref[:, o0:o0 + d_attn])
    if emit_f32:
        put(v32_ref, v)
    v_t = v.T.astype(BF16)
    for b in range(nb):
        v_ref[b] = v_t[:, b * ts:(b + 1) * ts]
    o0 += d_attn
    put(gt_ref, _dot(hb, w_ref[:, o0:]).astype(BF16))


def _row_spec(n_rows):
    if n_rows == 1:
        return lambda b, s: (0, 0, 0)
    return lambda b, s: (b, 0, 0)


def _inproj_call(x, shift, scale, g, w_in, rope_tabs, *, nb, ts, emit_f32, d_pool, d_attn):
    bsz, seq, d = x.shape
    d_in = w_in.shape[1]
    d_gate = d_in - d_pool - 3 * d_attn
    rope = rope_tabs is not None
    assert nb == 1 or (shift.shape[0] == 1 and not rope)
    tok = lambda b, s: (b, s, 0)
    in_specs = [
        pl.BlockSpec((nb, ts, d), tok),
        pl.BlockSpec((1, 1, d), _row_spec(shift.shape[0])),
        pl.BlockSpec((1, 1, d), _row_spec(scale.shape[0])),
        _const_spec((1, d)),
        _const_spec(w_in.shape),
    ]
    args = [x, shift, scale, g, w_in]
    if rope:
        in_specs += [pl.BlockSpec((ts, LANES), lambda b, s: (s, 0))] * 3
        args += list(rope_tabs)
    widths = [d_pool, d_attn, d_attn, d_attn, d_gate]
    out_specs = [pl.BlockSpec((nb, ts, w), tok) for w in widths]
    out_shape = [jax.ShapeDtypeStruct((bsz, seq, w), BF16) for w in widths]
    out_specs[3] = pl.BlockSpec((nb, d_attn, ts), lambda b, s: (b, 0, s))
    out_shape[3] = jax.ShapeDtypeStruct((bsz, d_attn, seq), BF16)
    if emit_f32:
        out_specs += [pl.BlockSpec((nb, ts, d_attn), tok)] * 2
        out_shape += [jax.ShapeDtypeStruct((bsz, seq, d_attn), F32)] * 2
    return pl.pallas_call(
        functools.partial(_inproj_kernel, rope=rope, emit_f32=emit_f32, d_pool=d_pool, d_attn=d_attn),
        grid=(bsz // nb, seq // ts),
        in_specs=in_specs,
        out_specs=out_specs,
        out_shape=out_shape,
        compiler_params=_params("parallel", "parallel"),
        name="in_proj_rope" if rope else "in_proj",
    )(*args)


def _attn_kernel(*refs, n_src, tq, out_scale):
    lam_ref, sub_ref, q_ref = refs[:3]
    kv_refs = refs[3:3 + 2 * n_src]
    o_ref = refs[3 + 2 * n_src]
    lam = lam_ref[0:1, 0:1]
    tu = min(tq, QUERY_UNIT)
    lane = lax.broadcasted_iota(jnp.int32, (tu, V_DIM), 1)
    chunks = []
    for i in range(n_src):
        n_keys = kv_refs[2 * i].shape[1]
        step = min(n_keys, KEY_CHUNK)
        chunks += [(i, r, r + step) for r in range(0, n_keys, step)]
    units = [(b, h, slice(j * tu, (j + 1) * tu))
             for b in range(q_ref.shape[0]) for h in range(N_HEADS) for j in range(tq // tu)]

    def head_cols(h):
        return slice(h * V_DIM, (h + 1) * V_DIM)

    def stacked_queries(unit):
        b, h, rows = unit
        q = q_ref[b, rows, head_cols(h)].astype(F32)
        return jnp.concatenate([jnp.where(lane < HEAD_DIM, q, 0.0),
                                jnp.where(lane >= HEAD_DIM, q, 0.0)], axis=0).astype(BF16)

    def score_chunk(unit, qq, chunk):
        i, r0, r1 = chunk
        k = kv_refs[2 * i][unit[0], r0:r1, head_cols(unit[1])].astype(BF16)
        return lax.dot_general(k, qq, (((1,), (1,)), ((), ())), preferred_element_type=F32)

    def value_chunk(unit, e, chunk):
        i, r0, r1 = chunk
        v_t = kv_refs[2 * i + 1][unit[0], head_cols(unit[1]), r0:r1].astype(BF16)
        ones_row = lax.broadcasted_iota(jnp.int32, (BF16_SUBLANES, r1 - r0), 0) == 0
        v_ext = jnp.concatenate([v_t, jnp.where(ones_row, 1.0, 0.0).astype(BF16)], axis=0)
        return _dot(v_ext, e)

    def column_max(scores):
        m = jnp.max(scores[0], axis=0, keepdims=True)
        for s in scores[1:]:
            m = jnp.maximum(m, jnp.max(s, axis=0, keepdims=True))
        return m

    def output_phase(unit, pv):
        b, h, rows = unit
        r = 1.0 / pv[V_DIM:V_DIM + 1, :]
        o_t = pv[:V_DIM, :tu] * r[:, :tu] - pv[:V_DIM, tu:] * (r[:, tu:] * lam)
        ms = jnp.mean(o_t * o_t, axis=0, keepdims=True)
        o = (o_t * lax.rsqrt(ms + EPS)).T
        o_ref[b, rows, head_cols(h)] = (o * sub_ref[...] * out_scale).astype(BF16)

    n_units = len(units)
    scores, exps = {}, {}
    for t in range(n_units + 2):
        u_exp, u_val = t - 1, t - 2
        do_score, do_exp, do_val = t < n_units, 0 <= u_exp < n_units, 0 <= u_val < n_units
        if do_score:
            qq = stacked_queries(units[t])
            scores[t] = []
        if do_exp:
            m = column_max(scores[u_exp])
            exps[u_exp] = []
        pv = None
        for ci, c in enumerate(chunks):
            if do_score:
                scores[t].append(score_chunk(units[t], qq, c))
            if do_exp:
                exps[u_exp].append(jnp.exp(scores[u_exp][ci] - m).astype(BF16))
            if do_val:
                part = value_chunk(units[u_val], exps[u_val][ci], c)
                pv = part if pv is None else pv + part
        if do_exp:
            del scores[u_exp]
        if do_val:
            del exps[u_val]
            output_phase(units[u_val], pv)


def _attn_call(lam, subln, q, kvs, *, nb, tq, out_scale):
    bsz, seq, d_attn = q.shape
    assert d_attn == N_HEADS * V_DIM
    in_specs = [
        _const_spec(lam.shape),
        _const_spec(subln.shape),
        pl.BlockSpec((nb, tq, d_attn), lambda b, i: (b, i, 0)),
    ]
    args = [lam, subln, q]
    for kv in kvs:
        in_specs.append(pl.BlockSpec((nb,) + kv.shape[1:], lambda b, i: (b, 0, 0)))
        args.append(kv)
    return pl.pallas_call(
        functools.partial(_attn_kernel, n_src=len(kvs) // 2, tq=tq, out_scale=out_scale),
        grid=(bsz // nb, seq // tq),
        in_specs=in_specs,
        out_specs=pl.BlockSpec((nb, tq, d_attn), lambda b, i: (b, i, 0)),
        out_shape=jax.ShapeDtypeStruct((bsz, seq, d_attn), BF16),
        compiler_params=_params("parallel", "parallel"),
        name="attn%d" % (len(kvs) // 2),
    )(*args)


def _fill_with_halo(dst_ref, prev_ref, mids, next_ref, tm):
    s = pl.program_id(1)
    last = pl.num_programs(1) - 1
    prev = prev_ref[0].astype(dst_ref.dtype)
    nxt = next_ref[0].astype(dst_ref.dtype)
    for b, mid in enumerate(mids):
        base = b * (tm + 2 * HALO)
        dst_ref[base:base + HALO] = jnp.where(s > 0, prev, jnp.zeros_like(prev))
        dst_ref[base + HALO:base + HALO + tm] = mid
        dst_ref[base + HALO + tm:base + 2 * HALO + tm] = jnp.where(s < last, nxt, jnp.zeros_like(nxt))


def _mix_kernel(x_ref, u_ref, uprev_ref, unext_ref, o_ref, gt_ref, gate1_ref, shift2_ref, scale2_ref,
                n2_ref, wpool_ref, pscale_ref, wpa_ref, wpb_ref, wout_ref, band_ref,
                x1_ref, h2_ref, pad_ref, *, tm, seq):
    nb, _, d = x_ref.shape
    rows = nb * tm
    _fill_with_halo(pad_ref, uprev_ref, [u_ref[b] for b in range(nb)], unext_ref, tm)
    uf = u_ref[...].reshape(rows, u_ref.shape[-1]).astype(F32)
    o = o_ref[...].reshape(rows, o_ref.shape[-1])
    t = pl.program_id(1) * tm + lax.rem(lax.broadcasted_iota(jnp.int32, (rows, 1), 0), tm)

    n_groups = len(POOL_WINDOWS)
    cols_per_group = d // n_groups
    block_starts = [b * (tm + 2 * HALO) + r for b in range(nb) for r in range(0, tm, POOL_BLOCK)]
    mixed, y_b = [], []
    for gi, w in enumerate(POOL_WINDOWS):
        cols = slice(gi * LANES, (gi + 1) * LANES)
        y_b.append(_dot(o, wpb_ref[:, gi * cols_per_group:(gi + 1) * cols_per_group]))
        sums = [_dot(band_ref[gi], pad_ref[r:r + POOL_BLOCK + 2 * HALO, cols]) for r in block_starts]
        acc = sums[0] if len(sums) == 1 else jnp.concatenate(sums, axis=0)
        lo = jnp.maximum(t - w // 2, 0)
        hi = jnp.minimum(t + (w - w // 2), seq)
        cnt = (hi - lo).astype(F32)
        pooled = acc / cnt - uf[:, cols]
        mixed.append(_dot(pooled.astype(BF16), wpool_ref[gi]) * pscale_ref[:, cols])
    mixed = jnp.concatenate(mixed, axis=1).astype(BF16)
    y_b = jnp.concatenate(y_b, axis=1)

    y_a = _dot(mixed, wpa_ref[...])
    g = jax.nn.sigmoid(gt_ref[...].reshape(rows, gt_ref.shape[-1]).astype(F32))
    merged = (g[:, :d] * y_a + g[:, d:] * y_b).astype(BF16)
    x1 = x_ref[...].reshape(rows, d) + gate1_ref[0] * _dot(merged, wout_ref[...])
    x1_ref[...] = x1.reshape(nb, tm, d)
    h2 = _rms_modulate(x1, n2_ref[...], shift2_ref[0], scale2_ref[0]).astype(BF16)
    h2_ref[...] = h2.reshape(nb, tm, d)


def _halo_specs(width, tm, seq, nb=1):
    blocks_per_tile = tm // HALO
    last_block = seq // HALO - 1
    prev = pl.BlockSpec((1, HALO, width), lambda b, s: (b * nb, jnp.maximum(s * blocks_per_tile - 1, 0), 0))
    nxt = pl.BlockSpec((1, HALO, width),
                       lambda b, s: (b * nb, jnp.minimum((s + 1) * blocks_per_tile, last_block), 0))
    return prev, nxt


def _pool_bands():
    t = jnp.arange(POOL_BLOCK, dtype=jnp.int32)[:, None]
    j = jnp.arange(POOL_BLOCK + 2 * HALO, dtype=jnp.int32)[None, :]
    rel = j - HALO - t
    return jnp.stack([((rel >= -(w // 2)) & (rel < w - w // 2)).astype(BF16) for w in POOL_WINDOWS])


def _mix_call(x, u, o, gates, gate1, shift2, scale2, n2, w_pool, pscale, wpa, wpb, wout, *, nb, tm):
    bsz, seq, d = x.shape
    d_pool = u.shape[-1]
    assert tm % POOL_BLOCK == 0 and max(POOL_WINDOWS) // 2 <= HALO
    assert nb == 1 or (tm == seq and gate1.shape[0] == 1)
    bands = _pool_bands()
    tok = lambda b, s: (b, s, 0)
    uprev, unext = _halo_specs(d_pool, tm, seq, nb)
    in_specs = [
        pl.BlockSpec((nb, tm, d), tok),
        pl.BlockSpec((nb, tm, d_pool), tok), uprev, unext,
        pl.BlockSpec((nb, tm, o.shape[-1]), tok),
        pl.BlockSpec((nb, tm, gates.shape[-1]), tok),
        pl.BlockSpec((1, 1, d), _row_spec(gate1.shape[0])),
        pl.BlockSpec((1, 1, d), _row_spec(shift2.shape[0])),
        pl.BlockSpec((1, 1, d), _row_spec(scale2.shape[0])),
        _const_spec(n2.shape), _const_spec(w_pool.shape), _const_spec(pscale.shape),
        _const_spec(wpa.shape), _const_spec(wpb.shape), _const_spec(wout.shape), _const_spec(bands.shape),
    ]
    return pl.pallas_call(
        functools.partial(_mix_kernel, tm=tm, seq=seq),
        grid=(bsz // nb, seq // tm),
        in_specs=in_specs,
        out_specs=[pl.BlockSpec((nb, tm, d), tok), pl.BlockSpec((nb, tm, d), tok)],
        out_shape=[jax.ShapeDtypeStruct((bsz, seq, d), F32), jax.ShapeDtypeStruct((bsz, seq, d), BF16)],
        scratch_shapes=[pltpu.VMEM((nb * (tm + 2 * HALO), d_pool), BF16)],
        compiler_params=_params("parallel", "parallel"),
        name="mix",
    )(x, u, u, u, o, gates, gate1, shift2, scale2, n2, w_pool, pscale, wpa, wpb, wout, bands)


def _ffn_kernel(x1_ref, h2_ref, hprev_ref, hnext_ref, win_ref, cw_ref, cb_ref, wout_ref, gate2_ref, fnw_ref,
                y_ref, hext_ref, act_ref, *, tm, d_ff):
    nb, _, d = x1_ref.shape
    _fill_with_halo(hext_ref, hprev_ref, [h2_ref[b] for b in range(nb)], hnext_ref, tm)
    he = hext_ref[...]
    h2 = h2_ref[...].reshape(nb * tm, d)

    def tile_rows(t):
        parts = [t[b * (tm + 2 * HALO) + HALO:b * (tm + 2 * HALO) + HALO + tm] for b in range(nb)]
        return parts[0] if nb == 1 else jnp.concatenate(parts, axis=0)

    for c in range(d_ff // FFN_CHUNK):
        cols = slice(c * FFN_CHUNK, (c + 1) * FFN_CHUNK)
        a_ext = _dot(he, win_ref[:, cols])
        up = _dot(h2, win_ref[:, d_ff + c * FFN_CHUNK:d_ff + (c + 1) * FFN_CHUNK])
        rows = a_ext.shape[0]
        conv = (tile_rows(pltpu.roll(a_ext, 1, 0)) * cw_ref[0:1, cols]
                + tile_rows(a_ext) * cw_ref[1:2, cols]
                + tile_rows(pltpu.roll(a_ext, rows - 1, 0)) * cw_ref[2:3, cols]
                + cb_ref[:, cols])
        act_ref[:, cols] = (conv * jax.nn.sigmoid(conv) * up).astype(BF16)
    x2 = x1_ref[...].reshape(nb * tm, d) + gate2_ref[0] * _dot(act_ref[...], wout_ref[...])
    ms = jnp.mean(x2 * x2, axis=-1, keepdims=True)
    y_ref[...] = (x2 * lax.rsqrt(ms + EPS) * fnw_ref[...]).reshape(nb, tm, d)


def _ffn_call(x1, h2, w_ffn_in, conv_w, conv_b, w_ffn_out, gate2, fnw, *, nb, tm):
    bsz, seq, d = x1.shape
    d_ff = w_ffn_out.shape[0]
    assert nb == 1 or (tm == seq and gate2.shape[0] == 1)
    tok = lambda b, s: (b, s, 0)
    hprev, hnext = _halo_specs(d, tm, seq, nb)
    in_specs = [
        pl.BlockSpec((nb, tm, d), tok),
        pl.BlockSpec((nb, tm, d), tok), hprev, hnext,
        _const_spec(w_ffn_in.shape), _const_spec(conv_w.shape), _const_spec(conv_b.shape),
        _const_spec(w_ffn_out.shape),
        pl.BlockSpec((1, 1, d), _row_spec(gate2.shape[0])),
        _const_spec(fnw.shape),
    ]
    return pl.pallas_call(
        functools.partial(_ffn_kernel, tm=tm, d_ff=d_ff),
        grid=(bsz // nb, seq // tm),
        in_specs=in_specs,
        out_specs=pl.BlockSpec((nb, tm, d), tok),
        out_shape=jax.ShapeDtypeStruct((bsz, seq, d), F32),
        scratch_shapes=[pltpu.VMEM((nb * (tm + 2 * HALO), d), BF16), pltpu.VMEM((nb * tm, d_ff), BF16)],
        compiler_params=_params("parallel", "parallel"),
        name="ffn",
    )(x1, h2, h2, h2, w_ffn_in, conv_w, conv_b, w_ffn_out, gate2, fnw)


def _rope_tables(n_tok):
    rows = n_tok // GRID_W
    row = jnp.repeat(jnp.arange(rows, dtype=F32), GRID_W)
    col = jnp.tile(jnp.arange(GRID_W, dtype=F32), rows)
    n_freq = HEAD_DIM // 4
    inv = ROPE_BASE ** (-jnp.arange(n_freq, dtype=F32) / n_freq)
    ang_row = row[:, None] * inv
    ang_col = col[:, None] * inv
    zeros = jnp.zeros_like(ang_row)
    cos_row, sin_row = jnp.cos(ang_row), jnp.sin(ang_row)
    cos_col, sin_col = jnp.cos(ang_col), jnp.sin(ang_col)
    cos = jnp.concatenate([cos_row, cos_row, cos_col, cos_col], axis=1)
    sa = jnp.concatenate([-sin_row, zeros, -sin_col, zeros], axis=1)
    sb = jnp.concatenate([zeros, sin_row, zeros, sin_col], axis=1)
    rep = LANES // HEAD_DIM
    return tuple(jnp.tile(t, (1, rep)) for t in (cos, sa, sb))


def _stream(x, mods, w, lam, ctx_kv, *, lam_init, final_norm_w):
    shift1, scale1, gate1, shift2, scale2, gate2 = mods
    seq = x.shape[1]
    tm = min(seq, TOKEN_TILE)
    tq = min(seq, QUERY_TILE)
    is_ctx = ctx_kv is None
    nb = TOKEN_TILE // tm if (is_ctx and tm == seq) else 1
    d_pool = w["w_proj_a"].shape[0]
    d_attn = w["w_proj_b"].shape[0]
    rope_tabs = None if is_ctx else _rope_tables(seq)
    outs = _inproj_call(x, shift1, scale1, w["norm1_w"], w["w_in"], rope_tabs,
                        nb=nb, ts=tm, emit_f32=is_ctx, d_pool=d_pool, d_attn=d_attn)
    u, q, k, v, gates = outs[:5]
    kvs = [k, v] if is_ctx else [ctx_kv[0], ctx_kv[1], k, v]
    o = _attn_call(lam, w["subln_w"], q, kvs, nb=1, tq=tq, out_scale=1.0 - lam_init)
    x1, h2 = _mix_call(x, u, o, gates, gate1, shift2, scale2, w["norm2_w"], w["w_pool"], w["pool_scale"],
                       w["w_proj_a"], w["w_proj_b"], w["w_out"], nb=nb, tm=tm)
    y = _ffn_call(x1, h2, w["w_ffn_in"], w["ffn_conv_w"], w["ffn_conv_b"], w["w_ffn_out"], gate2,
                  final_norm_w, nb=nb, tm=tm)
    return y, outs[5:]


def kernel(x_prompt, x_sample, cache_k, cache_v, c, c_ctx, w_ada, b_ada, w_in, w_pool, pool_scale, w_proj_a, w_proj_b, w_out, lam_q1, lam_k1, lam_q2, lam_k2, subln_w, norm1_w, norm2_w, w_ffn_in, ffn_conv_w, ffn_conv_b, w_ffn_out, final_norm_w):
    assert w_ada.shape[0] == 1, "single trunk layer"
    bsz, seq, d = x_prompt.shape
    dec_b, dec_seq, _ = x_sample.shape
    assert 1 + dec_b <= MOD_ROWS
    lam_init = 0.8 - 0.6 * math.exp(-0.3 * 0)

    cc = jnp.zeros((MOD_ROWS, d), F32).at[0].set(c_ctx).at[1:1 + dec_b].set(c)
    lamv = jnp.concatenate([lam_q1, lam_k1, lam_q2, lam_k2], axis=0)
    mod, lam = _mod_call(cc, w_ada[0], b_ada, lamv, lam_init)
    mod = mod.reshape(MOD_ROWS, 6, 1, d)
    mods_ctx = [mod[0:1, j] for j in range(6)]
    mods_lat = [mod[1:1 + dec_b, j] for j in range(6)]

    w = dict(
        w_in=w_in[0].astype(BF16), w_pool=w_pool[0].astype(BF16), pool_scale=pool_scale,
        w_proj_a=w_proj_a[0].astype(BF16), w_proj_b=w_proj_b[0].astype(BF16), w_out=w_out[0].astype(BF16),
        subln_w=subln_w, norm1_w=norm1_w, norm2_w=norm2_w,
        w_ffn_in=w_ffn_in[0].astype(BF16), ffn_conv_w=ffn_conv_w[0], ffn_conv_b=ffn_conv_b,
        w_ffn_out=w_ffn_out[0].astype(BF16),
    )
    fnw = final_norm_w.reshape(1, d)

    y_prompt, (k32, v32) = _stream(x_prompt, mods_ctx, w, lam, None, lam_init=lam_init, final_norm_w=fnw)
    d_attn = k32.shape[-1]
    ctx_kv = (cache_k[:, 0].reshape(dec_b, -1, d_attn),
              jnp.swapaxes(cache_v[:, 0].reshape(dec_b, -1, d_attn), 1, 2))
    y_sample, _ = _stream(x_sample, mods_lat, w, lam, ctx_kv, lam_init=lam_init, final_norm_w=fnw)

    new_cache_k = k32.reshape(bsz, 1, seq, N_HEADS, 2, HEAD_DIM)
    new_cache_v = v32.reshape(bsz, 1, seq, N_HEADS, V_DIM)
    return (y_prompt, y_sample, new_cache_k, new_cache_v)
```

```python
import functools
import math

import jax
import jax.numpy as jnp
from jax import lax
from jax.experimental import pallas as pl
from jax.experimental.pallas import tpu as pltpu

F32 = jnp.float32
BF16 = jnp.bfloat16

GRID_W = 64
N_HEADS = 8
HEAD_DIM = 64
V_DIM = 2 * HEAD_DIM
POOL_WINDOWS = (2, 4, 8, 16)
ROPE_BASE = 10000.0
EPS = 1e-6

LANES = 128
BF16_SUBLANES = 16
VMEM_LIMIT_BYTES = 56 * 1024 * 1024

MOD_ROWS = 16
MOD_TN = 1536
HALO = BF16_SUBLANES
FFN_CHUNK = 256
TOKEN_TILE = 512
QUERY_TILE = 1024
QUERY_UNIT = 256
KEY_CHUNK = 256
POOL_BLOCK = 256
LOOP_STEPS = 6


def _dot(a, b):
    return jnp.dot(a, b, preferred_element_type=F32)


def _const_spec(shape):
    zeros = (0,) * len(shape)
    return pl.BlockSpec(shape, lambda *_: zeros, pipeline_mode=pl.Buffered(1))


def _params(*sem):
    return pltpu.CompilerParams(dimension_semantics=sem, vmem_limit_bytes=VMEM_LIMIT_BYTES)


def _mod_kernel(cc_ref, w_ref, b_ref, lamv_ref, mod_ref, lam_ref, *, lam_init):
    c = cc_ref[...]
    sc = (c * jax.nn.sigmoid(c)).astype(BF16)
    mod_ref[...] = _dot(sc, w_ref[...].astype(BF16)) + b_ref[...]
    lv = lamv_ref[...]
    p1 = jnp.sum(lv[0:1] * lv[1:2], axis=-1, keepdims=True)
    p2 = jnp.sum(lv[2:3] * lv[3:4], axis=-1, keepdims=True)
    lam = jnp.exp(p1) - jnp.exp(p2) + lam_init
    lam_ref[...] = jnp.broadcast_to(lam, lam_ref.shape)


def _mod_call(cc, w_ada, b_ada, lamv, lam_init):
    d, n = w_ada.shape
    return pl.pallas_call(
        functools.partial(_mod_kernel, lam_init=lam_init),
        grid=(n // MOD_TN,),
        in_specs=[
            pl.BlockSpec((MOD_ROWS, d), lambda j: (0, 0)),
            pl.BlockSpec((d, MOD_TN), lambda j: (0, j)),
            pl.BlockSpec((1, MOD_TN), lambda j: (0, j)),
            pl.BlockSpec(lamv.shape, lambda j: (0, 0)),
        ],
        out_specs=[
            pl.BlockSpec((MOD_ROWS, MOD_TN), lambda j: (0, j)),
            pl.BlockSpec((8, LANES), lambda j: (0, 0)),
        ],
        out_shape=[
            jax.ShapeDtypeStruct((MOD_ROWS, n), F32),
            jax.ShapeDtypeStruct((8, LANES), F32),
        ],
        compiler_params=_params("arbitrary"),
        name="mod",
    )(cc, w_ada, b_ada, lamv)


def _rms_modulate(x, g, shift, scale):
    ms = jnp.mean(x * x, axis=-1, keepdims=True)
    return x * lax.rsqrt(ms + EPS) * g * (1.0 + scale) + shift


def _inproj_kernel(*refs, rope, emit_f32, d_pool, d_attn):
    x_ref, shift_ref, scale_ref, g_ref, w_ref = refs[:5]
    pos = 5
    if rope:
        cos_ref, sa_ref, sb_ref = refs[pos:pos + 3]
        pos += 3
    u_ref, q_ref, k_ref, v_ref, gt_ref = refs[pos:pos + 5]
    pos += 5
    if emit_f32:
        k32_ref, v32_ref = refs[pos:pos + 2]

    nb, ts, d = x_ref.shape
    rows = nb * ts
    x = x_ref[...].reshape(rows, d)
    hb = _rms_modulate(x, g_ref[...], shift_ref[0], scale_ref[0]).astype(BF16)

    def put(ref, val):
        ref[...] = val.reshape(nb, ts, val.shape[-1])

    def rotate(t):
        if not rope:
            return t
        cos, sa, sb = cos_ref[...], sa_ref[...], sb_ref[...]
        outs = []
        for c in range(t.shape[1] // LANES):
            tc = t[:, c * LANES:(c + 1) * LANES]
            outs.append(tc * cos + pltpu.roll(tc, LANES - HEAD_DIM // 4, 1) * sa
                        + pltpu.roll(tc, HEAD_DIM // 4, 1) * sb)
        return jnp.concatenate(outs, axis=1)

    o0 = 0
    put(u_ref, _dot(hb, w_ref[:, o0:o0 + d_pool]).astype(BF16))
    o0 += d_pool
    q = _dot(hb, w_ref[:, o0:o0 + d_attn])
    put(q_ref, (rotate(q) * (HEAD_DIM ** -0.5)).astype(BF16))
    o0 += d_attn
    k = _dot(hb, w_ref[:, o0:o0 + d_attn])
    if emit_f32:
        put(k32_ref, k)
    put(k_ref, rotate(k).astype(BF16))
    o0 += d_attn
    v = _dot(hb, w_ref[:, o0:o0 + d_attn])
    if emit_f32:
        put(v32_ref, v)
    v_t = v.T.astype(BF16)
    for b in range(nb):
        v_ref[b] = v_t[:, b * ts:(b + 1) * ts]
    o0 += d_attn
    put(gt_ref, _dot(hb, w_ref[:, o0:]).astype(BF16))


def _row_spec(n_rows):
    if n_rows == 1:
        return lambda b, s: (0, 0, 0)
    return lambda b, s: (b, 0, 0)


def _inproj_call(x, shift, scale, g, w_in, rope_tabs, *, nb, ts, emit_f32, d_pool, d_attn):
    bsz, seq, d = x.shape
    d_in = w_in.shape[1]
    d_gate = d_in - d_pool - 3 * d_attn
    rope = rope_tabs is not None
    assert nb == 1 or (shift.shape[0] == 1 and not rope)
    tok = lambda b, s: (b, s, 0)
    in_specs = [
        pl.BlockSpec((nb, ts, d), tok),
        pl.BlockSpec((1, 1, d), _row_spec(shift.shape[0])),
        pl.BlockSpec((1, 1, d), _row_spec(scale.shape[0])),
        _const_spec((1, d)),
        _const_spec(w_in.shape),
    ]
    args = [x, shift, scale, g, w_in]
    if rope:
        in_specs += [pl.BlockSpec((ts, LANES), lambda b, s: (s, 0))] * 3
        args += list(rope_tabs)
    widths = [d_pool, d_attn, d_attn, d_attn, d_gate]
    out_specs = [pl.BlockSpec((nb, ts, w), tok) for w in widths]
    out_shape = [jax.ShapeDtypeStruct((bsz, seq, w), BF16) for w in widths]
    out_specs[3] = pl.BlockSpec((nb, d_attn, ts), lambda b, s: (b, 0, s))
    out_shape[3] = jax.ShapeDtypeStruct((bsz, d_attn, seq), BF16)
    if emit_f32:
        out_specs += [pl.BlockSpec((nb, ts, d_attn), tok)] * 2
        out_shape += [jax.ShapeDtypeStruct((bsz, seq, d_attn), F32)] * 2
    return pl.pallas_call(
        functools.partial(_inproj_kernel, rope=rope, emit_f32=emit_f32, d_pool=d_pool, d_attn=d_attn),
        grid=(bsz // nb, seq // ts),
        in_specs=in_specs,
        out_specs=out_specs,
        out_shape=out_shape,
        compiler_params=_params("parallel", "parallel"),
        name="in_proj_rope" if rope else "in_proj",
    )(*args)


def _attn_kernel(*refs, n_src, tq, out_scale):
    lam_ref, sub_ref, q_ref = refs[:3]
    kv_refs = refs[3:3 + 2 * n_src]
    o_ref = refs[3 + 2 * n_src]
    lam = lam_ref[0:1, 0:1]
    tu = min(tq, QUERY_UNIT)
    lane = lax.broadcasted_iota(jnp.int32, (tu, V_DIM), 1)
    chunks = []
    for i in range(n_src):
        n_keys = kv_refs[2 * i].shape[1]
        step = min(n_keys, KEY_CHUNK)
        chunks += [(i, r, r + step) for r in range(0, n_keys, step)]
    units = [(b, h, slice(j * tu, (j + 1) * tu))
             for b in range(q_ref.shape[0]) for h in range(N_HEADS) for j in range(tq // tu)]

    def head_cols(h):
        return slice(h * V_DIM, (h + 1) * V_DIM)

    def stacked_queries(unit):
        b, h, rows = unit
        q = q_ref[b, rows, head_cols(h)].astype(F32)
        return jnp.concatenate([jnp.where(lane < HEAD_DIM, q, 0.0),
                                jnp.where(lane >= HEAD_DIM, q, 0.0)], axis=0).astype(BF16)

    def score_chunk(unit, qq, chunk):
        i, r0, r1 = chunk
        k = kv_refs[2 * i][unit[0], r0:r1, head_cols(unit[1])].astype(BF16)
        return lax.dot_general(k, qq, (((1,), (1,)), ((), ())), preferred_element_type=F32)

    def value_chunk(unit, e, chunk):
        i, r0, r1 = chunk
        v_t = kv_refs[2 * i + 1][unit[0], head_cols(unit[1]), r0:r1].astype(BF16)
        ones_row = lax.broadcasted_iota(jnp.int32, (BF16_SUBLANES, r1 - r0), 0) == 0
        v_ext = jnp.concatenate([v_t, jnp.where(ones_row, 1.0, 0.0).astype(BF16)], axis=0)
        return _dot(v_ext, e)

    def column_max(scores):
        m = jnp.max(scores[0], axis=0, keepdims=True)
        for s in scores[1:]:
            m = jnp.maximum(m, jnp.max(s, axis=0, keepdims=True))
        return m

    def output_phase(unit, pv):
        b, h, rows = unit
        r = 1.0 / pv[V_DIM:V_DIM + 1, :]
        o_t = pv[:V_DIM, :tu] * r[:, :tu] - pv[:V_DIM, tu:] * (r[:, tu:] * lam)
        ms = jnp.mean(o_t * o_t, axis=0, keepdims=True)
        o = (o_t * lax.rsqrt(ms + EPS)).T
        o_ref[b, rows, head_cols(h)] = (o * sub_ref[...] * out_scale).astype(BF16)

    n_units = len(units)
    scores, exps = {}, {}
    for t in range(n_units + 2):
        u_exp, u_val = t - 1, t - 2
        do_score, do_exp, do_val = t < n_units, 0 <= u_exp < n_units, 0 <= u_val < n_units
        if do_score:
            qq = stacked_queries(units[t])
            scores[t] = []
        if do_exp:
            m = column_max(scores[u_exp])
            exps[u_exp] = []
        pv = None
        for ci, c in enumerate(chunks):
            if do_score:
                scores[t].append(score_chunk(units[t], qq, c))
            if do_exp:
                exps[u_exp].append(jnp.exp(scores[u_exp][ci] - m).astype(BF16))
            if do_val:
                part = value_chunk(units[u_val], exps[u_val][ci], c)
                pv = part if pv is None else pv + part
        if do_exp:
            del scores[u_exp]
        if do_val:
            del exps[u_val]
            output_phase(units[u_val], pv)


def _attn_call(lam, subln, q, kvs, *, nb, tq, out_scale):
    bsz, seq, d_attn = q.shape
    assert d_attn == N_HEADS * V_DIM
    in_specs = [
        _const_spec(lam.shape),
        _const_spec(subln.shape),
        pl.BlockSpec((nb, tq, d_attn), lambda b, i: (b, i, 0)),
    ]
    args = [lam, subln, q]
    for kv in kvs:
        in_specs.append(pl.BlockSpec((nb,) + kv.shape[1:], lambda b, i: (b, 0, 0)))
        args.append(kv)
    return pl.pallas_call(
        functools.partial(_attn_kernel, n_src=len(kvs) // 2, tq=tq, out_scale=out_scale),
        grid=(bsz // nb, seq // tq),
        in_specs=in_specs,
        out_specs=pl.BlockSpec((nb, tq, d_attn), lambda b, i: (b, i, 0)),
        out_shape=jax.ShapeDtypeStruct((bsz, seq, d_attn), BF16),
        compiler_params=_params("parallel", "parallel"),
        name="attn%d" % (len(kvs) // 2),
    )(*args)


def _attn_loop_kernel(*refs, n_src, tq, out_scale):
    lam_ref, sub_ref, q_ref = refs[:3]
    kv_refs = refs[3:3 + 2 * n_src]
    o_ref, s_ref, e_ref, m_ref = refs[3 + 2 * n_src:]
    lam = lam_ref[0:1, 0:1]
    tu = min(tq, QUERY_UNIT)
    n_sub = tq // tu
    n_units = N_HEADS * n_sub
    assert n_units % 2 == 0 and n_sub & (n_sub - 1) == 0
    lane = lax.broadcasted_iota(jnp.int32, (tu, V_DIM), 1)
    chunks = []
    for i in range(n_src):
        n_keys = kv_refs[2 * i].shape[1]
        offset = sum(kv_refs[2 * j].shape[1] for j in range(i))
        step = min(n_keys, KEY_CHUNK)
        chunks += [(i, r, r + step, offset + r) for r in range(0, n_keys, step)]

    def unit_slices(u):
        if isinstance(u, int):
            h, j = divmod(u, n_sub)
            return slice(j * tu, (j + 1) * tu), slice(h * V_DIM, (h + 1) * V_DIM)
        h = lax.shift_right_logical(u, jnp.int32(n_sub.bit_length() - 1))
        j = lax.bitwise_and(u, jnp.int32(n_sub - 1))
        return pl.ds(pl.multiple_of(j * tu, tu), tu), pl.ds(pl.multiple_of(h * V_DIM, V_DIM), V_DIM)

    def stacked_queries(rows, cols):
        q = q_ref[0, rows, cols].astype(F32)
        return jnp.concatenate([jnp.where(lane < HEAD_DIM, q, 0.0),
                                jnp.where(lane >= HEAD_DIM, q, 0.0)], axis=0).astype(BF16)

    def pipeline_step(u_score, u_exp, u_val, slot):
        if u_score is not None:
            rows_s, cols_s = unit_slices(u_score)
            qq = stacked_queries(rows_s, cols_s)
            m_new = None
        if u_exp is not None:
            m = m_ref[1 - slot, 0:1, :]
        if u_val is not None:
            rows_v, cols_v = unit_slices(u_val)
            pv = None
        for i, r0, r1, s0 in chunks:
            srows = slice(s0, s0 + r1 - r0)
            if u_score is not None:
                k = kv_refs[2 * i][0, r0:r1, cols_s].astype(BF16)
                s = lax.dot_general(k, qq, (((1,), (1,)), ((), ())), preferred_element_type=F32)
                s_ref[slot, srows, :] = s
                s_max = jnp.max(s, axis=0, keepdims=True)
                m_new = s_max if m_new is None else jnp.maximum(m_new, s_max)
            if u_exp is not None:
                e_ref[1 - slot, srows, :] = jnp.exp(s_ref[1 - slot, srows, :] - m).astype(BF16)
            if u_val is not None:
                v_t = kv_refs[2 * i + 1][0, cols_v, r0:r1].astype(BF16)
                ones_row = lax.broadcasted_iota(jnp.int32, (BF16_SUBLANES, r1 - r0), 0) == 0
                v_ext = jnp.concatenate([v_t, jnp.where(ones_row, 1.0, 0.0).astype(BF16)], axis=0)
                part = _dot(v_ext, e_ref[slot, srows, :])
                pv = part if pv is None else pv + part
        if u_score is not None:
            m_ref[slot, 0:1, :] = m_new
        if u_val is not None:
            r = 1.0 / pv[V_DIM:V_DIM + 1, :]
            o_t = pv[:V_DIM, :tu] * r[:, :tu] - pv[:V_DIM, tu:] * (r[:, tu:] * lam)
            ms = jnp.mean(o_t * o_t, axis=0, keepdims=True)
            o = (o_t * lax.rsqrt(ms + EPS)).T
            o_ref[0, rows_v, cols_v] = (o * sub_ref[...] * out_scale).astype(BF16)

    pipeline_step(0, None, None, 0)
    pipeline_step(1, 0, None, 1)

    steady = n_units - 2
    assert steady % LOOP_STEPS == 0 and LOOP_STEPS % 2 == 0

    def steady_steps(i, carry):
        t0 = 2 + LOOP_STEPS * i
        for j in range(LOOP_STEPS):
            t = t0 + j
            pipeline_step(t, t - 1, t - 2, j % 2)
        return carry

    lax.fori_loop(0, steady // LOOP_STEPS, steady_steps, 0)
    pipeline_step(None, n_units - 1, n_units - 2, 0)
    pipeline_step(None, None, n_units - 1, 1)


def _attn_loop_call(lam, subln, q, kvs, *, tq, out_scale):
    bsz, seq, d_attn = q.shape
    assert d_attn == N_HEADS * V_DIM
    in_specs = [
        _const_spec(lam.shape),
        _const_spec(subln.shape),
        pl.BlockSpec((1, tq, d_attn), lambda b, i: (b, i, 0)),
    ]
    args = [lam, subln, q]
    for kv in kvs:
        in_specs.append(pl.BlockSpec((1,) + kv.shape[1:], lambda b, i: (b, 0, 0)))
        args.append(kv)
    n_keys = sum(kv.shape[1] for kv in kvs[0::2])
    width = 2 * min(tq, QUERY_UNIT)
    return pl.pallas_call(
        functools.partial(_attn_loop_kernel, n_src=len(kvs) // 2, tq=tq, out_scale=out_scale),
        grid=(bsz, seq // tq),
        in_specs=in_specs,
        out_specs=pl.BlockSpec((1, tq, d_attn), lambda b, i: (b, i, 0)),
        out_shape=jax.ShapeDtypeStruct((bsz, seq, d_attn), BF16),
        scratch_shapes=[pltpu.VMEM((2, n_keys, width), F32), pltpu.VMEM((2, n_keys, width), BF16),
                        pltpu.VMEM((2, 8, width), F32)],
        compiler_params=_params("parallel", "parallel"),
        name="attn%d" % (len(kvs) // 2),
    )(*args)


def _fill_with_halo(dst_ref, prev_ref, mids, next_ref, tm):
    s = pl.program_id(1)
    last = pl.num_programs(1) - 1
    prev = prev_ref[0].astype(dst_ref.dtype)
    nxt = next_ref[0].astype(dst_ref.dtype)
    for b, mid in enumerate(mids):
        base = b * (tm + 2 * HALO)
        dst_ref[base:base + HALO] = jnp.where(s > 0, prev, jnp.zeros_like(prev))
        dst_ref[base + HALO:base + HALO + tm] = mid
        dst_ref[base + HALO + tm:base + 2 * HALO + tm] = jnp.where(s < last, nxt, jnp.zeros_like(nxt))


def _mix_kernel(x_ref, u_ref, uprev_ref, unext_ref, o_ref, gt_ref, gate1_ref, shift2_ref, scale2_ref,
                n2_ref, wpool_ref, pscale_ref, wpa_ref, wpb_ref, wout_ref, band_ref,
                x1_ref, h2_ref, pad_ref, *, tm, seq):
    nb, _, d = x_ref.shape
    rows = nb * tm
    _fill_with_halo(pad_ref, uprev_ref, [u_ref[b] for b in range(nb)], unext_ref, tm)
    uf = u_ref[...].reshape(rows, u_ref.shape[-1]).astype(F32)
    o = o_ref[...].reshape(rows, o_ref.shape[-1])
    t = pl.program_id(1) * tm + lax.rem(lax.broadcasted_iota(jnp.int32, (rows, 1), 0), tm)

    n_groups = len(POOL_WINDOWS)
    cols_per_group = d // n_groups
    block_starts = [b * (tm + 2 * HALO) + r for b in range(nb) for r in range(0, tm, POOL_BLOCK)]
    mixed, y_b = [], []
    for gi, w in enumerate(POOL_WINDOWS):
        cols = slice(gi * LANES, (gi + 1) * LANES)
        y_b.append(_dot(o, wpb_ref[:, gi * cols_per_group:(gi + 1) * cols_per_group]))
        sums = [_dot(band_ref[gi], pad_ref[r:r + POOL_BLOCK + 2 * HALO, cols]) for r in block_starts]
        acc = sums[0] if len(sums) == 1 else jnp.concatenate(sums, axis=0)
        lo = jnp.maximum(t - w // 2, 0)
        hi = jnp.minimum(t + (w - w // 2), seq)
        cnt = (hi - lo).astype(F32)
        pooled = acc / cnt - uf[:, cols]
        mixed.append(_dot(pooled.astype(BF16), wpool_ref[gi]) * pscale_ref[:, cols])
    mixed = jnp.concatenate(mixed, axis=1).astype(BF16)
    y_b = jnp.concatenate(y_b, axis=1)

    y_a = _dot(mixed, wpa_ref[...])
    g = jax.nn.sigmoid(gt_ref[...].reshape(rows, gt_ref.shape[-1]).astype(F32))
    merged = (g[:, :d] * y_a + g[:, d:] * y_b).astype(BF16)
    x1 = x_ref[...].reshape(rows, d) + gate1_ref[0] * _dot(merged, wout_ref[...])
    x1_ref[...] = x1.reshape(nb, tm, d)
    h2 = _rms_modulate(x1, n2_ref[...], shift2_ref[0], scale2_ref[0]).astype(BF16)
    h2_ref[...] = h2.reshape(nb, tm, d)


def _halo_specs(width, tm, seq, nb=1):
    blocks_per_tile = tm // HALO
    last_block = seq // HALO - 1
    prev = pl.BlockSpec((1, HALO, width), lambda b, s: (b * nb, jnp.maximum(s * blocks_per_tile - 1, 0), 0))
    nxt = pl.BlockSpec((1, HALO, width),
                       lambda b, s: (b * nb, jnp.minimum((s + 1) * blocks_per_tile, last_block), 0))
    return prev, nxt


def _pool_bands():
    t = jnp.arange(POOL_BLOCK, dtype=jnp.int32)[:, None]
    j = jnp.arange(POOL_BLOCK + 2 * HALO, dtype=jnp.int32)[None, :]
    rel = j - HALO - t
    return jnp.stack([((rel >= -(w // 2)) & (rel < w - w // 2)).astype(BF16) for w in POOL_WINDOWS])


def _mix_call(x, u, o, gates, gate1, shift2, scale2, n2, w_pool, pscale, wpa, wpb, wout, *, nb, tm):
    bsz, seq, d = x.shape
    d_pool = u.shape[-1]
    assert tm % POOL_BLOCK == 0 and max(POOL_WINDOWS) // 2 <= HALO
    assert nb == 1 or (tm == seq and gate1.shape[0] == 1)
    bands = _pool_bands()
    tok = lambda b, s: (b, s, 0)
    uprev, unext = _halo_specs(d_pool, tm, seq, nb)
    in_specs = [
        pl.BlockSpec((nb, tm, d), tok),
        pl.BlockSpec((nb, tm, d_pool), tok), uprev, unext,
        pl.BlockSpec((nb, tm, o.shape[-1]), tok),
        pl.BlockSpec((nb, tm, gates.shape[-1]), tok),
        pl.BlockSpec((1, 1, d), _row_spec(gate1.shape[0])),
        pl.BlockSpec((1, 1, d), _row_spec(shift2.shape[0])),
        pl.BlockSpec((1, 1, d), _row_spec(scale2.shape[0])),
        _const_spec(n2.shape), _const_spec(w_pool.shape), _const_spec(pscale.shape),
        _const_spec(wpa.shape), _const_spec(wpb.shape), _const_spec(wout.shape), _const_spec(bands.shape),
    ]
    return pl.pallas_call(
        functools.partial(_mix_kernel, tm=tm, seq=seq),
        grid=(bsz // nb, seq // tm),
        in_specs=in_specs,
        out_specs=[pl.BlockSpec((nb, tm, d), tok), pl.BlockSpec((nb, tm, d), tok)],
        out_shape=[jax.ShapeDtypeStruct((bsz, seq, d), F32), jax.ShapeDtypeStruct((bsz, seq, d), BF16)],
        scratch_shapes=[pltpu.VMEM((nb * (tm + 2 * HALO), d_pool), BF16)],
        compiler_params=_params("parallel", "parallel"),
        name="mix",
    )(x, u, u, u, o, gates, gate1, shift2, scale2, n2, w_pool, pscale, wpa, wpb, wout, bands)


def _ffn_kernel(x1_ref, h2_ref, hprev_ref, hnext_ref, win_ref, cw_ref, cb_ref, wout_ref, gate2_ref, fnw_ref,
                y_ref, hext_ref, act_ref, *, tm, d_ff):
    nb, _, d = x1_ref.shape
    _fill_with_halo(hext_ref, hprev_ref, [h2_ref[b] for b in range(nb)], hnext_ref, tm)
    he = hext_ref[...]
    h2 = h2_ref[...].reshape(nb * tm, d)

    def tile_rows(t):
        parts = [t[b * (tm + 2 * HALO) + HALO:b * (tm + 2 * HALO) + HALO + tm] for b in range(nb)]
        return parts[0] if nb == 1 else jnp.concatenate(parts, axis=0)

    for c in range(d_ff // FFN_CHUNK):
        cols = slice(c * FFN_CHUNK, (c + 1) * FFN_CHUNK)
        a_ext = _dot(he, win_ref[:, cols])
        up = _dot(h2, win_ref[:, d_ff + c * FFN_CHUNK:d_ff + (c + 1) * FFN_CHUNK])
        rows = a_ext.shape[0]
        conv = (tile_rows(pltpu.roll(a_ext, 1, 0)) * cw_ref[0:1, cols]
                + tile_rows(a_ext) * cw_ref[1:2, cols]
                + tile_rows(pltpu.roll(a_ext, rows - 1, 0)) * cw_ref[2:3, cols]
                + cb_ref[:, cols])
        act_ref[:, cols] = (conv * jax.nn.sigmoid(conv) * up).astype(BF16)
    x2 = x1_ref[...].reshape(nb * tm, d) + gate2_ref[0] * _dot(act_ref[...], wout_ref[...])
    ms = jnp.mean(x2 * x2, axis=-1, keepdims=True)
    y_ref[...] = (x2 * lax.rsqrt(ms + EPS) * fnw_ref[...]).reshape(nb, tm, d)


def _ffn_call(x1, h2, w_ffn_in, conv_w, conv_b, w_ffn_out, gate2, fnw, *, nb, tm):
    bsz, seq, d = x1.shape
    d_ff = w_ffn_out.shape[0]
    assert nb == 1 or (tm == seq and gate2.shape[0] == 1)
    tok = lambda b, s: (b, s, 0)
    hprev, hnext = _halo_specs(d, tm, seq, nb)
    in_specs = [
        pl.BlockSpec((nb, tm, d), tok),
        pl.BlockSpec((nb, tm, d), tok), hprev, hnext,
        _const_spec(w_ffn_in.shape), _const_spec(conv_w.shape), _const_spec(conv_b.shape),
        _const_spec(w_ffn_out.shape),
        pl.BlockSpec((1, 1, d), _row_spec(gate2.shape[0])),
        _const_spec(fnw.shape),
    ]
    return pl.pallas_call(
        functools.partial(_ffn_kernel, tm=tm, d_ff=d_ff),
        grid=(bsz // nb, seq // tm),
        in_specs=in_specs,
        out_specs=pl.BlockSpec((nb, tm, d), tok),
        out_shape=jax.ShapeDtypeStruct((bsz, seq, d), F32),
        scratch_shapes=[pltpu.VMEM((nb * (tm + 2 * HALO), d), BF16), pltpu.VMEM((nb * tm, d_ff), BF16)],
        compiler_params=_params("parallel", "parallel"),
        name="ffn",
    )(x1, h2, h2, h2, w_ffn_in, conv_w, conv_b, w_ffn_out, gate2, fnw)


def _rope_tables(n_tok):
    rows = n_tok // GRID_W
    row = jnp.repeat(jnp.arange(rows, dtype=F32), GRID_W)
    col = jnp.tile(jnp.arange(GRID_W, dtype=F32), rows)
    n_freq = HEAD_DIM // 4
    inv = ROPE_BASE ** (-jnp.arange(n_freq, dtype=F32) / n_freq)
    ang_row = row[:, None] * inv
    ang_col = col[:, None] * inv
    zeros = jnp.zeros_like(ang_row)
    cos_row, sin_row = jnp.cos(ang_row), jnp.sin(ang_row)
    cos_col, sin_col = jnp.cos(ang_col), jnp.sin(ang_col)
    cos = jnp.concatenate([cos_row, cos_row, cos_col, cos_col], axis=1)
    sa = jnp.concatenate([-sin_row, zeros, -sin_col, zeros], axis=1)
    sb = jnp.concatenate([zeros, sin_row, zeros, sin_col], axis=1)
    rep = LANES // HEAD_DIM
    return tuple(jnp.tile(t, (1, rep)) for t in (cos, sa, sb))


def _stream(x, mods, w, lam, ctx_kv, *, lam_init, final_norm_w):
    shift1, scale1, gate1, shift2, scale2, gate2 = mods
    seq = x.shape[1]
    tm = min(seq, TOKEN_TILE)
    tq = min(seq, QUERY_TILE)
    is_ctx = ctx_kv is None
    nb = TOKEN_TILE // tm if (is_ctx and tm == seq) else 1
    d_pool = w["w_proj_a"].shape[0]
    d_attn = w["w_proj_b"].shape[0]
    rope_tabs = None if is_ctx else _rope_tables(seq)
    outs = _inproj_call(x, shift1, scale1, w["norm1_w"], w["w_in"], rope_tabs,
                        nb=nb, ts=tm, emit_f32=is_ctx, d_pool=d_pool, d_attn=d_attn)
    u, q, k, v, gates = outs[:5]
    kvs = [k, v] if is_ctx else [ctx_kv[0], ctx_kv[1], k, v]
    o = _attn_loop_call(lam, w["subln_w"], q, kvs, tq=tq, out_scale=1.0 - lam_init)
    x1, h2 = _mix_call(x, u, o, gates, gate1, shift2, scale2, w["norm2_w"], w["w_pool"], w["pool_scale"],
                       w["w_proj_a"], w["w_proj_b"], w["w_out"], nb=nb, tm=tm)
    y = _ffn_call(x1, h2, w["w_ffn_in"], w["ffn_conv_w"], w["ffn_conv_b"], w["w_ffn_out"], gate2,
                  final_norm_w, nb=nb, tm=tm)
    return y, outs[5:]


def kernel(x_prompt, x_sample, cache_k, cache_v, c, c_ctx, w_ada, b_ada, w_in, w_pool, pool_scale, w_proj_a, w_proj_b, w_out, lam_q1, lam_k1, lam_q2, lam_k2, subln_w, norm1_w, norm2_w, w_ffn_in, ffn_conv_w, ffn_conv_b, w_ffn_out, final_norm_w):
    assert w_ada.shape[0] == 1, "single trunk layer"
    bsz, seq, d = x_prompt.shape
    dec_b, dec_seq, _ = x_sample.shape
    assert 1 + dec_b <= MOD_ROWS
    lam_init = 0.8 - 0.6 * math.exp(-0.3 * 0)

    cc = jnp.zeros((MOD_ROWS, d), F32).at[0].set(c_ctx).at[1:1 + dec_b].set(c)
    lamv = jnp.concatenate([lam_q1, lam_k1, lam_q2, lam_k2], axis=0)
    mod, lam = _mod_call(cc, w_ada[0], b_ada, lamv, lam_init)
    mod = mod.reshape(MOD_ROWS, 6, 1, d)
    mods_ctx = [mod[0:1, j] for j in range(6)]
    mods_lat = [mod[1:1 + dec_b, j] for j in range(6)]

    w = dict(
        w_in=w_in[0].astype(BF16), w_pool=w_pool[0].astype(BF16), pool_scale=pool_scale,
        w_proj_a=w_proj_a[0].astype(BF16), w_proj_b=w_proj_b[0].astype(BF16), w_out=w_out[0].astype(BF16),
        subln_w=subln_w, norm1_w=norm1_w, norm2_w=norm2_w,
        w_ffn_in=w_ffn_in[0].astype(BF16), ffn_conv_w=ffn_conv_w[0], ffn_conv_b=ffn_conv_b,
        w_ffn_out=w_ffn_out[0].astype(BF16),
    )
    fnw = final_norm_w.reshape(1, d)

    y_prompt, (k32, v32) = _stream(x_prompt, mods_ctx, w, lam, None, lam_init=lam_init, final_norm_w=fnw)
    d_attn = k32.shape[-1]
    ctx_kv = (cache_k[:, 0].reshape(dec_b, -1, d_attn),
              jnp.swapaxes(cache_v[:, 0].reshape(dec_b, -1, d_attn), 1, 2))
    y_sample, _ = _stream(x_sample, mods_lat, w, lam, ctx_kv, lam_init=lam_init, final_norm_w=fnw)

    new_cache_k = k32.reshape(bsz, 1, seq, N_HEADS, 2, HEAD_DIM)
    new_cache_v = v32.reshape(bsz, 1, seq, N_HEADS, V_DIM)
    return (y_prompt, y_sample, new_cache_k, new_cache_v)
```

```python
import functools
import math
from typing import NamedTuple

import jax
import jax.numpy as jnp
from jax import lax
from jax.experimental import pallas as pl
from jax.experimental.pallas import tpu as pltpu

F32 = jnp.float32
BF16 = jnp.bfloat16

GRID_W = 64
N_HEADS = 8
HEAD_DIM = 64
V_DIM = 2 * HEAD_DIM
POOL_WINDOWS = (2, 4, 8, 16)
ROPE_BASE = 10000.0
EPS = 1e-6

LANES = 128
BF16_SUBLANES = 16
VMEM_LIMIT_BYTES = 56 * 1024 * 1024

MOD_ROWS = 16
MOD_TN = 1536
HALO = BF16_SUBLANES
FFN_CHUNK = 256
TOKEN_TILE = 512
QUERY_TILE = 512
QUERY_UNIT = 256
KEY_CHUNK = 256
POOL_BLOCK = 256


def _dot(a, b):
    return jnp.dot(a, b, preferred_element_type=F32)


def _const_spec(shape):
    zeros = (0,) * len(shape)
    return pl.BlockSpec(shape, lambda *_: zeros, pipeline_mode=pl.Buffered(1))


def _params(*sem):
    return pltpu.CompilerParams(dimension_semantics=sem, vmem_limit_bytes=VMEM_LIMIT_BYTES)


def _mod_kernel(cc_ref, w_ref, b_ref, lamv_ref, mod_ref, lam_ref, *, lam_init):
    c = cc_ref[...]
    sc = (c * jax.nn.sigmoid(c)).astype(BF16)
    mod_ref[...] = _dot(sc, w_ref[...].astype(BF16)) + b_ref[...]
    lv = lamv_ref[...]
    p1 = jnp.sum(lv[0:1] * lv[1:2], axis=-1, keepdims=True)
    p2 = jnp.sum(lv[2:3] * lv[3:4], axis=-1, keepdims=True)
    lam = jnp.exp(p1) - jnp.exp(p2) + lam_init
    lam_ref[...] = jnp.broadcast_to(lam, lam_ref.shape)


def _mod_call(cc, w_ada, b_ada, lamv, lam_init):
    d, n = w_ada.shape
    return pl.pallas_call(
        functools.partial(_mod_kernel, lam_init=lam_init),
        grid=(n // MOD_TN,),
        in_specs=[
            pl.BlockSpec((MOD_ROWS, d), lambda j: (0, 0)),
            pl.BlockSpec((d, MOD_TN), lambda j: (0, j)),
            pl.BlockSpec((1, MOD_TN), lambda j: (0, j)),
            pl.BlockSpec(lamv.shape, lambda j: (0, 0)),
        ],
        out_specs=[
            pl.BlockSpec((MOD_ROWS, MOD_TN), lambda j: (0, j)),
            pl.BlockSpec((8, LANES), lambda j: (0, 0)),
        ],
        out_shape=[
            jax.ShapeDtypeStruct((MOD_ROWS, n), F32),
            jax.ShapeDtypeStruct((8, LANES), F32),
        ],
        compiler_params=_params("arbitrary"),
        name="mod",
    )(cc, w_ada, b_ada, lamv)


def _rms_modulate(x, g, shift, scale):
    ms = jnp.mean(x * x, axis=-1, keepdims=True)
    return x * lax.rsqrt(ms + EPS) * g * (1.0 + scale) + shift


def _inproj_kernel(*refs, rope, emit_f32, d_pool, d_attn):
    x_ref, shift_ref, scale_ref, g_ref, w_ref = refs[:5]
    pos = 5
    if rope:
        cos_ref, sa_ref, sb_ref = refs[pos:pos + 3]
        pos += 3
    u_ref, q_ref, k_ref, v_ref, gt_ref = refs[pos:pos + 5]
    pos += 5
    if emit_f32:
        k32_ref, v32_ref = refs[pos:pos + 2]

    nb, ts, d = x_ref.shape
    rows = nb * ts
    x = x_ref[...].reshape(rows, d)
    hb = _rms_modulate(x, g_ref[...], shift_ref[0, 0], scale_ref[0, 0]).astype(BF16)

    def put(ref, val):
        ref[...] = val.reshape(nb, ts, val.shape[-1])

    def rotate(t):
        if not rope:
            return t
        cos, sa, sb = cos_ref[...], sa_ref[...], sb_ref[...]
        outs = []
        for c in range(t.shape[1] // LANES):
            tc = t[:, c * LANES:(c + 1) * LANES]
            outs.append(tc * cos + pltpu.roll(tc, LANES - HEAD_DIM // 4, 1) * sa
                        + pltpu.roll(tc, HEAD_DIM // 4, 1) * sb)
        return jnp.concatenate(outs, axis=1)

    o0 = 0
    put(u_ref, _dot(hb, w_ref[:, o0:o0 + d_pool]).astype(BF16))
    o0 += d_pool
    q = _dot(hb, w_ref[:, o0:o0 + d_attn])
    put(q_ref, (rotate(q) * (HEAD_DIM ** -0.5)).astype(BF16))
    o0 += d_attn
    k = _dot(hb, w_ref[:, o0:o0 + d_attn])
    if emit_f32:
        put(k32_ref, k)
    put(k_ref, rotate(k).astype(BF16))
    o0 += d_attn
    v = _dot(hb, w_ref[:, o0:o0 + d_attn])
    if emit_f32:
        put(v32_ref, v)
    v_t = v.T.astype(BF16)
    for b in range(nb):
        v_ref[b] = v_t[:, b * ts:(b + 1) * ts]
    o0 += d_attn
    put(gt_ref, _dot(hb, w_ref[:, o0:]).astype(BF16))


class _ModRow(NamedTuple):
    array: jax.Array
    kind: int
    row0: int
    per_batch: int

    @property
    def shared(self):
        return self.per_batch == 0

    def spec(self):
        block = (1, 1) + self.array.shape[2:]
        return pl.BlockSpec(block, lambda b, s: (self.row0 + self.per_batch * b, self.kind, 0, 0))


def _inproj_call(x, shift, scale, g, w_in, rope_tabs, *, nb, ts, emit_f32, d_pool, d_attn):
    bsz, seq, d = x.shape
    d_in = w_in.shape[1]
    d_gate = d_in - d_pool - 3 * d_attn
    rope = rope_tabs is not None
    assert nb == 1 or (shift.shared and not rope)
    tok = lambda b, s: (b, s, 0)
    in_specs = [
        pl.BlockSpec((nb, ts, d), tok),
        shift.spec(),
        scale.spec(),
        _const_spec((1, d)),
        _const_spec(w_in.shape),
    ]
    args = [x, shift.array, scale.array, g, w_in]
    if rope:
        in_specs += [pl.BlockSpec((ts, LANES), lambda b, s: (s, 0))] * 3
        args += list(rope_tabs)
    widths = [d_pool, d_attn, d_attn, d_attn, d_gate]
    out_specs = [pl.BlockSpec((nb, ts, w), tok) for w in widths]
    out_shape = [jax.ShapeDtypeStruct((bsz, seq, w), BF16) for w in widths]
    out_specs[3] = pl.BlockSpec((nb, d_attn, ts), lambda b, s: (b, 0, s))
    out_shape[3] = jax.ShapeDtypeStruct((bsz, d_attn, seq), BF16)
    if emit_f32:
        out_specs += [pl.BlockSpec((nb, ts, d_attn), tok)] * 2
        out_shape += [jax.ShapeDtypeStruct((bsz, seq, d_attn), F32)] * 2
    return pl.pallas_call(
        functools.partial(_inproj_kernel, rope=rope, emit_f32=emit_f32, d_pool=d_pool, d_attn=d_attn),
        grid=(bsz // nb, seq // ts),
        in_specs=in_specs,
        out_specs=out_specs,
        out_shape=out_shape,
        compiler_params=_params("parallel", "parallel"),
        name="in_proj_rope" if rope else "in_proj",
    )(*args)


def _attn_kernel(*refs, n_src, tq, out_scale):
    lam_ref, sub_ref, q_ref = refs[:3]
    kv_refs = refs[3:3 + 2 * n_src]
    o_ref = refs[3 + 2 * n_src]
    lam = lam_ref[0:1, 0:1]
    tu = min(tq, QUERY_UNIT)
    lane = lax.broadcasted_iota(jnp.int32, (tu, V_DIM), 1)
    chunks = []
    for i in range(n_src):
        n_keys = kv_refs[2 * i].shape[1]
        step = min(n_keys, KEY_CHUNK)
        chunks += [(i, r, r + step) for r in range(0, n_keys, step)]
    units = [(b, h, slice(j * tu, (j + 1) * tu))
             for b in range(q_ref.shape[0]) for h in range(N_HEADS) for j in range(tq // tu)]

    def head_cols(h):
        return slice(h * V_DIM, (h + 1) * V_DIM)

    def stacked_queries(unit):
        b, h, rows = unit
        q = q_ref[b, rows, head_cols(h)].astype(F32)
        return jnp.concatenate([jnp.where(lane < HEAD_DIM, q, 0.0),
                                jnp.where(lane >= HEAD_DIM, q, 0.0)], axis=0).astype(BF16)

    def score_chunk(unit, qq, chunk):
        i, r0, r1 = chunk
        k = kv_refs[2 * i][unit[0], r0:r1, head_cols(unit[1])].astype(BF16)
        return lax.dot_general(k, qq, (((1,), (1,)), ((), ())), preferred_element_type=F32)

    def value_chunk(unit, e, chunk):
        i, r0, r1 = chunk
        v_t = kv_refs[2 * i + 1][unit[0], head_cols(unit[1]), r0:r1].astype(BF16)
        ones_row = lax.broadcasted_iota(jnp.int32, (BF16_SUBLANES, r1 - r0), 0) == 0
        v_ext = jnp.concatenate([v_t, jnp.where(ones_row, 1.0, 0.0).astype(BF16)], axis=0)
        return _dot(v_ext, e)

    def column_max(scores):
        m = jnp.max(scores[0], axis=0, keepdims=True)
        for s in scores[1:]:
            m = jnp.maximum(m, jnp.max(s, axis=0, keepdims=True))
        return m

    def output_phase(unit, pv):
        b, h, rows = unit
        r = 1.0 / pv[V_DIM:V_DIM + 1, :]
        o_t = pv[:V_DIM, :tu] * r[:, :tu] - pv[:V_DIM, tu:] * (r[:, tu:] * lam)
        ms = jnp.mean(o_t * o_t, axis=0, keepdims=True)
        o = (o_t * lax.rsqrt(ms + EPS)).T
        o_ref[b, rows, head_cols(h)] = (o * sub_ref[...] * out_scale).astype(BF16)

    n_units = len(units)
    scores, exps = {}, {}
    for t in range(n_units + 2):
        u_exp, u_val = t - 1, t - 2
        do_score, do_exp, do_val = t < n_units, 0 <= u_exp < n_units, 0 <= u_val < n_units
        if do_score:
            qq = stacked_queries(units[t])
            scores[t] = []
        if do_exp:
            m = column_max(scores[u_exp])
            exps[u_exp] = []
        pv = None
        for ci, c in enumerate(chunks):
            if do_score:
                scores[t].append(score_chunk(units[t], qq, c))
            if do_exp:
                exps[u_exp].append(jnp.exp(scores[u_exp][ci] - m).astype(BF16))
            if do_val:
                part = value_chunk(units[u_val], exps[u_val][ci], c)
                pv = part if pv is None else pv + part
        if do_exp:
            del scores[u_exp]
        if do_val:
            del exps[u_val]
            output_phase(units[u_val], pv)


def _attn_call(lam, subln, q, kvs, *, nb, tq, out_scale):
    bsz, seq, d_attn = q.shape
    assert d_attn == N_HEADS * V_DIM
    in_specs = [
        _const_spec(lam.shape),
        _const_spec(subln.shape),
        pl.BlockSpec((nb, tq, d_attn), lambda b, i: (b, i, 0)),
    ]
    args = [lam, subln, q]
    for kv in kvs:
        in_specs.append(pl.BlockSpec((nb,) + kv.shape[1:], lambda b, i: (b, 0, 0)))
        args.append(kv)
    return pl.pallas_call(
        functools.partial(_attn_kernel, n_src=len(kvs) // 2, tq=tq, out_scale=out_scale),
        grid=(bsz // nb, seq // tq),
        in_specs=in_specs,
        out_specs=pl.BlockSpec((nb, tq, d_attn), lambda b, i: (b, i, 0)),
        out_shape=jax.ShapeDtypeStruct((bsz, seq, d_attn), BF16),
        compiler_params=_params("parallel", "parallel"),
        name="attn%d" % (len(kvs) // 2),
    )(*args)


def _fill_with_halo(dst_ref, prev_ref, mids, next_ref, tm):
    s = pl.program_id(1)
    last = pl.num_programs(1) - 1
    prev = prev_ref[0].astype(dst_ref.dtype)
    nxt = next_ref[0].astype(dst_ref.dtype)
    for b, mid in enumerate(mids):
        base = b * (tm + 2 * HALO)
        dst_ref[base:base + HALO] = jnp.where(s > 0, prev, jnp.zeros_like(prev))
        dst_ref[base + HALO:base + HALO + tm] = mid
        dst_ref[base + HALO + tm:base + 2 * HALO + tm] = jnp.where(s < last, nxt, jnp.zeros_like(nxt))


def _mix_kernel(x_ref, u_ref, uprev_ref, unext_ref, o_ref, gt_ref, gate1_ref, shift2_ref, scale2_ref,
                n2_ref, wpool_ref, pscale_ref, wpa_ref, wpb_ref, wout_ref, band_ref,
                x1_ref, h2_ref, pad_ref, *, tm, seq):
    nb, _, d = x_ref.shape
    rows = nb * tm
    _fill_with_halo(pad_ref, uprev_ref, [u_ref[b] for b in range(nb)], unext_ref, tm)
    uf = u_ref[...].reshape(rows, u_ref.shape[-1]).astype(F32)
    o = o_ref[...].reshape(rows, o_ref.shape[-1])
    t = pl.program_id(1) * tm + lax.rem(lax.broadcasted_iota(jnp.int32, (rows, 1), 0), tm)

    n_groups = len(POOL_WINDOWS)
    cols_per_group = d // n_groups
    block_starts = [b * (tm + 2 * HALO) + r for b in range(nb) for r in range(0, tm, POOL_BLOCK)]
    mixed, y_b = [], []
    for gi, w in enumerate(POOL_WINDOWS):
        cols = slice(gi * LANES, (gi + 1) * LANES)
        y_b.append(_dot(o, wpb_ref[:, gi * cols_per_group:(gi + 1) * cols_per_group]))
        sums = [_dot(band_ref[gi], pad_ref[r:r + POOL_BLOCK + 2 * HALO, cols]) for r in block_starts]
        acc = sums[0] if len(sums) == 1 else jnp.concatenate(sums, axis=0)
        lo = jnp.maximum(t - w // 2, 0)
        hi = jnp.minimum(t + (w - w // 2), seq)
        cnt = (hi - lo).astype(F32)
        pooled = acc / cnt - uf[:, cols]
        mixed.append(_dot(pooled.astype(BF16), wpool_ref[gi]) * pscale_ref[:, cols])
    mixed = jnp.concatenate(mixed, axis=1).astype(BF16)
    y_b = jnp.concatenate(y_b, axis=1)

    y_a = _dot(mixed, wpa_ref[...])
    g = jax.nn.sigmoid(gt_ref[...].reshape(rows, gt_ref.shape[-1]).astype(F32))
    merged = (g[:, :d] * y_a + g[:, d:] * y_b).astype(BF16)
    x1 = x_ref[...].reshape(rows, d) + gate1_ref[0, 0] * _dot(merged, wout_ref[...])
    x1_ref[...] = x1.reshape(nb, tm, d)
    h2 = _rms_modulate(x1, n2_ref[...], shift2_ref[0, 0], scale2_ref[0, 0]).astype(BF16)
    h2_ref[...] = h2.reshape(nb, tm, d)


def _halo_specs(width, tm, seq, nb=1):
    blocks_per_tile = tm // HALO
    last_block = seq // HALO - 1
    prev = pl.BlockSpec((1, HALO, width), lambda b, s: (b * nb, jnp.maximum(s * blocks_per_tile - 1, 0), 0))
    nxt = pl.BlockSpec((1, HALO, width),
                       lambda b, s: (b * nb, jnp.minimum((s + 1) * blocks_per_tile, last_block), 0))
    return prev, nxt


def _pool_bands():
    t = jnp.arange(POOL_BLOCK, dtype=jnp.int32)[:, None]
    j = jnp.arange(POOL_BLOCK + 2 * HALO, dtype=jnp.int32)[None, :]
    rel = j - HALO - t
    return jnp.stack([((rel >= -(w // 2)) & (rel < w - w // 2)).astype(BF16) for w in POOL_WINDOWS])


def _mix_call(x, u, o, gates, gate1, shift2, scale2, n2, w_pool, pscale, wpa, wpb, wout, *, nb, tm):
    bsz, seq, d = x.shape
    d_pool = u.shape[-1]
    assert tm % POOL_BLOCK == 0 and max(POOL_WINDOWS) // 2 <= HALO
    assert nb == 1 or (tm == seq and gate1.shared)
    bands = _pool_bands()
    tok = lambda b, s: (b, s, 0)
    uprev, unext = _halo_specs(d_pool, tm, seq, nb)
    in_specs = [
        pl.BlockSpec((nb, tm, d), tok),
        pl.BlockSpec((nb, tm, d_pool), tok), uprev, unext,
        pl.BlockSpec((nb, tm, o.shape[-1]), tok),
        pl.BlockSpec((nb, tm, gates.shape[-1]), tok),
        gate1.spec(), shift2.spec(), scale2.spec(),
        _const_spec(n2.shape), _const_spec(w_pool.shape), _const_spec(pscale.shape),
        _const_spec(wpa.shape), _const_spec(wpb.shape), _const_spec(wout.shape), _const_spec(bands.shape),
    ]
    return pl.pallas_call(
        functools.partial(_mix_kernel, tm=tm, seq=seq),
        grid=(bsz // nb, seq // tm),
        in_specs=in_specs,
        out_specs=[pl.BlockSpec((nb, tm, d), tok), pl.BlockSpec((nb, tm, d), tok)],
        out_shape=[jax.ShapeDtypeStruct((bsz, seq, d), F32), jax.ShapeDtypeStruct((bsz, seq, d), BF16)],
        scratch_shapes=[pltpu.VMEM((nb * (tm + 2 * HALO), d_pool), BF16)],
        compiler_params=_params("parallel", "parallel"),
        name="mix",
    )(x, u, u, u, o, gates, gate1.array, shift2.array, scale2.array, n2, w_pool, pscale, wpa, wpb, wout, bands)


def _ffn_kernel(x1_ref, h2_ref, hprev_ref, hnext_ref, win_ref, cw_ref, cb_ref, wout_ref, gate2_ref, fnw_ref,
                y_ref, hext_ref, act_ref, *, tm, d_ff):
    nb, _, d = x1_ref.shape
    _fill_with_halo(hext_ref, hprev_ref, [h2_ref[b] for b in range(nb)], hnext_ref, tm)
    he = hext_ref[...]
    h2 = h2_ref[...].reshape(nb * tm, d)

    def tile_rows(t):
        parts = [t[b * (tm + 2 * HALO) + HALO:b * (tm + 2 * HALO) + HALO + tm] for b in range(nb)]
        return parts[0] if nb == 1 else jnp.concatenate(parts, axis=0)

    for c in range(d_ff // FFN_CHUNK):
        cols = slice(c * FFN_CHUNK, (c + 1) * FFN_CHUNK)
        a_ext = _dot(he, win_ref[:, cols])
        up = _dot(h2, win_ref[:, d_ff + c * FFN_CHUNK:d_ff + (c + 1) * FFN_CHUNK])
        rows = a_ext.shape[0]
        conv = (tile_rows(pltpu.roll(a_ext, 1, 0)) * cw_ref[0:1, cols]
                + tile_rows(a_ext) * cw_ref[1:2, cols]
                + tile_rows(pltpu.roll(a_ext, rows - 1, 0)) * cw_ref[2:3, cols]
                + cb_ref[:, cols])
        act_ref[:, cols] = (conv * jax.nn.sigmoid(conv) * up).astype(BF16)
    x2 = x1_ref[...].reshape(nb * tm, d) + gate2_ref[0, 0] * _dot(act_ref[...], wout_ref[...])
    ms = jnp.mean(x2 * x2, axis=-1, keepdims=True)
    y_ref[...] = (x2 * lax.rsqrt(ms + EPS) * fnw_ref[...]).reshape(nb, tm, d)


def _ffn_call(x1, h2, w_ffn_in, conv_w, conv_b, w_ffn_out, gate2, fnw, *, nb, tm):
    bsz, seq, d = x1.shape
    d_ff = w_ffn_out.shape[0]
    assert nb == 1 or (tm == seq and gate2.shared)
    tok = lambda b, s: (b, s, 0)
    hprev, hnext = _halo_specs(d, tm, seq, nb)
    in_specs = [
        pl.BlockSpec((nb, tm, d), tok),
        pl.BlockSpec((nb, tm, d), tok), hprev, hnext,
        _const_spec(w_ffn_in.shape), _const_spec(conv_w.shape), _const_spec(conv_b.shape),
        _const_spec(w_ffn_out.shape),
        gate2.spec(),
        _const_spec(fnw.shape),
    ]
    return pl.pallas_call(
        functools.partial(_ffn_kernel, tm=tm, d_ff=d_ff),
        grid=(bsz // nb, seq // tm),
        in_specs=in_specs,
        out_specs=pl.BlockSpec((nb, tm, d), tok),
        out_shape=jax.ShapeDtypeStruct((bsz, seq, d), F32),
        scratch_shapes=[pltpu.VMEM((nb * (tm + 2 * HALO), d), BF16), pltpu.VMEM((nb * tm, d_ff), BF16)],
        compiler_params=_params("parallel", "parallel"),
        name="ffn",
    )(x1, h2, h2, h2, w_ffn_in, conv_w, conv_b, w_ffn_out, gate2.array, fnw)


def _rope_tables(n_tok):
    rows = n_tok // GRID_W
    row = jnp.repeat(jnp.arange(rows, dtype=F32), GRID_W)
    col = jnp.tile(jnp.arange(GRID_W, dtype=F32), rows)
    n_freq = HEAD_DIM // 4
    inv = ROPE_BASE ** (-jnp.arange(n_freq, dtype=F32) / n_freq)
    ang_row = row[:, None] * inv
    ang_col = col[:, None] * inv
    zeros = jnp.zeros_like(ang_row)
    cos_row, sin_row = jnp.cos(ang_row), jnp.sin(ang_row)
    cos_col, sin_col = jnp.cos(ang_col), jnp.sin(ang_col)
    cos = jnp.concatenate([cos_row, cos_row, cos_col, cos_col], axis=1)
    sa = jnp.concatenate([-sin_row, zeros, -sin_col, zeros], axis=1)
    sb = jnp.concatenate([zeros, sin_row, zeros, sin_col], axis=1)
    rep = LANES // HEAD_DIM
    return tuple(jnp.tile(t, (1, rep)) for t in (cos, sa, sb))


def _stream(x, mods, w, lam, ctx_kv, *, lam_init, final_norm_w):
    shift1, scale1, gate1, shift2, scale2, gate2 = mods
    seq = x.shape[1]
    tm = min(seq, TOKEN_TILE)
    tq = min(seq, QUERY_TILE)
    is_ctx = ctx_kv is None
    nb = TOKEN_TILE // tm if (is_ctx and tm == seq) else 1
    d_pool = w["w_proj_a"].shape[0]
    d_attn = w["w_proj_b"].shape[0]
    rope_tabs = None if is_ctx else _rope_tables(seq)
    outs = _inproj_call(x, shift1, scale1, w["norm1_w"], w["w_in"], rope_tabs,
                        nb=nb, ts=tm, emit_f32=is_ctx, d_pool=d_pool, d_attn=d_attn)
    u, q, k, v, gates = outs[:5]
    kvs = [k, v] if is_ctx else [ctx_kv[0], ctx_kv[1], k, v]
    o = _attn_call(lam, w["subln_w"], q, kvs, nb=1, tq=tq, out_scale=1.0 - lam_init)
    x1, h2 = _mix_call(x, u, o, gates, gate1, shift2, scale2, w["norm2_w"], w["w_pool"], w["pool_scale"],
                       w["w_proj_a"], w["w_proj_b"], w["w_out"], nb=nb, tm=tm)
    y = _ffn_call(x1, h2, w["w_ffn_in"], w["ffn_conv_w"], w["ffn_conv_b"], w["w_ffn_out"], gate2,
                  final_norm_w, nb=nb, tm=tm)
    return y, outs[5:]


def kernel(x_prompt, x_sample, cache_k, cache_v, c, c_ctx, w_ada, b_ada, w_in, w_pool, pool_scale, w_proj_a, w_proj_b, w_out, lam_q1, lam_k1, lam_q2, lam_k2, subln_w, norm1_w, norm2_w, w_ffn_in, ffn_conv_w, ffn_conv_b, w_ffn_out, final_norm_w):
    assert w_ada.shape[0] == 1, "single trunk layer"
    bsz, seq, d = x_prompt.shape
    dec_b, dec_seq, _ = x_sample.shape
    assert 1 + dec_b <= MOD_ROWS
    lam_init = 0.8 - 0.6 * math.exp(-0.3 * 0)

    cc = jnp.zeros((MOD_ROWS, d), F32).at[0].set(c_ctx).at[1:1 + dec_b].set(c)
    lamv = jnp.concatenate([lam_q1, lam_k1, lam_q2, lam_k2], axis=0)
    mod, lam = _mod_call(cc, w_ada[0], b_ada, lamv, lam_init)
    mod = mod.reshape(MOD_ROWS, 6, 1, d)
    mods_ctx = [_ModRow(mod, j, 0, 0) for j in range(6)]
    mods_lat = [_ModRow(mod, j, 1, 1) for j in range(6)]

    w = dict(
        w_in=w_in[0].astype(BF16), w_pool=w_pool[0].astype(BF16), pool_scale=pool_scale,
        w_proj_a=w_proj_a[0].astype(BF16), w_proj_b=w_proj_b[0].astype(BF16), w_out=w_out[0].astype(BF16),
        subln_w=subln_w, norm1_w=norm1_w, norm2_w=norm2_w,
        w_ffn_in=w_ffn_in[0].astype(BF16), ffn_conv_w=ffn_conv_w[0], ffn_conv_b=ffn_conv_b,
        w_ffn_out=w_ffn_out[0].astype(BF16),
    )
    fnw = final_norm_w.reshape(1, d)

    y_prompt, (k32, v32) = _stream(x_prompt, mods_ctx, w, lam, None, lam_init=lam_init, final_norm_w=fnw)
    d_attn = k32.shape[-1]
    ctx_kv = (cache_k[:, 0].reshape(dec_b, -1, d_attn),
              jnp.swapaxes(cache_v[:, 0].reshape(dec_b, -1, d_attn), 1, 2))
    y_sample, _ = _stream(x_sample, mods_lat, w, lam, ctx_kv, lam_init=lam_init, final_norm_w=fnw)

    new_cache_k = k32.reshape(bsz, 1, seq, N_HEADS, 2, HEAD_DIM)
    new_cache_v = v32.reshape(bsz, 1, seq, N_HEADS, V_DIM)
    return (y_prompt, y_sample, new_cache_k, new_cache_v)
```

```python
import functools
import math
from typing import NamedTuple

import jax
import jax.numpy as jnp
from jax import lax
from jax.experimental import pallas as pl
from jax.experimental.pallas import tpu as pltpu

F32 = jnp.float32
BF16 = jnp.bfloat16

GRID_W = 64
N_HEADS = 8
HEAD_DIM = 64
V_DIM = 2 * HEAD_DIM
POOL_WINDOWS = (2, 4, 8, 16)
ROPE_BASE = 10000.0
EPS = 1e-6

LANES = 128
BF16_SUBLANES = 16
VMEM_LIMIT_BYTES = 56 * 1024 * 1024

MOD_ROWS = 16
MOD_TN = 3072
HALO = BF16_SUBLANES
FFN_CHUNK = 256
TOKEN_TILE = 512
QUERY_TILE = 512
QUERY_UNIT = 256
KEY_CHUNK = 256
POOL_BLOCK = 256


def _dot(a, b):
    return jnp.dot(a, b, preferred_element_type=F32)


def _const_spec(shape):
    zeros = (0,) * len(shape)
    return pl.BlockSpec(shape, lambda *_: zeros, pipeline_mode=pl.Buffered(1))


def _params(*sem):
    return pltpu.CompilerParams(dimension_semantics=sem, vmem_limit_bytes=VMEM_LIMIT_BYTES)


def _mod_kernel(cc_ref, w_ref, b_ref, lamv_ref, mod_ref, lam_ref, *, lam_init):
    c = cc_ref[...]
    sc = (c * jax.nn.sigmoid(c)).astype(BF16)
    mod_ref[...] = _dot(sc, w_ref[...].astype(BF16)) + b_ref[...]
    lv = lamv_ref[...]
    p1 = jnp.sum(lv[0:1] * lv[1:2], axis=-1, keepdims=True)
    p2 = jnp.sum(lv[2:3] * lv[3:4], axis=-1, keepdims=True)
    lam = jnp.exp(p1) - jnp.exp(p2) + lam_init
    lam_ref[...] = jnp.broadcast_to(lam, lam_ref.shape)


def _mod_call(cc, w_ada, b_ada, lamv, lam_init):
    d, n = w_ada.shape
    return pl.pallas_call(
        functools.partial(_mod_kernel, lam_init=lam_init),
        grid=(n // MOD_TN,),
        in_specs=[
            pl.BlockSpec((MOD_ROWS, d), lambda j: (0, 0)),
            pl.BlockSpec((d, MOD_TN), lambda j: (0, j)),
            pl.BlockSpec((1, MOD_TN), lambda j: (0, j)),
            pl.BlockSpec(lamv.shape, lambda j: (0, 0)),
        ],
        out_specs=[
            pl.BlockSpec((MOD_ROWS, MOD_TN), lambda j: (0, j)),
            pl.BlockSpec((8, LANES), lambda j: (0, 0)),
        ],
        out_shape=[
            jax.ShapeDtypeStruct((MOD_ROWS, n), F32),
            jax.ShapeDtypeStruct((8, LANES), F32),
        ],
        compiler_params=_params("arbitrary"),
        name="mod",
    )(cc, w_ada, b_ada, lamv)


def _rms_modulate(x, g, shift, scale):
    ms = jnp.mean(x * x, axis=-1, keepdims=True)
    return x * lax.rsqrt(ms + EPS) * g * (1.0 + scale) + shift


def _inproj_kernel(*refs, rope, emit_f32, d_pool, d_attn):
    x_ref, shift_ref, scale_ref, g_ref, w_ref = refs[:5]
    pos = 5
    if rope:
        cos_ref, sa_ref, sb_ref = refs[pos:pos + 3]
        pos += 3
    u_ref, q_ref, k_ref, v_ref, gt_ref = refs[pos:pos + 5]
    pos += 5
    if emit_f32:
        k32_ref, v32_ref = refs[pos:pos + 2]

    nb, ts, d = x_ref.shape
    rows = nb * ts
    x = x_ref[...].reshape(rows, d)
    hb = _rms_modulate(x, g_ref[...], shift_ref[0, 0], scale_ref[0, 0]).astype(BF16)

    def put(ref, val):
        ref[...] = val.reshape(nb, ts, val.shape[-1])

    def rotate(t):
        if not rope:
            return t
        cos, sa, sb = cos_ref[...], sa_ref[...], sb_ref[...]
        outs = []
        for c in range(t.shape[1] // LANES):
            tc = t[:, c * LANES:(c + 1) * LANES]
            outs.append(tc * cos + pltpu.roll(tc, LANES - HEAD_DIM // 4, 1) * sa
                        + pltpu.roll(tc, HEAD_DIM // 4, 1) * sb)
        return jnp.concatenate(outs, axis=1)

    o0 = 0
    put(u_ref, _dot(hb, w_ref[:, o0:o0 + d_pool]).astype(BF16))
    o0 += d_pool
    q = _dot(hb, w_ref[:, o0:o0 + d_attn])
    put(q_ref, (rotate(q) * (HEAD_DIM ** -0.5)).astype(BF16))
    o0 += d_attn
    k = _dot(hb, w_ref[:, o0:o0 + d_attn])
    if emit_f32:
        put(k32_ref, k)
    put(k_ref, rotate(k).astype(BF16))
    o0 += d_attn
    v = _dot(hb, w_ref[:, o0:o0 + d_attn])
    if emit_f32:
        put(v32_ref, v)
    v_t = v.T.astype(BF16)
    for b in range(nb):
        v_ref[b] = v_t[:, b * ts:(b + 1) * ts]
    o0 += d_attn
    put(gt_ref, _dot(hb, w_ref[:, o0:]).astype(BF16))


class _ModRow(NamedTuple):
    array: jax.Array
    kind: int
    row0: int
    per_batch: int

    @property
    def shared(self):
        return self.per_batch == 0

    def spec(self):
        block = (1, 1) + self.array.shape[2:]
        return pl.BlockSpec(block, lambda b, s: (self.row0 + self.per_batch * b, self.kind, 0, 0))


def _inproj_call(x, shift, scale, g, w_in, rope_tabs, *, nb, ts, emit_f32, d_pool, d_attn):
    bsz, seq, d = x.shape
    d_in = w_in.shape[1]
    d_gate = d_in - d_pool - 3 * d_attn
    rope = rope_tabs is not None
    assert nb == 1 or (shift.shared and not rope)
    tok = lambda b, s: (b, s, 0)
    in_specs = [
        pl.BlockSpec((nb, ts, d), tok),
        shift.spec(),
        scale.spec(),
        _const_spec((1, d)),
        _const_spec(w_in.shape),
    ]
    args = [x, shift.array, scale.array, g, w_in]
    if rope:
        in_specs += [pl.BlockSpec((ts, LANES), lambda b, s: (s, 0))] * 3
        args += list(rope_tabs)
    widths = [d_pool, d_attn, d_attn, d_attn, d_gate]
    out_specs = [pl.BlockSpec((nb, ts, w), tok) for w in widths]
    out_shape = [jax.ShapeDtypeStruct((bsz, seq, w), BF16) for w in widths]
    out_specs[3] = pl.BlockSpec((nb, d_attn, ts), lambda b, s: (b, 0, s))
    out_shape[3] = jax.ShapeDtypeStruct((bsz, d_attn, seq), BF16)
    if emit_f32:
        out_specs += [pl.BlockSpec((nb, ts, d_attn), tok)] * 2
        out_shape += [jax.ShapeDtypeStruct((bsz, seq, d_attn), F32)] * 2
    return pl.pallas_call(
        functools.partial(_inproj_kernel, rope=rope, emit_f32=emit_f32, d_pool=d_pool, d_attn=d_attn),
        grid=(bsz // nb, seq // ts),
        in_specs=in_specs,
        out_specs=out_specs,
        out_shape=out_shape,
        compiler_params=_params("parallel", "parallel"),
        name="in_proj_rope" if rope else "in_proj",
    )(*args)


def _query_unit(tq):
    return min(QUERY_UNIT, max(tq // 2, LANES))


def _attn_kernel(*refs, n_src, tq, out_scale):
    lam_ref, sub_ref, q_ref = refs[:3]
    kv_refs = refs[3:3 + 2 * n_src]
    o_ref = refs[3 + 2 * n_src]
    lam = lam_ref[0:1, 0:1]
    tu = _query_unit(tq)
    lane = lax.broadcasted_iota(jnp.int32, (tu, V_DIM), 1)
    chunks = []
    for i in range(n_src):
        n_keys = kv_refs[2 * i].shape[1]
        step = min(n_keys, KEY_CHUNK)
        chunks += [(i, r, r + step) for r in range(0, n_keys, step)]
    units = [(b, h, slice(j * tu, (j + 1) * tu))
             for b in range(q_ref.shape[0]) for h in range(N_HEADS) for j in range(tq // tu)]

    def head_cols(h):
        return slice(h * V_DIM, (h + 1) * V_DIM)

    def stacked_queries(unit):
        b, h, rows = unit
        q = q_ref[b, rows, head_cols(h)].astype(F32)
        return jnp.concatenate([jnp.where(lane < HEAD_DIM, q, 0.0),
                                jnp.where(lane >= HEAD_DIM, q, 0.0)], axis=0).astype(BF16)

    def score_chunk(unit, qq, chunk):
        i, r0, r1 = chunk
        k = kv_refs[2 * i][unit[0], r0:r1, head_cols(unit[1])].astype(BF16)
        return lax.dot_general(k, qq, (((1,), (1,)), ((), ())), preferred_element_type=F32)

    def value_chunk(unit, e, chunk):
        i, r0, r1 = chunk
        v_t = kv_refs[2 * i + 1][unit[0], head_cols(unit[1]), r0:r1].astype(BF16)
        ones_row = lax.broadcasted_iota(jnp.int32, (BF16_SUBLANES, r1 - r0), 0) == 0
        v_ext = jnp.concatenate([v_t, jnp.where(ones_row, 1.0, 0.0).astype(BF16)], axis=0)
        return _dot(v_ext, e)

    def column_max(scores):
        m = jnp.max(scores[0], axis=0, keepdims=True)
        for s in scores[1:]:
            m = jnp.maximum(m, jnp.max(s, axis=0, keepdims=True))
        return m

    def output_phase(unit, pv):
        b, h, rows = unit
        r = 1.0 / pv[V_DIM:V_DIM + 1, :]
        o_t = pv[:V_DIM, :tu] * r[:, :tu] - pv[:V_DIM, tu:] * (r[:, tu:] * lam)
        ms = jnp.mean(o_t * o_t, axis=0, keepdims=True)
        o = (o_t * lax.rsqrt(ms + EPS)).T
        o_ref[b, rows, head_cols(h)] = (o * sub_ref[...] * out_scale).astype(BF16)

    n_units = len(units)
    scores, exps = {}, {}
    for t in range(n_units + 2):
        u_exp, u_val = t - 1, t - 2
        do_score, do_exp, do_val = t < n_units, 0 <= u_exp < n_units, 0 <= u_val < n_units
        if do_score:
            qq = stacked_queries(units[t])
            scores[t] = []
        if do_exp:
            m = column_max(scores[u_exp])
            exps[u_exp] = []
        pv = None
        for ci, c in enumerate(chunks):
            if do_score:
                scores[t].append(score_chunk(units[t], qq, c))
            if do_exp:
                exps[u_exp].append(jnp.exp(scores[u_exp][ci] - m).astype(BF16))
            if do_val:
                part = value_chunk(units[u_val], exps[u_val][ci], c)
                pv = part if pv is None else pv + part
        if do_exp:
            del scores[u_exp]
        if do_val:
            del exps[u_val]
            output_phase(units[u_val], pv)


def _attn_call(lam, subln, q, kvs, *, nb, tq, out_scale):
    bsz, seq, d_attn = q.shape
    assert d_attn == N_HEADS * V_DIM
    in_specs = [
        _const_spec(lam.shape),
        _const_spec(subln.shape),
        pl.BlockSpec((nb, tq, d_attn), lambda b, i: (b, i, 0)),
    ]
    args = [lam, subln, q]
    for kv in kvs:
        in_specs.append(pl.BlockSpec((nb,) + kv.shape[1:], lambda b, i: (b, 0, 0)))
        args.append(kv)
    return pl.pallas_call(
        functools.partial(_attn_kernel, n_src=len(kvs) // 2, tq=tq, out_scale=out_scale),
        grid=(bsz // nb, seq // tq),
        in_specs=in_specs,
        out_specs=pl.BlockSpec((nb, tq, d_attn), lambda b, i: (b, i, 0)),
        out_shape=jax.ShapeDtypeStruct((bsz, seq, d_attn), BF16),
        compiler_params=_params("parallel", "parallel"),
        name="attn%d" % (len(kvs) // 2),
    )(*args)


def _fill_with_halo(dst_ref, prev_ref, mids, next_ref, tm):
    s = pl.program_id(1)
    last = pl.num_programs(1) - 1
    prev = prev_ref[0].astype(dst_ref.dtype)
    nxt = next_ref[0].astype(dst_ref.dtype)
    for b, mid in enumerate(mids):
        base = b * (tm + 2 * HALO)
        dst_ref[base:base + HALO] = jnp.where(s > 0, prev, jnp.zeros_like(prev))
        dst_ref[base + HALO:base + HALO + tm] = mid
        dst_ref[base + HALO + tm:base + 2 * HALO + tm] = jnp.where(s < last, nxt, jnp.zeros_like(nxt))


def _mix_kernel(x_ref, u_ref, uprev_ref, unext_ref, o_ref, gt_ref, gate1_ref, shift2_ref, scale2_ref,
                n2_ref, wpool_ref, pscale_ref, wpa_ref, wpb_ref, wout_ref, band_ref,
                x1_ref, h2_ref, pad_ref, *, tm, seq):
    nb, _, d = x_ref.shape
    rows = nb * tm
    _fill_with_halo(pad_ref, uprev_ref, [u_ref[b] for b in range(nb)], unext_ref, tm)
    uf = u_ref[...].reshape(rows, u_ref.shape[-1]).astype(F32)
    o = o_ref[...].reshape(rows, o_ref.shape[-1])
    t = pl.program_id(1) * tm + lax.rem(lax.broadcasted_iota(jnp.int32, (rows, 1), 0), tm)

    n_groups = len(POOL_WINDOWS)
    cols_per_group = d // n_groups
    block_starts = [b * (tm + 2 * HALO) + r for b in range(nb) for r in range(0, tm, POOL_BLOCK)]
    mixed, y_b = [], []
    for gi, w in enumerate(POOL_WINDOWS):
        cols = slice(gi * LANES, (gi + 1) * LANES)
        y_b.append(_dot(o, wpb_ref[:, gi * cols_per_group:(gi + 1) * cols_per_group]))
        sums = [_dot(band_ref[gi], pad_ref[r:r + POOL_BLOCK + 2 * HALO, cols]) for r in block_starts]
        acc = sums[0] if len(sums) == 1 else jnp.concatenate(sums, axis=0)
        lo = jnp.maximum(t - w // 2, 0)
        hi = jnp.minimum(t + (w - w // 2), seq)
        cnt = (hi - lo).astype(F32)
        pooled = acc / cnt - uf[:, cols]
        mixed.append(_dot(pooled.astype(BF16), wpool_ref[gi]) * pscale_ref[:, cols])
    mixed = jnp.concatenate(mixed, axis=1).astype(BF16)
    y_b = jnp.concatenate(y_b, axis=1)

    y_a = _dot(mixed, wpa_ref[...])
    g = jax.nn.sigmoid(gt_ref[...].reshape(rows, gt_ref.shape[-1]).astype(F32))
    merged = (g[:, :d] * y_a + g[:, d:] * y_b).astype(BF16)
    x1 = x_ref[...].reshape(rows, d) + gate1_ref[0, 0] * _dot(merged, wout_ref[...])
    x1_ref[...] = x1.reshape(nb, tm, d)
    h2 = _rms_modulate(x1, n2_ref[...], shift2_ref[0, 0], scale2_ref[0, 0]).astype(BF16)
    h2_ref[...] = h2.reshape(nb, tm, d)


def _halo_specs(width, tm, seq, nb=1):
    blocks_per_tile = tm // HALO
    last_block = seq // HALO - 1
    prev = pl.BlockSpec((1, HALO, width), lambda b, s: (b * nb, jnp.maximum(s * blocks_per_tile - 1, 0), 0))
    nxt = pl.BlockSpec((1, HALO, width),
                       lambda b, s: (b * nb, jnp.minimum((s + 1) * blocks_per_tile, last_block), 0))
    return prev, nxt


def _pool_bands():
    t = jnp.arange(POOL_BLOCK, dtype=jnp.int32)[:, None]
    j = jnp.arange(POOL_BLOCK + 2 * HALO, dtype=jnp.int32)[None, :]
    rel = j - HALO - t
    return jnp.stack([((rel >= -(w // 2)) & (rel < w - w // 2)).astype(BF16) for w in POOL_WINDOWS])


def _mix_call(x, u, o, gates, gate1, shift2, scale2, n2, w_pool, pscale, wpa, wpb, wout, *, nb, tm):
    bsz, seq, d = x.shape
    d_pool = u.shape[-1]
    assert tm % POOL_BLOCK == 0 and max(POOL_WINDOWS) // 2 <= HALO
    assert nb == 1 or (tm == seq and gate1.shared)
    bands = _pool_bands()
    tok = lambda b, s: (b, s, 0)
    uprev, unext = _halo_specs(d_pool, tm, seq, nb)
    in_specs = [
        pl.BlockSpec((nb, tm, d), tok),
        pl.BlockSpec((nb, tm, d_pool), tok), uprev, unext,
        pl.BlockSpec((nb, tm, o.shape[-1]), tok),
        pl.BlockSpec((nb, tm, gates.shape[-1]), tok),
        gate1.spec(), shift2.spec(), scale2.spec(),
        _const_spec(n2.shape), _const_spec(w_pool.shape), _const_spec(pscale.shape),
        _const_spec(wpa.shape), _const_spec(wpb.shape), _const_spec(wout.shape), _const_spec(bands.shape),
    ]
    return pl.pallas_call(
        functools.partial(_mix_kernel, tm=tm, seq=seq),
        grid=(bsz // nb, seq // tm),
        in_specs=in_specs,
        out_specs=[pl.BlockSpec((nb, tm, d), tok), pl.BlockSpec((nb, tm, d), tok)],
        out_shape=[jax.ShapeDtypeStruct((bsz, seq, d), F32), jax.ShapeDtypeStruct((bsz, seq, d), BF16)],
        scratch_shapes=[pltpu.VMEM((nb * (tm + 2 * HALO), d_pool), BF16)],
        compiler_params=_params("parallel", "parallel"),
        name="mix",
    )(x, u, u, u, o, gates, gate1.array, shift2.array, scale2.array, n2, w_pool, pscale, wpa, wpb, wout, bands)


def _ffn_kernel(x1_ref, h2_ref, hprev_ref, hnext_ref, win_ref, cw_ref, cb_ref, wout_ref, gate2_ref, fnw_ref,
                y_ref, hext_ref, act_ref, *, tm, d_ff):
    nb, _, d = x1_ref.shape
    _fill_with_halo(hext_ref, hprev_ref, [h2_ref[b] for b in range(nb)], hnext_ref, tm)
    he = hext_ref[...]
    h2 = h2_ref[...].reshape(nb * tm, d)

    def tile_rows(t):
        parts = [t[b * (tm + 2 * HALO) + HALO:b * (tm + 2 * HALO) + HALO + tm] for b in range(nb)]
        return parts[0] if nb == 1 else jnp.concatenate(parts, axis=0)

    for c in range(d_ff // FFN_CHUNK):
        cols = slice(c * FFN_CHUNK, (c + 1) * FFN_CHUNK)
        a_ext = _dot(he, win_ref[:, cols])
        up = _dot(h2, win_ref[:, d_ff + c * FFN_CHUNK:d_ff + (c + 1) * FFN_CHUNK])
        rows = a_ext.shape[0]
        conv = (tile_rows(pltpu.roll(a_ext, 1, 0)) * cw_ref[0:1, cols]
                + tile_rows(a_ext) * cw_ref[1:2, cols]
                + tile_rows(pltpu.roll(a_ext, rows - 1, 0)) * cw_ref[2:3, cols]
                + cb_ref[:, cols])
        act_ref[:, cols] = (conv * jax.nn.sigmoid(conv) * up).astype(BF16)
    x2 = x1_ref[...].reshape(nb * tm, d) + gate2_ref[0, 0] * _dot(act_ref[...], wout_ref[...])
    ms = jnp.mean(x2 * x2, axis=-1, keepdims=True)
    y_ref[...] = (x2 * lax.rsqrt(ms + EPS) * fnw_ref[...]).reshape(nb, tm, d)


def _ffn_call(x1, h2, w_ffn_in, conv_w, conv_b, w_ffn_out, gate2, fnw, *, nb, tm):
    bsz, seq, d = x1.shape
    d_ff = w_ffn_out.shape[0]
    assert nb == 1 or (tm == seq and gate2.shared)
    tok = lambda b, s: (b, s, 0)
    hprev, hnext = _halo_specs(d, tm, seq, nb)
    in_specs = [
        pl.BlockSpec((nb, tm, d), tok),
        pl.BlockSpec((nb, tm, d), tok), hprev, hnext,
        _const_spec(w_ffn_in.shape), _const_spec(conv_w.shape), _const_spec(conv_b.shape),
        _const_spec(w_ffn_out.shape),
        gate2.spec(),
        _const_spec(fnw.shape),
    ]
    return pl.pallas_call(
        functools.partial(_ffn_kernel, tm=tm, d_ff=d_ff),
        grid=(bsz // nb, seq // tm),
        in_specs=in_specs,
        out_specs=pl.BlockSpec((nb, tm, d), tok),
        out_shape=jax.ShapeDtypeStruct((bsz, seq, d), F32),
        scratch_shapes=[pltpu.VMEM((nb * (tm + 2 * HALO), d), BF16), pltpu.VMEM((nb * tm, d_ff), BF16)],
        compiler_params=_params("parallel", "parallel"),
        name="ffn",
    )(x1, h2, h2, h2, w_ffn_in, conv_w, conv_b, w_ffn_out, gate2.array, fnw)


def _rope_tables(n_tok):
    rows = n_tok // GRID_W
    row = jnp.repeat(jnp.arange(rows, dtype=F32), GRID_W)
    col = jnp.tile(jnp.arange(GRID_W, dtype=F32), rows)
    n_freq = HEAD_DIM // 4
    inv = ROPE_BASE ** (-jnp.arange(n_freq, dtype=F32) / n_freq)
    ang_row = row[:, None] * inv
    ang_col = col[:, None] * inv
    zeros = jnp.zeros_like(ang_row)
    cos_row, sin_row = jnp.cos(ang_row), jnp.sin(ang_row)
    cos_col, sin_col = jnp.cos(ang_col), jnp.sin(ang_col)
    cos = jnp.concatenate([cos_row, cos_row, cos_col, cos_col], axis=1)
    sa = jnp.concatenate([-sin_row, zeros, -sin_col, zeros], axis=1)
    sb = jnp.concatenate([zeros, sin_row, zeros, sin_col], axis=1)
    rep = LANES // HEAD_DIM
    return tuple(jnp.tile(t, (1, rep)) for t in (cos, sa, sb))


def _stream(x, mods, w, lam, ctx_kv, *, lam_init, final_norm_w):
    shift1, scale1, gate1, shift2, scale2, gate2 = mods
    seq = x.shape[1]
    tm = min(seq, TOKEN_TILE)
    tq = min(seq, QUERY_TILE)
    is_ctx = ctx_kv is None
    nb = TOKEN_TILE // tm if (is_ctx and tm == seq) else 1
    d_pool = w["w_proj_a"].shape[0]
    d_attn = w["w_proj_b"].shape[0]
    rope_tabs = None if is_ctx else _rope_tables(seq)
    outs = _inproj_call(x, shift1, scale1, w["norm1_w"], w["w_in"], rope_tabs,
                        nb=nb, ts=tm, emit_f32=is_ctx, d_pool=d_pool, d_attn=d_attn)
    u, q, k, v, gates = outs[:5]
    kvs = [k, v] if is_ctx else [ctx_kv[0], ctx_kv[1], k, v]
    o = _attn_call(lam, w["subln_w"], q, kvs, nb=1, tq=tq, out_scale=1.0 - lam_init)
    x1, h2 = _mix_call(x, u, o, gates, gate1, shift2, scale2, w["norm2_w"], w["w_pool"], w["pool_scale"],
                       w["w_proj_a"], w["w_proj_b"], w["w_out"], nb=nb, tm=tm)
    y = _ffn_call(x1, h2, w["w_ffn_in"], w["ffn_conv_w"], w["ffn_conv_b"], w["w_ffn_out"], gate2,
                  final_norm_w, nb=nb, tm=tm)
    return y, outs[5:]


def kernel(x_prompt, x_sample, cache_k, cache_v, c, c_ctx, w_ada, b_ada, w_in, w_pool, pool_scale, w_proj_a, w_proj_b, w_out, lam_q1, lam_k1, lam_q2, lam_k2, subln_w, norm1_w, norm2_w, w_ffn_in, ffn_conv_w, ffn_conv_b, w_ffn_out, final_norm_w):
    assert w_ada.shape[0] == 1, "single trunk layer"
    bsz, seq, d = x_prompt.shape
    dec_b, dec_seq, _ = x_sample.shape
    assert 1 + dec_b <= MOD_ROWS
    lam_init = 0.8 - 0.6 * math.exp(-0.3 * 0)

    cc = jnp.zeros((MOD_ROWS, d), F32).at[0].set(c_ctx).at[1:1 + dec_b].set(c)
    lamv = jnp.concatenate([lam_q1, lam_k1, lam_q2, lam_k2], axis=0)
    mod, lam = _mod_call(cc, w_ada[0], b_ada, lamv, lam_init)
    mod = mod.reshape(MOD_ROWS, 6, 1, d)
    mods_ctx = [_ModRow(mod, j, 0, 0) for j in range(6)]
    mods_lat = [_ModRow(mod, j, 1, 1) for j in range(6)]

    w = dict(
        w_in=w_in[0].astype(BF16), w_pool=w_pool[0].astype(BF16), pool_scale=pool_scale,
        w_proj_a=w_proj_a[0].astype(BF16), w_proj_b=w_proj_b[0].astype(BF16), w_out=w_out[0].astype(BF16),
        subln_w=subln_w, norm1_w=norm1_w, norm2_w=norm2_w,
        w_ffn_in=w_ffn_in[0].astype(BF16), ffn_conv_w=ffn_conv_w[0], ffn_conv_b=ffn_conv_b,
        w_ffn_out=w_ffn_out[0].astype(BF16),
    )
    fnw = final_norm_w.reshape(1, d)

    y_prompt, (k32, v32) = _stream(x_prompt, mods_ctx, w, lam, None, lam_init=lam_init, final_norm_w=fnw)
    d_attn = k32.shape[-1]
    ctx_kv = (cache_k[:, 0].reshape(dec_b, -1, d_attn),
              jnp.swapaxes(cache_v[:, 0].reshape(dec_b, -1, d_attn), 1, 2))
    y_sample, _ = _stream(x_sample, mods_lat, w, lam, ctx_kv, lam_init=lam_init, final_norm_w=fnw)

    new_cache_k = k32.reshape(bsz, 1, seq, N_HEADS, 2, HEAD_DIM)
    new_cache_v = v32.reshape(bsz, 1, seq, N_HEADS, V_DIM)
    return (y_prompt, y_sample, new_cache_k, new_cache_v)
```

```python
import functools
import math
from typing import NamedTuple

import jax
import jax.numpy as jnp
from jax import lax
from jax.experimental import pallas as pl
from jax.experimental.pallas import tpu as pltpu

F32 = jnp.float32
BF16 = jnp.bfloat16

GRID_W = 64
N_HEADS = 8
HEAD_DIM = 64
V_DIM = 2 * HEAD_DIM
POOL_WINDOWS = (2, 4, 8, 16)
ROPE_BASE = 10000.0
EPS = 1e-6

LANES = 128
BF16_SUBLANES = 16
VMEM_LIMIT_BYTES = 56 * 1024 * 1024

MOD_ROWS = 16
MOD_TN = 1536
HALO = BF16_SUBLANES
FFN_CHUNK = 256
TOKEN_TILE = 512
QUERY_TILE = 512
QUERY_UNIT = 256
KEY_CHUNK = 256
POOL_BLOCK = 256


def _dot(a, b):
    return jnp.dot(a, b, preferred_element_type=F32)


def _const_spec(shape):
    zeros = (0,) * len(shape)
    return pl.BlockSpec(shape, lambda *_: zeros, pipeline_mode=pl.Buffered(1))


def _params(*sem):
    return pltpu.CompilerParams(dimension_semantics=sem, vmem_limit_bytes=VMEM_LIMIT_BYTES)


def _mod_kernel(cc_ref, w_ref, b_ref, lamv_ref, mod_ref, lam_ref, *, lam_init):
    c = cc_ref[...]
    sc = (c * jax.nn.sigmoid(c)).astype(BF16)
    mod_ref[...] = _dot(sc, w_ref[...].astype(BF16)) + b_ref[...]
    lv = lamv_ref[...]
    p1 = jnp.sum(lv[0:1] * lv[1:2], axis=-1, keepdims=True)
    p2 = jnp.sum(lv[2:3] * lv[3:4], axis=-1, keepdims=True)
    lam = jnp.exp(p1) - jnp.exp(p2) + lam_init
    lam_ref[...] = jnp.broadcast_to(lam, lam_ref.shape)


def _mod_call(cc, w_ada, b_ada, lamv, lam_init):
    d, n = w_ada.shape
    return pl.pallas_call(
        functools.partial(_mod_kernel, lam_init=lam_init),
        grid=(n // MOD_TN,),
        in_specs=[
            pl.BlockSpec((MOD_ROWS, d), lambda j: (0, 0)),
            pl.BlockSpec((d, MOD_TN), lambda j: (0, j)),
            pl.BlockSpec((1, MOD_TN), lambda j: (0, j)),
            pl.BlockSpec(lamv.shape, lambda j: (0, 0)),
        ],
        out_specs=[
            pl.BlockSpec((MOD_ROWS, MOD_TN), lambda j: (0, j)),
            pl.BlockSpec((8, LANES), lambda j: (0, 0)),
        ],
        out_shape=[
            jax.ShapeDtypeStruct((MOD_ROWS, n), F32),
            jax.ShapeDtypeStruct((8, LANES), F32),
        ],
        compiler_params=_params("arbitrary"),
        name="mod",
    )(cc, w_ada, b_ada, lamv)


def _rms_modulate(x, g, shift, scale):
    ms = jnp.mean(x * x, axis=-1, keepdims=True)
    return x * lax.rsqrt(ms + EPS) * g * (1.0 + scale) + shift


def _inproj_kernel(*refs, rope, emit_f32, d_pool, d_attn):
    x_ref, shift_ref, scale_ref, g_ref, w_ref = refs[:5]
    pos = 5
    if rope:
        cos_ref, sa_ref, sb_ref = refs[pos:pos + 3]
        pos += 3
    u_ref, q_ref, k_ref, v_ref, gt_ref = refs[pos:pos + 5]
    pos += 5
    if emit_f32:
        k32_ref, v32_ref = refs[pos:pos + 2]

    nb, ts, d = x_ref.shape
    rows = nb * ts
    x = x_ref[...].reshape(rows, d)
    hb = _rms_modulate(x, g_ref[...], shift_ref[0, 0], scale_ref[0, 0]).astype(BF16)

    def put(ref, val):
        ref[...] = val.reshape(nb, ts, val.shape[-1])

    def rotate(t):
        if not rope:
            return t
        cos, sa, sb = cos_ref[...], sa_ref[...], sb_ref[...]
        outs = []
        for c in range(t.shape[1] // LANES):
            tc = t[:, c * LANES:(c + 1) * LANES]
            outs.append(tc * cos + pltpu.roll(tc, LANES - HEAD_DIM // 4, 1) * sa
                        + pltpu.roll(tc, HEAD_DIM // 4, 1) * sb)
        return jnp.concatenate(outs, axis=1)

    o0 = 0
    put(u_ref, _dot(hb, w_ref[:, o0:o0 + d_pool]).astype(BF16))
    o0 += d_pool
    q = _dot(hb, w_ref[:, o0:o0 + d_attn])
    put(q_ref, (rotate(q) * (HEAD_DIM ** -0.5)).astype(BF16))
    o0 += d_attn
    k = _dot(hb, w_ref[:, o0:o0 + d_attn])
    if emit_f32:
        put(k32_ref, k)
    put(k_ref, rotate(k).astype(BF16))
    o0 += d_attn
    v = _dot(hb, w_ref[:, o0:o0 + d_attn])
    if emit_f32:
        put(v32_ref, v)
    v_t = v.T.astype(BF16)
    for b in range(nb):
        v_ref[b] = v_t[:, b * ts:(b + 1) * ts]
    o0 += d_attn
    put(gt_ref, _dot(hb, w_ref[:, o0:]).astype(BF16))


class _ModRow(NamedTuple):
    array: jax.Array
    kind: int
    row0: int
    per_batch: int

    @property
    def shared(self):
        return self.per_batch == 0

    def spec(self):
        block = (1, 1) + self.array.shape[2:]
        return pl.BlockSpec(block, lambda b, s: (self.row0 + self.per_batch * b, self.kind, 0, 0))


def _inproj_call(x, shift, scale, g, w_in, rope_tabs, *, nb, ts, emit_f32, d_pool, d_attn):
    bsz, seq, d = x.shape
    d_in = w_in.shape[1]
    d_gate = d_in - d_pool - 3 * d_attn
    rope = rope_tabs is not None
    assert nb == 1 or (shift.shared and not rope)
    tok = lambda b, s: (b, s, 0)
    in_specs = [
        pl.BlockSpec((nb, ts, d), tok),
        shift.spec(),
        scale.spec(),
        _const_spec((1, d)),
        _const_spec(w_in.shape),
    ]
    args = [x, shift.array, scale.array, g, w_in]
    if rope:
        in_specs += [pl.BlockSpec((ts, LANES), lambda b, s: (s, 0))] * 3
        args += list(rope_tabs)
    widths = [d_pool, d_attn, d_attn, d_attn, d_gate]
    out_specs = [pl.BlockSpec((nb, ts, w), tok) for w in widths]
    out_shape = [jax.ShapeDtypeStruct((bsz, seq, w), BF16) for w in widths]
    out_specs[3] = pl.BlockSpec((nb, d_attn, ts), lambda b, s: (b, 0, s))
    out_shape[3] = jax.ShapeDtypeStruct((bsz, d_attn, seq), BF16)
    if emit_f32:
        out_specs += [pl.BlockSpec((nb, ts, d_attn), tok)] * 2
        out_shape += [jax.ShapeDtypeStruct((bsz, seq, d_attn), F32)] * 2
    return pl.pallas_call(
        functools.partial(_inproj_kernel, rope=rope, emit_f32=emit_f32, d_pool=d_pool, d_attn=d_attn),
        grid=(bsz // nb, seq // ts),
        in_specs=in_specs,
        out_specs=out_specs,
        out_shape=out_shape,
        compiler_params=_params("parallel", "parallel"),
        name="in_proj_rope" if rope else "in_proj",
    )(*args)


def _attn_kernel(*refs, n_src, tq, out_scale):
    lam_ref, sub_ref, q_ref = refs[:3]
    kv_refs = refs[3:3 + 2 * n_src]
    o_ref = refs[3 + 2 * n_src]
    lam = lam_ref[0:1, 0:1]
    tu = min(tq, QUERY_UNIT)
    lane = lax.broadcasted_iota(jnp.int32, (tu, V_DIM), 1)
    chunks = []
    for i in range(n_src):
        n_keys = kv_refs[2 * i].shape[1]
        step = min(n_keys, KEY_CHUNK)
        chunks += [(i, r, r + step) for r in range(0, n_keys, step)]
    units = [(b, h, slice(j * tu, (j + 1) * tu))
             for b in range(q_ref.shape[0]) for h in range(N_HEADS) for j in range(tq // tu)]

    def head_cols(h):
        return slice(h * V_DIM, (h + 1) * V_DIM)

    def stacked_queries(unit):
        b, h, rows = unit
        q = q_ref[b, rows, head_cols(h)].astype(F32)
        return jnp.concatenate([jnp.where(lane < HEAD_DIM, q, 0.0),
                                jnp.where(lane >= HEAD_DIM, q, 0.0)], axis=0).astype(BF16)

    def score_chunk(unit, qq, chunk):
        i, r0, r1 = chunk
        k = kv_refs[2 * i][unit[0], r0:r1, head_cols(unit[1])].astype(BF16)
        return lax.dot_general(k, qq, (((1,), (1,)), ((), ())), preferred_element_type=F32)

    def value_chunk(unit, e, chunk):
        i, r0, r1 = chunk
        v_t = kv_refs[2 * i + 1][unit[0], head_cols(unit[1]), r0:r1].astype(BF16)
        ones_row = lax.broadcasted_iota(jnp.int32, (BF16_SUBLANES, r1 - r0), 0) == 0
        v_ext = jnp.concatenate([v_t, jnp.where(ones_row, 1.0, 0.0).astype(BF16)], axis=0)
        return _dot(v_ext, e)

    def column_max(scores):
        m = jnp.max(scores[0], axis=0, keepdims=True)
        for s in scores[1:]:
            m = jnp.maximum(m, jnp.max(s, axis=0, keepdims=True))
        return m

    def output_phase(unit, pv):
        b, h, rows = unit
        r = 1.0 / pv[V_DIM:V_DIM + 1, :]
        o_t = pv[:V_DIM, :tu] * r[:, :tu] - pv[:V_DIM, tu:] * (r[:, tu:] * lam)
        ms = jnp.mean(o_t * o_t, axis=0, keepdims=True)
        o = (o_t * lax.rsqrt(ms + EPS)).T
        o_ref[b, rows, head_cols(h)] = (o * sub_ref[...] * out_scale).astype(BF16)

    n_units = len(units)
    scores, exps = {}, {}
    for t in range(n_units + 2):
        u_exp, u_val = t - 1, t - 2
        do_score, do_exp, do_val = t < n_units, 0 <= u_exp < n_units, 0 <= u_val < n_units
        if do_score:
            qq = stacked_queries(units[t])
            scores[t] = []
        if do_exp:
            m = column_max(scores[u_exp])
            exps[u_exp] = []
        pv = None
        for ci, c in enumerate(chunks):
            if do_score:
                scores[t].append(score_chunk(units[t], qq, c))
            if do_exp:
                exps[u_exp].append(jnp.exp(scores[u_exp][ci] - m).astype(BF16))
            if do_val:
                part = value_chunk(units[u_val], exps[u_val][ci], c)
                pv = part if pv is None else pv + part
        if do_exp:
            del scores[u_exp]
        if do_val:
            del exps[u_val]
            output_phase(units[u_val], pv)


def _attn_call(lam, subln, q, kvs, *, nb, tq, out_scale):
    bsz, seq, d_attn = q.shape
    assert d_attn == N_HEADS * V_DIM
    in_specs = [
        _const_spec(lam.shape),
        _const_spec(subln.shape),
        pl.BlockSpec((nb, tq, d_attn), lambda b, i: (b, i, 0)),
    ]
    args = [lam, subln, q]
    for kv in kvs:
        in_specs.append(pl.BlockSpec((nb,) + kv.shape[1:], lambda b, i: (b, 0, 0)))
        args.append(kv)
    return pl.pallas_call(
        functools.partial(_attn_kernel, n_src=len(kvs) // 2, tq=tq, out_scale=out_scale),
        grid=(bsz // nb, seq // tq),
        in_specs=in_specs,
        out_specs=pl.BlockSpec((nb, tq, d_attn), lambda b, i: (b, i, 0)),
        out_shape=jax.ShapeDtypeStruct((bsz, seq, d_attn), BF16),
        compiler_params=_params("parallel", "parallel"),
        name="attn%d" % (len(kvs) // 2),
    )(*args)


def _fill_with_halo(dst_ref, prev_ref, mids, next_ref, tm):
    s = pl.program_id(1)
    last = pl.num_programs(1) - 1
    prev = prev_ref[0].astype(dst_ref.dtype)
    nxt = next_ref[0].astype(dst_ref.dtype)
    for b, mid in enumerate(mids):
        base = b * (tm + 2 * HALO)
        dst_ref[base:base + HALO] = jnp.where(s > 0, prev, jnp.zeros_like(prev))
        dst_ref[base + HALO:base + HALO + tm] = mid
        dst_ref[base + HALO + tm:base + 2 * HALO + tm] = jnp.where(s < last, nxt, jnp.zeros_like(nxt))


def _mix_kernel(x_ref, u_ref, uprev_ref, unext_ref, o_ref, gt_ref, gate1_ref, shift2_ref, scale2_ref,
                n2_ref, wpool_ref, pscale_ref, wpa_ref, wpb_ref, wout_ref, band_ref,
                x1_ref, h2_ref, pad_ref, *, tm, seq):
    nb, _, d = x_ref.shape
    rows = nb * tm
    _fill_with_halo(pad_ref, uprev_ref, [u_ref[b] for b in range(nb)], unext_ref, tm)
    uf = u_ref[...].reshape(rows, u_ref.shape[-1]).astype(F32)
    o = o_ref[...].reshape(rows, o_ref.shape[-1])
    t = pl.program_id(1) * tm + lax.rem(lax.broadcasted_iota(jnp.int32, (rows, 1), 0), tm)

    n_groups = len(POOL_WINDOWS)
    cols_per_group = d // n_groups
    block_starts = [b * (tm + 2 * HALO) + r for b in range(nb) for r in range(0, tm, POOL_BLOCK)]
    mixed, y_b = [], []
    for gi, w in enumerate(POOL_WINDOWS):
        cols = slice(gi * LANES, (gi + 1) * LANES)
        y_b.append(_dot(o, wpb_ref[:, gi * cols_per_group:(gi + 1) * cols_per_group]))
        sums = [_dot(band_ref[gi], pad_ref[r:r + POOL_BLOCK + 2 * HALO, cols]) for r in block_starts]
        acc = sums[0] if len(sums) == 1 else jnp.concatenate(sums, axis=0)
        lo = jnp.maximum(t - w // 2, 0)
        hi = jnp.minimum(t + (w - w // 2), seq)
        cnt = (hi - lo).astype(F32)
        pooled = acc / cnt - uf[:, cols]
        mixed.append(_dot(pooled.astype(BF16), wpool_ref[gi]) * pscale_ref[:, cols])
    mixed = jnp.concatenate(mixed, axis=1).astype(BF16)
    y_b = jnp.concatenate(y_b, axis=1)

    y_a = _dot(mixed, wpa_ref[...])
    g = jax.nn.sigmoid(gt_ref[...].reshape(rows, gt_ref.shape[-1]).astype(F32))
    merged = (g[:, :d] * y_a + g[:, d:] * y_b).astype(BF16)
    x1 = x_ref[...].reshape(rows, d) + gate1_ref[0, 0] * _dot(merged, wout_ref[...])
    x1_ref[...] = x1.reshape(nb, tm, d)
    h2 = _rms_modulate(x1, n2_ref[...], shift2_ref[0, 0], scale2_ref[0, 0]).astype(BF16)
    h2_ref[...] = h2.reshape(nb, tm, d)


def _halo_specs(width, tm, seq, nb=1):
    blocks_per_tile = tm // HALO
    last_block = seq // HALO - 1
    prev = pl.BlockSpec((1, HALO, width), lambda b, s: (b * nb, jnp.maximum(s * blocks_per_tile - 1, 0), 0))
    nxt = pl.BlockSpec((1, HALO, width),
                       lambda b, s: (b * nb, jnp.minimum((s + 1) * blocks_per_tile, last_block), 0))
    return prev, nxt


def _pool_bands():
    t = jnp.arange(POOL_BLOCK, dtype=jnp.int32)[:, None]
    j = jnp.arange(POOL_BLOCK + 2 * HALO, dtype=jnp.int32)[None, :]
    rel = j - HALO - t
    return jnp.stack([((rel >= -(w // 2)) & (rel < w - w // 2)).astype(BF16) for w in POOL_WINDOWS])


def _mix_call(x, u, o, gates, gate1, shift2, scale2, n2, w_pool, pscale, wpa, wpb, wout, *, nb, tm):
    bsz, seq, d = x.shape
    d_pool = u.shape[-1]
    assert tm % POOL_BLOCK == 0 and max(POOL_WINDOWS) // 2 <= HALO
    assert nb == 1 or (tm == seq and gate1.shared)
    bands = _pool_bands()
    tok = lambda b, s: (b, s, 0)
    uprev, unext = _halo_specs(d_pool, tm, seq, nb)
    in_specs = [
        pl.BlockSpec((nb, tm, d), tok),
        pl.BlockSpec((nb, tm, d_pool), tok), uprev, unext,
        pl.BlockSpec((nb, tm, o.shape[-1]), tok),
        pl.BlockSpec((nb, tm, gates.shape[-1]), tok),
        gate1.spec(), shift2.spec(), scale2.spec(),
        _const_spec(n2.shape), _const_spec(w_pool.shape), _const_spec(pscale.shape),
        _const_spec(wpa.shape), _const_spec(wpb.shape), _const_spec(wout.shape), _const_spec(bands.shape),
    ]
    return pl.pallas_call(
        functools.partial(_mix_kernel, tm=tm, seq=seq),
        grid=(bsz // nb, seq // tm),
        in_specs=in_specs,
        out_specs=[pl.BlockSpec((nb, tm, d), tok), pl.BlockSpec((nb, tm, d), tok)],
        out_shape=[jax.ShapeDtypeStruct((bsz, seq, d), F32), jax.ShapeDtypeStruct((bsz, seq, d), BF16)],
        scratch_shapes=[pltpu.VMEM((nb * (tm + 2 * HALO), d_pool), BF16)],
        compiler_params=_params("parallel", "parallel"),
        name="mix",
    )(x, u, u, u, o, gates, gate1.array, shift2.array, scale2.array, n2, w_pool, pscale, wpa, wpb, wout, bands)


def _ffn_kernel(x1_ref, h2_ref, hprev_ref, hnext_ref, win_ref, cw_ref, cb_ref, wout_ref, gate2_ref, fnw_ref,
                y_ref, hext_ref, act_ref, *, tm, d_ff):
    nb, _, d = x1_ref.shape
    _fill_with_halo(hext_ref, hprev_ref, [h2_ref[b] for b in range(nb)], hnext_ref, tm)
    he = hext_ref[...]
    h2 = h2_ref[...].reshape(nb * tm, d)

    def tile_rows(t):
        parts = [t[b * (tm + 2 * HALO) + HALO:b * (tm + 2 * HALO) + HALO + tm] for b in range(nb)]
        return parts[0] if nb == 1 else jnp.concatenate(parts, axis=0)

    for c in range(d_ff // FFN_CHUNK):
        cols = slice(c * FFN_CHUNK, (c + 1) * FFN_CHUNK)
        a_ext = _dot(he, win_ref[:, cols])
        up = _dot(h2, win_ref[:, d_ff + c * FFN_CHUNK:d_ff + (c + 1) * FFN_CHUNK])
        rows = a_ext.shape[0]
        conv = (tile_rows(pltpu.roll(a_ext, 1, 0)) * cw_ref[0:1, cols]
                + tile_rows(a_ext) * cw_ref[1:2, cols]
                + tile_rows(pltpu.roll(a_ext, rows - 1, 0)) * cw_ref[2:3, cols]
                + cb_ref[:, cols])
        act_ref[:, cols] = (conv * jax.nn.sigmoid(conv) * up).astype(BF16)
    x2 = x1_ref[...].reshape(nb * tm, d) + gate2_ref[0, 0] * _dot(act_ref[...], wout_ref[...])
    ms = jnp.mean(x2 * x2, axis=-1, keepdims=True)
    y_ref[...] = (x2 * lax.rsqrt(ms + EPS) * fnw_ref[...]).reshape(nb, tm, d)


def _ffn_call(x1, h2, w_ffn_in, conv_w, conv_b, w_ffn_out, gate2, fnw, *, nb, tm):
    bsz, seq, d = x1.shape
    d_ff = w_ffn_out.shape[0]
    assert nb == 1 or (tm == seq and gate2.shared)
    tok = lambda b, s: (b, s, 0)
    hprev, hnext = _halo_specs(d, tm, seq, nb)
    in_specs = [
        pl.BlockSpec((nb, tm, d), tok),
        pl.BlockSpec((nb, tm, d), tok), hprev, hnext,
        _const_spec(w_ffn_in.shape), _const_spec(conv_w.shape), _const_spec(conv_b.shape),
        _const_spec(w_ffn_out.shape),
        gate2.spec(),
        _const_spec(fnw.shape),
    ]
    return pl.pallas_call(
        functools.partial(_ffn_kernel, tm=tm, d_ff=d_ff),
        grid=(bsz // nb, seq // tm),
        in_specs=in_specs,
        out_specs=pl.BlockSpec((nb, tm, d), tok),
        out_shape=jax.ShapeDtypeStruct((bsz, seq, d), F32),
        scratch_shapes=[pltpu.VMEM((nb * (tm + 2 * HALO), d), BF16), pltpu.VMEM((nb * tm, d_ff), BF16)],
        compiler_params=_params("parallel", "parallel"),
        name="ffn",
    )(x1, h2, h2, h2, w_ffn_in, conv_w, conv_b, w_ffn_out, gate2.array, fnw)


def _rope_tables(n_tok):
    rows = n_tok // GRID_W
    row = jnp.repeat(jnp.arange(rows, dtype=F32), GRID_W)
    col = jnp.tile(jnp.arange(GRID_W, dtype=F32), rows)
    n_freq = HEAD_DIM // 4
    inv = ROPE_BASE ** (-jnp.arange(n_freq, dtype=F32) / n_freq)
    ang_row = row[:, None] * inv
    ang_col = col[:, None] * inv
    zeros = jnp.zeros_like(ang_row)
    cos_row, sin_row = jnp.cos(ang_row), jnp.sin(ang_row)
    cos_col, sin_col = jnp.cos(ang_col), jnp.sin(ang_col)
    cos = jnp.concatenate([cos_row, cos_row, cos_col, cos_col], axis=1)
    sa = jnp.concatenate([-sin_row, zeros, -sin_col, zeros], axis=1)
    sb = jnp.concatenate([zeros, sin_row, zeros, sin_col], axis=1)
    rep = LANES // HEAD_DIM
    return tuple(jnp.tile(t, (1, rep)) for t in (cos, sa, sb))


def _stream(x, mods, w, lam, ctx_kv, *, lam_init, final_norm_w):
    shift1, scale1, gate1, shift2, scale2, gate2 = mods
    seq = x.shape[1]
    tm = min(seq, TOKEN_TILE)
    tq = min(seq, QUERY_TILE)
    is_ctx = ctx_kv is None
    nb = TOKEN_TILE // tm if (is_ctx and tm == seq) else 1
    d_pool = w["w_proj_a"].shape[0]
    d_attn = w["w_proj_b"].shape[0]
    rope_tabs = None if is_ctx else _rope_tables(seq)
    outs = _inproj_call(x, shift1, scale1, w["norm1_w"], w["w_in"], rope_tabs,
                        nb=nb, ts=tm, emit_f32=is_ctx, d_pool=d_pool, d_attn=d_attn)
    u, q, k, v, gates = outs[:5]
    kvs = [k, v] if is_ctx else [ctx_kv[0], ctx_kv[1], k, v]
    o = _attn_call(lam, w["subln_w"], q, kvs, nb=1, tq=tq, out_scale=1.0 - lam_init)
    x1, h2 = _mix_call(x, u, o, gates, gate1, shift2, scale2, w["norm2_w"], w["w_pool"], w["pool_scale"],
                       w["w_proj_a"], w["w_proj_b"], w["w_out"], nb=nb, tm=tm)
    y = _ffn_call(x1, h2, w["w_ffn_in"], w["ffn_conv_w"], w["ffn_conv_b"], w["w_ffn_out"], gate2,
                  final_norm_w, nb=nb, tm=tm)
    return y, outs[5:]


def kernel(x_prompt, x_sample, cache_k, cache_v, c, c_ctx, w_ada, b_ada, w_in, w_pool, pool_scale, w_proj_a, w_proj_b, w_out, lam_q1, lam_k1, lam_q2, lam_k2, subln_w, norm1_w, norm2_w, w_ffn_in, ffn_conv_w, ffn_conv_b, w_ffn_out, final_norm_w):
    assert w_ada.shape[0] == 1, "single trunk layer"
    bsz, seq, d = x_prompt.shape
    dec_b, dec_seq, _ = x_sample.shape
    assert 1 + dec_b <= MOD_ROWS
    lam_init = 0.8 - 0.6 * math.exp(-0.3 * 0)

    cc = jnp.zeros((MOD_ROWS, d), F32).at[0].set(c_ctx).at[1:1 + dec_b].set(c)
    lamv = jnp.concatenate([lam_q1, lam_k1, lam_q2, lam_k2], axis=0)
    mod, lam = _mod_call(cc, w_ada[0], b_ada, lamv, lam_init)
    mod = mod.reshape(MOD_ROWS, 6, 1, d)
    mods_ctx = [_ModRow(mod, j, 0, 0) for j in range(6)]
    mods_lat = [_ModRow(mod, j, 1, 1) for j in range(6)]

    w = dict(
        w_in=w_in[0].astype(BF16), w_pool=w_pool[0].astype(BF16), pool_scale=pool_scale,
        w_proj_a=w_proj_a[0].astype(BF16), w_proj_b=w_proj_b[0].astype(BF16), w_out=w_out[0].astype(BF16),
        subln_w=subln_w, norm1_w=norm1_w, norm2_w=norm2_w,
        w_ffn_in=w_ffn_in[0].astype(BF16), ffn_conv_w=ffn_conv_w[0], ffn_conv_b=ffn_conv_b,
        w_ffn_out=w_ffn_out[0].astype(BF16),
    )
    fnw = final_norm_w.reshape(1, d)

    y_prompt, (k32, v32) = _stream(x_prompt, mods_ctx, w, lam, None, lam_init=lam_init, final_norm_w=fnw)
    d_attn = k32.shape[-1]
    ctx_kv = (cache_k[:, 0].reshape(dec_b, -1, d_attn).astype(BF16),
              jnp.swapaxes(cache_v[:, 0].reshape(dec_b, -1, d_attn), 1, 2).astype(BF16))
    y_sample, _ = _stream(x_sample, mods_lat, w, lam, ctx_kv, lam_init=lam_init, final_norm_w=fnw)

    new_cache_k = k32.reshape(bsz, 1, seq, N_HEADS, 2, HEAD_DIM)
    new_cache_v = v32.reshape(bsz, 1, seq, N_HEADS, V_DIM)
    return (y_prompt, y_sample, new_cache_k, new_cache_v)
```

```python
import functools
import math

import jax
import jax.numpy as jnp
from jax import lax
from jax.experimental import pallas as pl
from jax.experimental.pallas import tpu as pltpu

F32 = jnp.float32
BF16 = jnp.bfloat16

GRID_W = 64
N_HEADS = 8
HEAD_DIM = 64
V_DIM = 2 * HEAD_DIM
POOL_WINDOWS = (2, 4, 8, 16)
ROPE_BASE = 10000.0
EPS = 1e-6

LANES = 128
BF16_SUBLANES = 16
VMEM_LIMIT_BYTES = 56 * 1024 * 1024

MOD_ROWS = 16
MOD_TN = 1536
HALO = BF16_SUBLANES
FFN_CHUNK = 256
TOKEN_TILE = 512
QUERY_TILE = 512
QUERY_UNIT = 256
KEY_CHUNK = 256
POOL_BLOCK = 256


def _dot(a, b):
    return jnp.dot(a, b, preferred_element_type=F32)


def _const_spec(shape):
    zeros = (0,) * len(shape)
    return pl.BlockSpec(shape, lambda *_: zeros, pipeline_mode=pl.Buffered(1))


def _params(*sem):
    return pltpu.CompilerParams(dimension_semantics=sem, vmem_limit_bytes=VMEM_LIMIT_BYTES)


def _mod_kernel(cc_ref, w_ref, b_ref, lamv_ref, mod_ref, lam_ref, *, lam_init):
    c = cc_ref[...]
    sc = (c * jax.nn.sigmoid(c)).astype(BF16)
    mod_ref[...] = _dot(sc, w_ref[...].astype(BF16)) + b_ref[...]
    lv = lamv_ref[...]
    p1 = jnp.sum(lv[0:1] * lv[1:2], axis=-1, keepdims=True)
    p2 = jnp.sum(lv[2:3] * lv[3:4], axis=-1, keepdims=True)
    lam = jnp.exp(p1) - jnp.exp(p2) + lam_init
    lam_ref[...] = jnp.broadcast_to(lam, lam_ref.shape)


def _mod_call(cc, w_ada, b_ada, lamv, lam_init):
    d, n = w_ada.shape
    return pl.pallas_call(
        functools.partial(_mod_kernel, lam_init=lam_init),
        grid=(n // MOD_TN,),
        in_specs=[
            pl.BlockSpec((MOD_ROWS, d), lambda j: (0, 0)),
            pl.BlockSpec((d, MOD_TN), lambda j: (0, j)),
            pl.BlockSpec((1, MOD_TN), lambda j: (0, j)),
            pl.BlockSpec(lamv.shape, lambda j: (0, 0)),
        ],
        out_specs=[
            pl.BlockSpec((MOD_ROWS, MOD_TN), lambda j: (0, j)),
            pl.BlockSpec((8, LANES), lambda j: (0, 0)),
        ],
        out_shape=[
            jax.ShapeDtypeStruct((MOD_ROWS, n), F32),
            jax.ShapeDtypeStruct((8, LANES), F32),
        ],
        compiler_params=_params("arbitrary"),
        name="mod",
    )(cc, w_ada, b_ada, lamv)


def _rms_modulate(x, g, shift, scale):
    ms = jnp.mean(x * x, axis=-1, keepdims=True)
    return x * lax.rsqrt(ms + EPS) * g * (1.0 + scale) + shift


def _inproj_kernel(*refs, rope, emit_f32, d_pool, d_attn):
    x_ref, shift_ref, scale_ref, g_ref, w_ref = refs[:5]
    pos = 5
    if rope:
        cos_ref, sa_ref, sb_ref = refs[pos:pos + 3]
        pos += 3
    u_ref, q_ref, k_ref, v_ref, gt_ref = refs[pos:pos + 5]
    pos += 5
    if emit_f32:
        k32_ref, v32_ref = refs[pos:pos + 2]

    nb, ts, d = x_ref.shape
    rows = nb * ts
    x = x_ref[...].reshape(rows, d)
    hb = _rms_modulate(x, g_ref[...], shift_ref[0], scale_ref[0]).astype(BF16)

    def put(ref, val):
        ref[...] = val.reshape(nb, ts, val.shape[-1])

    def rotate(t):
        if not rope:
            return t
        cos, sa, sb = cos_ref[...], sa_ref[...], sb_ref[...]
        outs = []
        for c in range(t.shape[1] // LANES):
            tc = t[:, c * LANES:(c + 1) * LANES]
            outs.append(tc * cos + pltpu.roll(tc, LANES - HEAD_DIM // 4, 1) * sa
                        + pltpu.roll(tc, HEAD_DIM // 4, 1) * sb)
        return jnp.concatenate(outs, axis=1)

    o0 = 0
    put(u_ref, _dot(hb, w_ref[:, o0:o0 + d_pool]).astype(BF16))
    o0 += d_pool
    q = _dot(hb, w_ref[:, o0:o0 + d_attn])
    put(q_ref, (rotate(q) * (HEAD_DIM ** -0.5)).astype(BF16))
    o0 += d_attn
    k = _dot(hb, w_ref[:, o0:o0 + d_attn])
    if emit_f32:
        put(k32_ref, k)
    put(k_ref, rotate(k).astype(BF16))
    o0 += d_attn
    v = _dot(hb, w_ref[:, o0:o0 + d_attn])
    if emit_f32:
        put(v32_ref, v)
    v_t = v.T.astype(BF16)
    for b in range(nb):
        v_ref[b] = v_t[:, b * ts:(b + 1) * ts]
    o0 += d_attn
    put(gt_ref, _dot(hb, w_ref[:, o0:]).astype(BF16))


def _row_spec(n_rows):
    if n_rows == 1:
        return lambda b, s: (0, 0, 0)
    return lambda b, s: (b, 0, 0)


def _inproj_call(x, shift, scale, g, w_in, rope_tabs, *, nb, ts, emit_f32, d_pool, d_attn):
    bsz, seq, d = x.shape
    d_in = w_in.shape[1]
    d_gate = d_in - d_pool - 3 * d_attn
    rope = rope_tabs is not None
    assert nb == 1 or (shift.shape[0] == 1 and not rope)
    tok = lambda b, s: (b, s, 0)
    in_specs = [
        pl.BlockSpec((nb, ts, d), tok),
        pl.BlockSpec((1, 1, d), _row_spec(shift.shape[0])),
        pl.BlockSpec((1, 1, d), _row_spec(scale.shape[0])),
        _const_spec((1, d)),
        _const_spec(w_in.shape),
    ]
    args = [x, shift, scale, g, w_in]
    if rope:
        in_specs += [pl.BlockSpec((ts, LANES), lambda b, s: (s, 0))] * 3
        args += list(rope_tabs)
    widths = [d_pool, d_attn, d_attn, d_attn, d_gate]
    out_specs = [pl.BlockSpec((nb, ts, w), tok) for w in widths]
    out_shape = [jax.ShapeDtypeStruct((bsz, seq, w), BF16) for w in widths]
    out_specs[3] = pl.BlockSpec((nb, d_attn, ts), lambda b, s: (b, 0, s))
    out_shape[3] = jax.ShapeDtypeStruct((bsz, d_attn, seq), BF16)
    if emit_f32:
        out_specs += [pl.BlockSpec((nb, ts, d_attn), tok)] * 2
        out_shape += [jax.ShapeDtypeStruct((bsz, seq, d_attn), F32)] * 2
    return pl.pallas_call(
        functools.partial(_inproj_kernel, rope=rope, emit_f32=emit_f32, d_pool=d_pool, d_attn=d_attn),
        grid=(bsz // nb, seq // ts),
        in_specs=in_specs,
        out_specs=out_specs,
        out_shape=out_shape,
        compiler_params=_params("parallel", "parallel"),
        name="in_proj_rope" if rope else "in_proj",
    )(*args)


def _attn_kernel(*refs, n_src, tq, out_scale):
    lam_ref, sub_ref, q_ref = refs[:3]
    kv_refs = refs[3:3 + 2 * n_src]
    o_ref = refs[3 + 2 * n_src]
    lam = lam_ref[0:1, 0:1]
    tu = min(tq, QUERY_UNIT)
    lane = lax.broadcasted_iota(jnp.int32, (tu, V_DIM), 1)
    chunks = []
    for i in range(n_src):
        n_keys = kv_refs[2 * i].shape[1]
        step = min(n_keys, KEY_CHUNK)
        chunks += [(i, r, r + step) for r in range(0, n_keys, step)]
    units = [(b, h, slice(j * tu, (j + 1) * tu))
             for b in range(q_ref.shape[0]) for h in range(N_HEADS) for j in range(tq // tu)]

    def head_cols(h):
        return slice(h * V_DIM, (h + 1) * V_DIM)

    def stacked_queries(unit):
        b, h, rows = unit
        q = q_ref[b, rows, head_cols(h)].astype(F32)
        return jnp.concatenate([jnp.where(lane < HEAD_DIM, q, 0.0),
                                jnp.where(lane >= HEAD_DIM, q, 0.0)], axis=0).astype(BF16)

    def score_chunk(unit, qq, chunk):
        i, r0, r1 = chunk
        k = kv_refs[2 * i][unit[0], r0:r1, head_cols(unit[1])].astype(BF16)
        return lax.dot_general(k, qq, (((1,), (1,)), ((), ())), preferred_element_type=F32)

    def value_chunk(unit, e, chunk):
        i, r0, r1 = chunk
        v_t = kv_refs[2 * i + 1][unit[0], head_cols(unit[1]), r0:r1].astype(BF16)
        ones_row = lax.broadcasted_iota(jnp.int32, (BF16_SUBLANES, r1 - r0), 0) == 0
        v_ext = jnp.concatenate([v_t, jnp.where(ones_row, 1.0, 0.0).astype(BF16)], axis=0)
        return _dot(v_ext, e)

    def column_max(scores):
        m = jnp.max(scores[0], axis=0, keepdims=True)
        for s in scores[1:]:
            m = jnp.maximum(m, jnp.max(s, axis=0, keepdims=True))
        return m

    def output_phase(unit, pv):
        b, h, rows = unit
        r = 1.0 / pv[V_DIM:V_DIM + 1, :]
        o_t = pv[:V_DIM, :tu] * r[:, :tu] - pv[:V_DIM, tu:] * (r[:, tu:] * lam)
        ms = jnp.mean(o_t * o_t, axis=0, keepdims=True)
        o = (o_t * lax.rsqrt(ms + EPS)).T
        o_ref[b, rows, head_cols(h)] = (o * sub_ref[...] * out_scale).astype(BF16)

    n_units = len(units)
    scores, exps = {}, {}
    for t in range(n_units + 2):
        u_exp, u_val = t - 1, t - 2
        do_score, do_exp, do_val = t < n_units, 0 <= u_exp < n_units, 0 <= u_val < n_units
        if do_score:
            qq = stacked_queries(units[t])
            scores[t] = []
        if do_exp:
            m = column_max(scores[u_exp])
            exps[u_exp] = []
        pv = None
        for ci, c in enumerate(chunks):
            if do_score:
                scores[t].append(score_chunk(units[t], qq, c))
            if do_exp:
                exps[u_exp].append(jnp.exp(scores[u_exp][ci] - m).astype(BF16))
            if do_val:
                part = value_chunk(units[u_val], exps[u_val][ci], c)
                pv = part if pv is None else pv + part
        if do_exp:
            del scores[u_exp]
        if do_val:
            del exps[u_val]
            output_phase(units[u_val], pv)


def _attn_call(lam, subln, q, kvs, *, nb, tq, out_scale):
    bsz, seq, d_attn = q.shape
    assert d_attn == N_HEADS * V_DIM
    in_specs = [
        _const_spec(lam.shape),
        _const_spec(subln.shape),
        pl.BlockSpec((nb, tq, d_attn), lambda b, i: (b, i, 0)),
    ]
    args = [lam, subln, q]
    for kv in kvs:
        in_specs.append(pl.BlockSpec((nb,) + kv.shape[1:], lambda b, i: (b, 0, 0)))
        args.append(kv)
    return pl.pallas_call(
        functools.partial(_attn_kernel, n_src=len(kvs) // 2, tq=tq, out_scale=out_scale),
        grid=(bsz // nb, seq // tq),
        in_specs=in_specs,
        out_specs=pl.BlockSpec((nb, tq, d_attn), lambda b, i: (b, i, 0)),
        out_shape=jax.ShapeDtypeStruct((bsz, seq, d_attn), BF16),
        compiler_params=_params("parallel", "parallel"),
        name="attn%d" % (len(kvs) // 2),
    )(*args)


def _fill_with_halo(dst_ref, prev_ref, mids, next_ref, tm):
    s = pl.program_id(1)
    last = pl.num_programs(1) - 1
    prev = prev_ref[0].astype(dst_ref.dtype)
    nxt = next_ref[0].astype(dst_ref.dtype)
    for b, mid in enumerate(mids):
        base = b * (tm + 2 * HALO)
        dst_ref[base:base + HALO] = jnp.where(s > 0, prev, jnp.zeros_like(prev))
        dst_ref[base + HALO:base + HALO + tm] = mid
        dst_ref[base + HALO + tm:base + 2 * HALO + tm] = jnp.where(s < last, nxt, jnp.zeros_like(nxt))


def _mix_kernel(x_ref, u_ref, uprev_ref, unext_ref, o_ref, gt_ref, gate1_ref, shift2_ref, scale2_ref,
                n2_ref, wpool_ref, pscale_ref, wpa_ref, wpb_ref, wout_ref, band_ref,
                x1_ref, h2_ref, pad_ref, *, tm, seq):
    nb, _, d = x_ref.shape
    rows = nb * tm
    _fill_with_halo(pad_ref, uprev_ref, [u_ref[b] for b in range(nb)], unext_ref, tm)
    uf = u_ref[...].reshape(rows, u_ref.shape[-1]).astype(F32)
    o = o_ref[...].reshape(rows, o_ref.shape[-1])
    t = pl.program_id(1) * tm + lax.rem(lax.broadcasted_iota(jnp.int32, (rows, 1), 0), tm)

    block_starts = [b * (tm + 2 * HALO) + r for b in range(nb) for r in range(0, tm, POOL_BLOCK)]
    y_b = _dot(o, wpb_ref[...])
    mixed = []
    for gi, w in enumerate(POOL_WINDOWS):
        cols = slice(gi * LANES, (gi + 1) * LANES)
        sums = [_dot(band_ref[gi], pad_ref[r:r + POOL_BLOCK + 2 * HALO, cols]) for r in block_starts]
        acc = sums[0] if len(sums) == 1 else jnp.concatenate(sums, axis=0)
        lo = jnp.maximum(t - w // 2, 0)
        hi = jnp.minimum(t + (w - w // 2), seq)
        cnt = (hi - lo).astype(F32)
        pooled = acc / cnt - uf[:, cols]
        mixed.append(_dot(pooled.astype(BF16), wpool_ref[gi]) * pscale_ref[:, cols])
    mixed = jnp.concatenate(mixed, axis=1).astype(BF16)

    y_a = _dot(mixed, wpa_ref[...])
    g = jax.nn.sigmoid(gt_ref[...].reshape(rows, gt_ref.shape[-1]).astype(F32))
    merged = (g[:, :d] * y_a + g[:, d:] * y_b).astype(BF16)
    x1 = x_ref[...].reshape(rows, d) + gate1_ref[0] * _dot(merged, wout_ref[...])
    x1_ref[...] = x1.reshape(nb, tm, d)
    h2 = _rms_modulate(x1, n2_ref[...], shift2_ref[0], scale2_ref[0]).astype(BF16)
    h2_ref[...] = h2.reshape(nb, tm, d)


def _halo_specs(width, tm, seq, nb=1):
    blocks_per_tile = tm // HALO
    last_block = seq // HALO - 1
    prev = pl.BlockSpec((1, HALO, width), lambda b, s: (b * nb, jnp.maximum(s * blocks_per_tile - 1, 0), 0))
    nxt = pl.BlockSpec((1, HALO, width),
                       lambda b, s: (b * nb, jnp.minimum((s + 1) * blocks_per_tile, last_block), 0))
    return prev, nxt


def _pool_bands():
    t = jnp.arange(POOL_BLOCK, dtype=jnp.int32)[:, None]
    j = jnp.arange(POOL_BLOCK + 2 * HALO, dtype=jnp.int32)[None, :]
    rel = j - HALO - t
    return jnp.stack([((rel >= -(w // 2)) & (rel < w - w // 2)).astype(BF16) for w in POOL_WINDOWS])


def _mix_call(x, u, o, gates, gate1, shift2, scale2, n2, w_pool, pscale, wpa, wpb, wout, *, nb, tm):
    bsz, seq, d = x.shape
    d_pool = u.shape[-1]
    assert tm % POOL_BLOCK == 0 and max(POOL_WINDOWS) // 2 <= HALO
    assert nb == 1 or (tm == seq and gate1.shape[0] == 1)
    bands = _pool_bands()
    tok = lambda b, s: (b, s, 0)
    uprev, unext = _halo_specs(d_pool, tm, seq, nb)
    in_specs = [
        pl.BlockSpec((nb, tm, d), tok),
        pl.BlockSpec((nb, tm, d_pool), tok), uprev, unext,
        pl.BlockSpec((nb, tm, o.shape[-1]), tok),
        pl.BlockSpec((nb, tm, gates.shape[-1]), tok),
        pl.BlockSpec((1, 1, d), _row_spec(gate1.shape[0])),
        pl.BlockSpec((1, 1, d), _row_spec(shift2.shape[0])),
        pl.BlockSpec((1, 1, d), _row_spec(scale2.shape[0])),
        _const_spec(n2.shape), _const_spec(w_pool.shape), _const_spec(pscale.shape),
        _const_spec(wpa.shape), _const_spec(wpb.shape), _const_spec(wout.shape), _const_spec(bands.shape),
    ]
    return pl.pallas_call(
        functools.partial(_mix_kernel, tm=tm, seq=seq),
        grid=(bsz // nb, seq // tm),
        in_specs=in_specs,
        out_specs=[pl.BlockSpec((nb, tm, d), tok), pl.BlockSpec((nb, tm, d), tok)],
        out_shape=[jax.ShapeDtypeStruct((bsz, seq, d), F32), jax.ShapeDtypeStruct((bsz, seq, d), BF16)],
        scratch_shapes=[pltpu.VMEM((nb * (tm + 2 * HALO), d_pool), BF16)],
        compiler_params=_params("parallel", "parallel"),
        name="mix",
    )(x, u, u, u, o, gates, gate1, shift2, scale2, n2, w_pool, pscale, wpa, wpb, wout, bands)


def _ffn_kernel(x1_ref, h2_ref, hprev_ref, hnext_ref, win_ref, cw_ref, cb_ref, wout_ref, gate2_ref, fnw_ref,
                y_ref, hext_ref, act_ref, *, tm, d_ff):
    nb, _, d = x1_ref.shape
    _fill_with_halo(hext_ref, hprev_ref, [h2_ref[b] for b in range(nb)], hnext_ref, tm)
    he = hext_ref[...]
    h2 = h2_ref[...].reshape(nb * tm, d)

    def tile_rows(t):
        parts = [t[b * (tm + 2 * HALO) + HALO:b * (tm + 2 * HALO) + HALO + tm] for b in range(nb)]
        return parts[0] if nb == 1 else jnp.concatenate(parts, axis=0)

    for c in range(d_ff // FFN_CHUNK):
        cols = slice(c * FFN_CHUNK, (c + 1) * FFN_CHUNK)
        a_ext = _dot(he, win_ref[:, cols])
        up = _dot(h2, win_ref[:, d_ff + c * FFN_CHUNK:d_ff + (c + 1) * FFN_CHUNK])
        rows = a_ext.shape[0]
        conv = (tile_rows(pltpu.roll(a_ext, 1, 0)) * cw_ref[0:1, cols]
                + tile_rows(a_ext) * cw_ref[1:2, cols]
                + tile_rows(pltpu.roll(a_ext, rows - 1, 0)) * cw_ref[2:3, cols]
                + cb_ref[:, cols])
        act_ref[:, cols] = (conv * jax.nn.sigmoid(conv) * up).astype(BF16)
    x2 = x1_ref[...].reshape(nb * tm, d) + gate2_ref[0] * _dot(act_ref[...], wout_ref[...])
    ms = jnp.mean(x2 * x2, axis=-1, keepdims=True)
    y_ref[...] = (x2 * lax.rsqrt(ms + EPS) * fnw_ref[...]).reshape(nb, tm, d)


def _ffn_call(x1, h2, w_ffn_in, conv_w, conv_b, w_ffn_out, gate2, fnw, *, nb, tm):
    bsz, seq, d = x1.shape
    d_ff = w_ffn_out.shape[0]
    assert nb == 1 or (tm == seq and gate2.shape[0] == 1)
    tok = lambda b, s: (b, s, 0)
    hprev, hnext = _halo_specs(d, tm, seq, nb)
    in_specs = [
        pl.BlockSpec((nb, tm, d), tok),
        pl.BlockSpec((nb, tm, d), tok), hprev, hnext,
        _const_spec(w_ffn_in.shape), _const_spec(conv_w.shape), _const_spec(conv_b.shape),
        _const_spec(w_ffn_out.shape),
        pl.BlockSpec((1, 1, d), _row_spec(gate2.shape[0])),
        _const_spec(fnw.shape),
    ]
    return pl.pallas_call(
        functools.partial(_ffn_kernel, tm=tm, d_ff=d_ff),
        grid=(bsz // nb, seq // tm),
        in_specs=in_specs,
        out_specs=pl.BlockSpec((nb, tm, d), tok),
        out_shape=jax.ShapeDtypeStruct((bsz, seq, d), F32),
        scratch_shapes=[pltpu.VMEM((nb * (tm + 2 * HALO), d), BF16), pltpu.VMEM((nb * tm, d_ff), BF16)],
        compiler_params=_params("parallel", "parallel"),
        name="ffn",
    )(x1, h2, h2, h2, w_ffn_in, conv_w, conv_b, w_ffn_out, gate2, fnw)


def _rope_tables(n_tok):
    rows = n_tok // GRID_W
    row = jnp.repeat(jnp.arange(rows, dtype=F32), GRID_W)
    col = jnp.tile(jnp.arange(GRID_W, dtype=F32), rows)
    n_freq = HEAD_DIM // 4
    inv = ROPE_BASE ** (-jnp.arange(n_freq, dtype=F32) / n_freq)
    ang_row = row[:, None] * inv
    ang_col = col[:, None] * inv
    zeros = jnp.zeros_like(ang_row)
    cos_row, sin_row = jnp.cos(ang_row), jnp.sin(ang_row)
    cos_col, sin_col = jnp.cos(ang_col), jnp.sin(ang_col)
    cos = jnp.concatenate([cos_row, cos_row, cos_col, cos_col], axis=1)
    sa = jnp.concatenate([-sin_row, zeros, -sin_col, zeros], axis=1)
    sb = jnp.concatenate([zeros, sin_row, zeros, sin_col], axis=1)
    rep = LANES // HEAD_DIM
    return tuple(jnp.tile(t, (1, rep)) for t in (cos, sa, sb))


def _stream(x, mods, w, lam, ctx_kv, *, lam_init, final_norm_w):
    shift1, scale1, gate1, shift2, scale2, gate2 = mods
    seq = x.shape[1]
    tm = min(seq, TOKEN_TILE)
    tq = min(seq, QUERY_TILE)
    is_ctx = ctx_kv is None
    nb = TOKEN_TILE // tm if (is_ctx and tm == seq) else 1
    d_pool = w["w_proj_a"].shape[0]
    d_attn = w["w_proj_b"].shape[0]
    rope_tabs = None if is_ctx else _rope_tables(seq)
    outs = _inproj_call(x, shift1, scale1, w["norm1_w"], w["w_in"], rope_tabs,
                        nb=nb, ts=tm, emit_f32=is_ctx, d_pool=d_pool, d_attn=d_attn)
    u, q, k, v, gates = outs[:5]
    kvs = [k, v] if is_ctx else [ctx_kv[0], ctx_kv[1], k, v]
    o = _attn_call(lam, w["subln_w"], q, kvs, nb=1, tq=tq, out_scale=1.0 - lam_init)
    x1, h2 = _mix_call(x, u, o, gates, gate1, shift2, scale2, w["norm2_w"], w["w_pool"], w["pool_scale"],
                       w["w_proj_a"], w["w_proj_b"], w["w_out"], nb=nb, tm=tm)
    y = _ffn_call(x1, h2, w["w_ffn_in"], w["ffn_conv_w"], w["ffn_conv_b"], w["w_ffn_out"], gate2,
                  final_norm_w, nb=nb, tm=tm)
    return y, outs[5:]


def kernel(x_prompt, x_sample, cache_k, cache_v, c, c_ctx, w_ada, b_ada, w_in, w_pool, pool_scale, w_proj_a, w_proj_b, w_out, lam_q1, lam_k1, lam_q2, lam_k2, subln_w, norm1_w, norm2_w, w_ffn_in, ffn_conv_w, ffn_conv_b, w_ffn_out, final_norm_w):
    assert w_ada.shape[0] == 1, "single trunk layer"
    bsz, seq, d = x_prompt.shape
    dec_b, dec_seq, _ = x_sample.shape
    assert 1 + dec_b <= MOD_ROWS
    lam_init = 0.8 - 0.6 * math.exp(-0.3 * 0)

    cc = jnp.zeros((MOD_ROWS, d), F32).at[0].set(c_ctx).at[1:1 + dec_b].set(c)
    lamv = jnp.concatenate([lam_q1, lam_k1, lam_q2, lam_k2], axis=0)
    mod, lam = _mod_call(cc, w_ada[0], b_ada, lamv, lam_init)
    mod = mod.reshape(MOD_ROWS, 6, 1, d)
    mods_ctx = [mod[0:1, j] for j in range(6)]
    mods_lat = [mod[1:1 + dec_b, j] for j in range(6)]

    w = dict(
        w_in=w_in[0].astype(BF16), w_pool=w_pool[0].astype(BF16), pool_scale=pool_scale,
        w_proj_a=w_proj_a[0].astype(BF16), w_proj_b=w_proj_b[0].astype(BF16), w_out=w_out[0].astype(BF16),
        subln_w=subln_w, norm1_w=norm1_w, norm2_w=norm2_w,
        w_ffn_in=w_ffn_in[0].astype(BF16), ffn_conv_w=ffn_conv_w[0], ffn_conv_b=ffn_conv_b,
        w_ffn_out=w_ffn_out[0].astype(BF16),
    )
    fnw = final_norm_w.reshape(1, d)

    y_prompt, (k32, v32) = _stream(x_prompt, mods_ctx, w, lam, None, lam_init=lam_init, final_norm_w=fnw)
    d_attn = k32.shape[-1]
    ctx_kv = (cache_k[:, 0].reshape(dec_b, -1, d_attn),
              jnp.swapaxes(cache_v[:, 0].reshape(dec_b, -1, d_attn), 1, 2))
    y_sample, _ = _stream(x_sample, mods_lat, w, lam, ctx_kv, lam_init=lam_init, final_norm_w=fnw)

    new_cache_k = k32.reshape(bsz, 1, seq, N_HEADS, 2, HEAD_DIM)
    new_cache_v = v32.reshape(bsz, 1, seq, N_HEADS, V_DIM)
    return (y_prompt, y_sample, new_cache_k, new_cache_v)
```
